```python
import jax, jax.numpy as jnp
from jax import lax
import numpy as np

D_MODEL = 4096
BATCH = 1
SEQ = 16384
DEPTH = 1
DEC_BATCH = 16
DEC_SEQ = 32
PAST_LEN = 4096

CHUNK = 64
HEAD_DIM = 128
N_HEADS_SB = 8
D_SB = N_HEADS_SB * HEAD_DIM
D_LRU = D_MODEL - D_SB
LRU_BLOCKS = 16
LRU_BLOCK_DIM = D_LRU // LRU_BLOCKS
LRU_CONV = 4
LRU_C = 8.0
D_MIX = D_LRU + D_SB
D_IN = 2 * D_LRU + 3 * D_SB
D_FF = 3 * D_MODEL
FFN_CONV = 3
SB_BLOCK = 128
SB_GROUP = 2048
PAD_POS = 2 ** 30
EPS = 1e-6

kernel_name = 'hymba_rglru_stickbreaking_convffn_step'


def rmsnorm(x, g):
    xf = x.astype(jnp.float32)
    y = xf * lax.rsqrt(jnp.mean(xf * xf, axis=-1, keepdims=True) + EPS)
    return (y * g.astype(jnp.float32)).astype(x.dtype)


def causal_dwconv(x, prev, w, b):
    width = w.shape[0]
    t = x.shape[1]
    xp = jnp.concatenate([prev.astype(x.dtype), x], axis=1)
    y = b + sum(xp[:, k:k + t] * w[k] for k in range(width))
    new_prev = xp[:, xp.shape[1] - (width - 1):]
    return y.astype(x.dtype), new_prev


def rg_lru(x, h0, w_a, b_a, w_x, b_x, lam):
    bsz, t, _ = x.shape
    xb = x.reshape(bsz, t, LRU_BLOCKS, LRU_BLOCK_DIM)
    r = jax.nn.sigmoid(jnp.einsum('bthi,hij->bthj', xb, w_a).reshape(bsz, t, D_LRU) + b_a)
    i = jax.nn.sigmoid(jnp.einsum('bthi,hij->bthj', xb, w_x).reshape(bsz, t, D_LRU) + b_x)
    log_a = (LRU_C * r.astype(jnp.float32)) * jax.nn.log_sigmoid(lam.astype(jnp.float32))
    a = jnp.exp(log_a)
    inp = jnp.sqrt(-jnp.expm1(2.0 * log_a)) * (i * x).astype(jnp.float32)
    inp = inp.at[:, 0].add(a[:, 0] * h0.astype(jnp.float32))

    def combine(c1, c2):
        a1, b1 = c1
        a2, b2 = c2
        return a1 * a2, a2 * b1 + b2

    _, h = lax.associative_scan(combine, (a, inp), axis=1)
    return h.astype(x.dtype), h[:, -1].astype(x.dtype)


def pad_keys(k, v, k_pos):
    n = k.shape[1]
    pad = (-n) % SB_BLOCK
    if pad == 0:
        return k, v, k_pos
    zk = jnp.zeros((k.shape[0], pad) + k.shape[2:], k.dtype)
    k = jnp.concatenate([k, zk], axis=1)
    v = jnp.concatenate([v, zk.astype(v.dtype)], axis=1)
    k_pos = jnp.concatenate([k_pos, jnp.full((pad,), PAD_POS, jnp.int32)])
    return k, v, k_pos


def sb_block(q, q_pos, k, v, k_pos):
    bsz, tq, h, _ = q.shape
    tk = k.shape[1]
    nkb = tk // SB_BLOCK
    z = jnp.einsum('bqhd,bkhd->bhqk', q, k).astype(jnp.float32) * (HEAD_DIM ** -0.5)
    mask = (k_pos[None, :] < q_pos[:, None])[None, None]
    log_1mb = jnp.where(mask, jax.nn.log_sigmoid(-z), 0.0)
    lb = log_1mb.reshape(bsz, h, tq, nkb, SB_BLOCK)
    tri_in = jnp.tri(SB_BLOCK, k=-1, dtype=jnp.float32)
    tri_out = jnp.tri(nkb, k=-1, dtype=jnp.float32)
    inner = jnp.einsum('bhqnj,js->bhqns', lb, tri_in)
    outer = jnp.einsum('bhqm,mn->bhqn', lb.sum(-1), tri_out)
    stick = (inner + outer[..., None]).reshape(bsz, h, tq, tk)
    w = jnp.where(mask, jnp.exp(z + log_1mb + stick), 0.0)
    return jnp.einsum('bhqk,bkhd->bqhd', w.astype(v.dtype), v)


def stick_breaking(q, k, v, q_pos, k_pos):
    bsz, t, h, dh = q.shape
    if t <= SB_BLOCK:
        kp, vp, pp = pad_keys(k, v, k_pos)
        return sb_block(q, q_pos, kp, vp, pp)
    past = k.shape[1] - t
    outs = []
    for start in range(0, t, SB_GROUP):
        size = min(SB_GROUP, t - start)
        end = past + start + size
        kp, vp, pp = pad_keys(k[:, :end], v[:, :end], k_pos[:end])
        nblk = size // SB_BLOCK
        qb = q[:, start:start + size].reshape(bsz, nblk, SB_BLOCK, h, dh).transpose(1, 0, 2, 3, 4)
        pb = q_pos[start:start + size].reshape(nblk, SB_BLOCK)
        out = lax.map(lambda qp: sb_block(qp[0], qp[1], kp, vp, pp), (qb, pb))
        outs.append(out.transpose(1, 0, 2, 3, 4).reshape(bsz, size, h, dh))
    return jnp.concatenate(outs, axis=1)


def layer(x, k_past, v_past, h0, conv_prev, ffn_prev, p):
    bsz, t, _ = x.shape
    past = k_past.shape[1]
    hn = rmsnorm(x, p['norm_mix_g'])
    proj = hn @ p['w_in']
    u, gate, q, k, v = jnp.split(
        proj, [D_LRU, 2 * D_LRU, 2 * D_LRU + D_SB, 2 * D_LRU + 2 * D_SB], axis=-1)
    u_conv, conv_state = causal_dwconv(u, conv_prev, p['lru_conv_w'], p['lru_conv_b'])
    h_lru, h_last = rg_lru(u_conv, h0, p['lru_w_a'], p['lru_b_a'], p['lru_w_x'],
                           p['lru_b_x'], p['lru_lambda'])
    y_lru = h_lru * jax.nn.gelu(gate)
    q = rmsnorm(q.reshape(bsz, t, N_HEADS_SB, HEAD_DIM), p['q_norm_g'])
    k = rmsnorm(k.reshape(bsz, t, N_HEADS_SB, HEAD_DIM), p['k_norm_g'])
    v = v.reshape(bsz, t, N_HEADS_SB, HEAD_DIM)
    k_all = jnp.concatenate([k_past.astype(k.dtype), k], axis=1)
    v_all = jnp.concatenate([v_past.astype(v.dtype), v], axis=1)
    q_pos = past + jnp.arange(t, dtype=jnp.int32)
    k_pos = jnp.arange(past + t, dtype=jnp.int32)
    y_sb = stick_breaking(q, k_all, v_all, q_pos, k_pos).reshape(bsz, t, D_SB)
    mix = jnp.concatenate([rmsnorm(y_lru, p['lru_out_g']), rmsnorm(y_sb, p['sb_out_g'])],
                          axis=-1) @ p['w_out']
    x = x + mix
    hn2 = rmsnorm(x, p['norm_ffn_g'])
    up = hn2 @ p['w_ffn_up']
    val, pre = jnp.split(up, [D_FF], axis=-1)
    pre_conv, ffn_state = causal_dwconv(pre, ffn_prev, p['ffn_conv_w'], p['ffn_conv_b'])
    x = x + (jax.nn.gelu(pre_conv) * val) @ p['w_ffn_down']
    return x, (k, v, h_last, conv_state, ffn_state)


def setup_inputs(seed: int = 0) -> dict:
    key = jax.random.key(seed)
    ks = jax.random.split(key, 32)
    f32 = jnp.float32
    nrm = lambda k, s, sc: jax.random.normal(k, s, f32) * sc
    u_init = jax.random.uniform(ks[14], (DEPTH, D_LRU), f32, 0.9, 0.999) ** (1.0 / LRU_C)
    return {
        'x_prompt': nrm(ks[0], (BATCH, SEQ, D_MODEL), 1.0),
        'x_sample': nrm(ks[1], (DEC_BATCH, DEC_SEQ, D_MODEL), 1.0),
        'cache_sb_k': nrm(ks[2], (DEPTH, DEC_BATCH, PAST_LEN, N_HEADS_SB, HEAD_DIM), 1.0),
        'cache_sb_v': nrm(ks[3], (DEPTH, DEC_BATCH, PAST_LEN, N_HEADS_SB, HEAD_DIM), 1.0),
        'state_lru_h': nrm(ks[4], (DEPTH, DEC_BATCH, D_LRU), 0.5),
        'state_lru_conv': nrm(ks[5], (DEPTH, DEC_BATCH, LRU_CONV - 1, D_LRU), 1.0),
        'state_ffn_conv': nrm(ks[6], (DEPTH, DEC_BATCH, FFN_CONV - 1, D_FF), 1.0),
        'norm_mix_g': 1.0 + nrm(ks[7], (DEPTH, D_MODEL), 0.01),
        'w_in': nrm(ks[8], (DEPTH, D_MODEL, D_IN), D_MODEL ** -0.5),
        'lru_conv_w': nrm(ks[9], (DEPTH, LRU_CONV, D_LRU), LRU_CONV ** -0.5),
        'lru_conv_b': nrm(ks[10], (DEPTH, D_LRU), 0.01),
        'lru_w_a': nrm(ks[11], (DEPTH, LRU_BLOCKS, LRU_BLOCK_DIM, LRU_BLOCK_DIM), LRU_BLOCK_DIM ** -0.5),
        'lru_b_a': nrm(ks[12], (DEPTH, D_LRU), 0.01),
        'lru_w_x': nrm(ks[13], (DEPTH, LRU_BLOCKS, LRU_BLOCK_DIM, LRU_BLOCK_DIM), LRU_BLOCK_DIM ** -0.5),
        'lru_b_x': nrm(ks[15], (DEPTH, D_LRU), 0.01),
        'lru_lambda': jnp.log(u_init) - jnp.log1p(-u_init),
        'q_norm_g': 1.0 + nrm(ks[16], (DEPTH, HEAD_DIM), 0.01),
        'k_norm_g': 1.0 + nrm(ks[17], (DEPTH, HEAD_DIM), 0.01),
        'lru_out_g': 1.0 + nrm(ks[18], (DEPTH, D_LRU), 0.01),
        'sb_out_g': 1.0 + nrm(ks[19], (DEPTH, D_SB), 0.01),
        'w_out': nrm(ks[20], (DEPTH, D_MIX, D_MODEL), D_MIX ** -0.5),
        'norm_ffn_g': 1.0 + nrm(ks[21], (DEPTH, D_MODEL), 0.01),
        'w_ffn_up': nrm(ks[22], (DEPTH, D_MODEL, 2 * D_FF), D_MODEL ** -0.5),
        'ffn_conv_w': nrm(ks[23], (DEPTH, FFN_CONV, D_FF), FFN_CONV ** -0.5),
        'ffn_conv_b': nrm(ks[24], (DEPTH, D_FF), 0.01),
        'w_ffn_down': nrm(ks[25], (DEPTH, D_FF, D_MODEL), D_FF ** -0.5),
    }


def reference(x_prompt, x_sample, cache_sb_k, cache_sb_v, state_lru_h, state_lru_conv,
              state_ffn_conv, norm_mix_g, w_in, lru_conv_w, lru_conv_b, lru_w_a, lru_b_a,
              lru_w_x, lru_b_x, lru_lambda, q_norm_g, k_norm_g, lru_out_g, sb_out_g, w_out,
              norm_ffn_g, w_ffn_up, ffn_conv_w, ffn_conv_b, w_ffn_down):
    bp = x_prompt.shape[0]
    dt = x_prompt.dtype
    xp, xs = x_prompt, x_sample
    st_p, st_s = [], []
    for l in range(DEPTH):
        p = {
            'norm_mix_g': norm_mix_g[l], 'w_in': w_in[l],
            'lru_conv_w': lru_conv_w[l], 'lru_conv_b': lru_conv_b[l],
            'lru_w_a': lru_w_a[l], 'lru_b_a': lru_b_a[l],
            'lru_w_x': lru_w_x[l], 'lru_b_x': lru_b_x[l], 'lru_lambda': lru_lambda[l],
            'q_norm_g': q_norm_g[l], 'k_norm_g': k_norm_g[l],
            'lru_out_g': lru_out_g[l], 'sb_out_g': sb_out_g[l], 'w_out': w_out[l],
            'norm_ffn_g': norm_ffn_g[l], 'w_ffn_up': w_ffn_up[l],
            'ffn_conv_w': ffn_conv_w[l], 'ffn_conv_b': ffn_conv_b[l],
            'w_ffn_down': w_ffn_down[l],
        }
        empty_kv = jnp.zeros((bp, 0, N_HEADS_SB, HEAD_DIM), dt)
        xp, sp = layer(xp, empty_kv, empty_kv,
                       jnp.zeros((bp, D_LRU), dt),
                       jnp.zeros((bp, LRU_CONV - 1, D_LRU), dt),
                       jnp.zeros((bp, FFN_CONV - 1, D_FF), dt), p)
        st_p.append(sp)
        xs, ss = layer(xs, cache_sb_k[l], cache_sb_v[l], state_lru_h[l], state_lru_conv[l],
                       state_ffn_conv[l], p)
        st_s.append(ss)
    new_sb_k_prompt = jnp.stack([s[0] for s in st_p])
    new_sb_v_prompt = jnp.stack([s[1] for s in st_p])
    new_lru_h_prompt = jnp.stack([s[2] for s in st_p])
    new_lru_conv_prompt = jnp.stack([s[3] for s in st_p])
    new_ffn_conv_prompt = jnp.stack([s[4] for s in st_p])
    new_sb_k_sample = jnp.stack([s[0] for s in st_s])
    new_sb_v_sample = jnp.stack([s[1] for s in st_s])
    new_lru_h_sample = jnp.stack([s[2] for s in st_s])
    new_lru_conv_sample = jnp.stack([s[3] for s in st_s])
    new_ffn_conv_sample = jnp.stack([s[4] for s in st_s])
    return (xp, xs, new_sb_k_prompt, new_sb_v_prompt, new_lru_h_prompt, new_lru_conv_prompt,
            new_ffn_conv_prompt, new_sb_k_sample, new_sb_v_sample, new_lru_h_sample,
            new_lru_conv_sample, new_ffn_conv_sample)
```

```python
import functools

import jax
import jax.numpy as jnp
from jax import lax
from jax.experimental import pallas as pl
from jax.experimental.pallas import tpu as pltpu

F32 = jnp.float32
BF16 = jnp.bfloat16

D_MODEL = 4096
HEAD_DIM = 128
N_HEADS = 8
D_SB = N_HEADS * HEAD_DIM
D_LRU = D_MODEL - D_SB
LRU_BLOCKS = 16
LRU_BLOCK_DIM = D_LRU // LRU_BLOCKS
LRU_PAIR = 2 * LRU_BLOCK_DIM
N_PAIRS = LRU_BLOCKS // 2
LRU_CONV = 4
LRU_C = 8.0
D_IN = 2 * D_LRU + 3 * D_SB
D_FF = 3 * D_MODEL
FFN_CONV = 3
EPS = 1e-6
SCALE = HEAD_DIM ** -0.5
HALO = 8

VMEM_LIMIT = 56 * 1024 * 1024


def _params(*sem):
    return pltpu.CompilerParams(dimension_semantics=sem, vmem_limit_bytes=VMEM_LIMIT)


def _rms(x, g):
    ms = jnp.mean(x * x, axis=-1, keepdims=True)
    return (x * lax.rsqrt(ms + EPS)) * g


def _gelu(x):
    return x * (0.5 * (1.0 + jnp.tanh(0.7978845608028654 * (x + 0.044715 * (x * x * x)))))


def _sigmoid(x):
    return 1.0 / (1.0 + jnp.exp(-x))


def _softplus(x):
    return jnp.maximum(x, 0.0) + jnp.log1p(jnp.exp(-jnp.abs(x)))


def _norm_matmul_body(x_ref, g_ref, w_ref, o_ref, hn_ref):
    @pl.when(pl.program_id(1) == 0)
    def _():
        hn_ref[...] = _rms(x_ref[...], g_ref[...]).astype(BF16)

    o_ref[...] = jnp.dot(hn_ref[...], w_ref[...], preferred_element_type=F32)


def _norm_matmul(x, g, w, tm, tn, name):
    m, k = x.shape
    n = w.shape[1]
    return pl.pallas_call(
        _norm_matmul_body,
        grid=(m // tm, n // tn),
        in_specs=[
            pl.BlockSpec((tm, k), lambda i, j: (i, 0)),
            pl.BlockSpec((1, k), lambda i, j: (0, 0)),
            pl.BlockSpec((k, tn), lambda i, j: (0, j)),
        ],
        out_specs=pl.BlockSpec((tm, tn), lambda i, j: (i, j)),
        out_shape=jax.ShapeDtypeStruct((m, n), F32),
        scratch_shapes=[pltpu.VMEM((tm, k), BF16)],
        compiler_params=_params("parallel", "arbitrary"),
        name=name,
    )(x, g, w)


def _outproj_body(a_ref, y_ref, gsb_ref, w1_ref, w2_ref, res_ref, o_ref, yn_ref):
    @pl.when(pl.program_id(1) == 0)
    def _():
        yn_ref[...] = _rms(y_ref[...], gsb_ref[...]).astype(BF16)

    acc = jnp.dot(a_ref[...], w1_ref[...], preferred_element_type=F32)
    acc = acc + jnp.dot(yn_ref[...], w2_ref[...], preferred_element_type=F32)
    o_ref[...] = res_ref[...] + acc


def _outproj(a, y_sb, g_sb, w_out, res, tm, tn):
    m = a.shape[0]
    n = w_out.shape[1]
    return pl.pallas_call(
        _outproj_body,
        grid=(m // tm, n // tn),
        in_specs=[
            pl.BlockSpec((tm, D_LRU), lambda i, j: (i, 0)),
            pl.BlockSpec((tm, D_SB), lambda i, j: (i, 0)),
            pl.BlockSpec((1, D_SB), lambda i, j: (0, 0)),
            pl.BlockSpec((D_LRU, tn), lambda i, j: (0, j)),
            pl.BlockSpec((D_SB, tn), lambda i, j: (D_LRU // D_SB, j)),
            pl.BlockSpec((tm, tn), lambda i, j: (i, j)),
        ],
        out_specs=pl.BlockSpec((tm, tn), lambda i, j: (i, j)),
        out_shape=jax.ShapeDtypeStruct((m, n), F32),
        scratch_shapes=[pltpu.VMEM((tm, D_SB), BF16)],
        compiler_params=_params("parallel", "arbitrary"),
        name="out_proj",
    )(a, y_sb, g_sb, w_out, w_out, res)


def _down_body(g_ref, w_ref, res_ref, o_ref, acc_ref):
    k = pl.program_id(2)

    @pl.when(k == 0)
    def _():
        acc_ref[...] = jnp.zeros_like(acc_ref)

    acc_ref[...] += jnp.dot(g_ref[...], w_ref[...], preferred_element_type=F32)

    @pl.when(k == pl.num_programs(2) - 1)
    def _():
        o_ref[...] = res_ref[...] + acc_ref[...]


def _down(g, w, res, tm, tn, tk):
    m, kdim = g.shape
    n = w.shape[1]
    return pl.pallas_call(
        _down_body,
        grid=(m // tm, n // tn, kdim // tk),
        in_specs=[
            pl.BlockSpec((tm, tk), lambda i, j, k: (i, k)),
            pl.BlockSpec((tk, tn), lambda i, j, k: (k, j)),
            pl.BlockSpec((tm, tn), lambda i, j, k: (i, j)),
        ],
        out_specs=pl.BlockSpec((tm, tn), lambda i, j, k: (i, j)),
        out_shape=jax.ShapeDtypeStruct((m, n), F32),
        scratch_shapes=[pltpu.VMEM((tm, tn), F32)],
        compiler_params=_params("parallel", "parallel", "arbitrary"),
        name="ffn_down",
    )(g, w, res)


def _lru_body(u_ref, gate_ref, h0_ref, cprev_ref, cw_ref, cb_ref, wa_ref, ba_ref, wx_ref, bx_ref,
              lam_ref, og_ref, y_ref, hlast_ref, cstate_ref,
              ext_ref, uc_ref, a_ref, b_ref, hs_ref, hc_ref, *, tt):
    @pl.when(pl.program_id(1) == 0)
    def _():
        hc_ref[...] = h0_ref[0]
        ext_ref[0:HALO, :] = cprev_ref[0]

    u = u_ref[...]
    ext_ref[HALO:HALO + tt, :] = u
    cw = cw_ref[...]
    uc = cb_ref[...]
    for k in range(LRU_CONV - 1):
        lo = HALO - (LRU_CONV - 1) + k
        uc = uc + ext_ref[lo:lo + tt, :] * cw[k:k + 1, :]
    uc_ref[...] = uc + u * cw[LRU_CONV - 1:LRU_CONV, :]

    last = ext_ref[tt:tt + HALO, :]
    cstate_ref[0] = last
    ext_ref[0:HALO, :] = last

    for p in range(N_PAIRS):
        sl = slice(p * LRU_PAIR, (p + 1) * LRU_PAIR)
        ucp = uc_ref[:, sl]
        xb = ucp.astype(BF16)
        r = _sigmoid(jnp.dot(xb, wa_ref[p], preferred_element_type=F32) + ba_ref[:, sl])
        i = _sigmoid(jnp.dot(xb, wx_ref[p], preferred_element_type=F32) + bx_ref[:, sl])
        log_a = (LRU_C * r) * (-_softplus(-lam_ref[:, sl]))
        a = jnp.exp(log_a)
        a_ref[:, sl] = a
        b_ref[:, sl] = jnp.sqrt(-jnp.tanh(log_a) * (a * a + 1.0)) * (i * ucp)

    def step(t, h):
        h = a_ref[pl.ds(t, 1), :] * h + b_ref[pl.ds(t, 1), :]
        hs_ref[pl.ds(t, 1), :] = h
        return h

    h = lax.fori_loop(0, tt, step, hc_ref[...], unroll=8)
    hc_ref[...] = h
    hlast_ref[0] = h

    y = hs_ref[...] * _gelu(gate_ref[...])
    y_ref[...] = _rms(y, og_ref[...]).astype(BF16)


def _lru(proj, h0, cprev, wts, nseq, seqlen, tt):
    m = nseq * seqlen
    nt = seqlen // tt
    row = lambda b, t: (b * nt + t, 0)
    vec = lambda b, t: (0, 0)
    return pl.pallas_call(
        functools.partial(_lru_body, tt=tt),
        grid=(nseq, nt),
        in_specs=[
            pl.BlockSpec((tt, D_LRU), row),
            pl.BlockSpec((tt, D_LRU), lambda b, t: (b * nt + t, 1)),
            pl.BlockSpec((1, 1, D_LRU), lambda b, t: (b, 0, 0)),
            pl.BlockSpec((1, HALO, D_LRU), lambda b, t: (b, 0, 0)),
            pl.BlockSpec((HALO, D_LRU), vec),
            pl.BlockSpec((1, D_LRU), vec),
            pl.BlockSpec((N_PAIRS, LRU_PAIR, LRU_PAIR), lambda b, t: (0, 0, 0)),
            pl.BlockSpec((1, D_LRU), vec),
            pl.BlockSpec((N_PAIRS, LRU_PAIR, LRU_PAIR), lambda b, t: (0, 0, 0)),
            pl.BlockSpec((1, D_LRU), vec),
            pl.BlockSpec((1, D_LRU), vec),
            pl.BlockSpec((1, D_LRU), vec),
        ],
        out_specs=[
            pl.BlockSpec((tt, D_LRU), row),
            pl.BlockSpec((1, 1, D_LRU), lambda b, t: (b, 0, 0)),
            pl.BlockSpec((1, HALO, D_LRU), lambda b, t: (b, 0, 0)),
        ],
        out_shape=[
            jax.ShapeDtypeStruct((m, D_LRU), BF16),
            jax.ShapeDtypeStruct((nseq, 1, D_LRU), F32),
            jax.ShapeDtypeStruct((nseq, HALO, D_LRU), F32),
        ],
        scratch_shapes=[
            pltpu.VMEM((tt + HALO, D_LRU), F32),
            pltpu.VMEM((tt, D_LRU), F32),
            pltpu.VMEM((tt, D_LRU), F32),
            pltpu.VMEM((tt, D_LRU), F32),
            pltpu.VMEM((tt, D_LRU), F32),
            pltpu.VMEM((1, D_LRU), F32),
        ],
        compiler_params=_params("arbitrary", "arbitrary"),
        name="rg_lru",
    )(proj, proj, h0, cprev, wts["lru_conv_w"], wts["lru_conv_b"], wts["lru_w_a"], wts["lru_b_a"],
      wts["lru_w_x"], wts["lru_b_x"], wts["lru_lambda"], wts["lru_out_g"])


def _qkv_body(q_ref, k_ref, v_ref, gq_ref, gk_ref, qb_ref, kf_ref, kb_ref, vf_ref, vb_ref):
    for h in range(N_HEADS):
        sl = slice(h * HEAD_DIM, (h + 1) * HEAD_DIM)
        qb_ref[:, sl] = _rms(q_ref[:, sl], gq_ref[...]).astype(BF16)
        kn = _rms(k_ref[:, sl], gk_ref[...])
        kf_ref[:, sl] = kn
        kb_ref[:, sl] = kn.astype(BF16)
    v = v_ref[...]
    vf_ref[...] = v
    vb_ref[...] = v.astype(BF16)


def _qkv(proj, gq, gk, tm):
    m = proj.shape[0]
    col0 = 2 * D_LRU // D_SB
    blk = lambda c: pl.BlockSpec((tm, D_SB), lambda i: (i, c))
    vec = pl.BlockSpec((1, HEAD_DIM), lambda i: (0, 0))
    out = pl.BlockSpec((tm, D_SB), lambda i: (i, 0))
    return pl.pallas_call(
        _qkv_body,
        grid=(m // tm,),
        in_specs=[blk(col0), blk(col0 + 1), blk(col0 + 2), vec, vec],
        out_specs=[out] * 5,
        out_shape=[jax.ShapeDtypeStruct((m, D_SB), d) for d in (BF16, F32, BF16, F32, BF16)],
        compiler_params=_params("parallel"),
        name="qkv_norm",
    )(proj, proj, proj, gq, gk)


def _sb_tile(q, k, v, tri, c, acc, masked):
    z = lax.dot_general(q, k, (((1,), (1,)), ((), ())), preferred_element_type=F32) * SCALE
    sp = _softplus(z)
    if masked:
        qpos = lax.broadcasted_iota(jnp.int32, z.shape, 0)
        kpos = lax.broadcasted_iota(jnp.int32, z.shape, 1)
        mask = kpos < qpos
        lb = jnp.where(mask, -sp, 0.0)
    else:
        lb = -sp
    inner = jnp.dot(lb.astype(BF16), tri, preferred_element_type=F32)
    w = jnp.exp(z - sp + (inner + c))
    if masked:
        w = jnp.where(mask, w, 0.0)
    acc = acc + jnp.dot(w.astype(BF16), v, preferred_element_type=F32)
    c = c + jnp.sum(lb, axis=-1, keepdims=True)
    return c, acc


def _attn_prompt_body(q_ref, k_ref, v_ref, tri_ref, o_ref, *, tq):
    qi = pl.program_id(1)
    q = q_ref[...]
    tri = tri_ref[...]
    c = jnp.zeros((tq, 1), F32)
    acc = jnp.zeros((tq, HEAD_DIM), F32)
    k0 = pl.multiple_of(qi * tq, tq)
    c, acc = _sb_tile(q, k_ref[pl.ds(k0, tq), :], v_ref[pl.ds(k0, tq), :], tri, c, acc, True)

    def body(n, carry):
        k0 = pl.multiple_of((qi - 1 - n) * tq, tq)
        return _sb_tile(q, k_ref[pl.ds(k0, tq), :], v_ref[pl.ds(k0, tq), :], tri, *carry, False)

    c, acc = lax.fori_loop(0, qi, body, (c, acc))
    o_ref[...] = acc


def _attn_prompt(qb, kb, vb, tri, tq):
    t = qb.shape[0]
    return pl.pallas_call(
        functools.partial(_attn_prompt_body, tq=tq),
        grid=(N_HEADS, t // tq),
        in_specs=[
            pl.BlockSpec((tq, HEAD_DIM), lambda h, i: (i, h)),
            pl.BlockSpec((t, HEAD_DIM), lambda h, i: (0, h)),
            pl.BlockSpec((t, HEAD_DIM), lambda h, i: (0, h)),
            pl.BlockSpec((tq, tq), lambda h, i: (0, 0)),
        ],
        out_specs=pl.BlockSpec((tq, HEAD_DIM), lambda h, i: (i, h)),
        out_shape=jax.ShapeDtypeStruct((t, D_SB), F32),
        compiler_params=_params("parallel", "parallel"),
        name="sb_attn_prompt",
    )(qb, kb, vb, tri)


def _attn_sample_body(q_ref, kn_ref, vn_ref, kc_ref, vc_ref, trin_ref, tri_ref, o_ref, *, tq, tk, past):
    q = q_ref[...]
    tri = tri_ref[...]
    c = jnp.zeros((tq, 1), F32)
    acc = jnp.zeros((tq, HEAD_DIM), F32)
    c, acc = _sb_tile(q, kn_ref[...], vn_ref[...], trin_ref[...], c, acc, True)

    def body(n, carry):
        k0 = pl.multiple_of((past // tk - 1 - n) * tk, tk)
        k = kc_ref[pl.ds(k0, tk), :].astype(BF16)
        v = vc_ref[pl.ds(k0, tk), :].astype(BF16)
        return _sb_tile(q, k, v, tri, *carry, False)

    c, acc = lax.fori_loop(0, past // tk, body, (c, acc))
    o_ref[...] = acc


def _attn_sample(qb, kb, vb, kc, vc, tri_new, tri, nseq, tq, tk):
    past = kc.shape[1]
    new = pl.BlockSpec((tq, HEAD_DIM), lambda b, h: (b, h))
    cache = pl.BlockSpec((None, past, HEAD_DIM), lambda b, h: (b, 0, h))
    return pl.pallas_call(
        functools.partial(_attn_sample_body, tq=tq, tk=tk, past=past),
        grid=(nseq, N_HEADS),
        in_specs=[new, new, new, cache, cache,
                  pl.BlockSpec((tq, tq), lambda b, h: (0, 0)),
                  pl.BlockSpec((tk, tk), lambda b, h: (0, 0))],
        out_specs=new,
        out_shape=jax.ShapeDtypeStruct((nseq * tq, D_SB), F32),
        compiler_params=_params("parallel", "parallel"),
        name="sb_attn_sample",
    )(qb, kb, vb, kc, vc, tri_new, tri)


def _ffn_act_body(val_ref, pre_ref, sprev_ref, cw_ref, cb_ref, g_ref, sout_ref, ext_ref, *, tt):
    @pl.when(pl.program_id(2) == 0)
    def _():
        ext_ref[0:HALO, :] = sprev_ref[0]

    pre = pre_ref[...]
    ext_ref[HALO:HALO + tt, :] = pre
    cw = cw_ref[...]
    pc = cb_ref[...]
    for k in range(FFN_CONV - 1):
        lo = HALO - (FFN_CONV - 1) + k
        pc = pc + ext_ref[lo:lo + tt, :] * cw[k:k + 1, :]
    pc = pc + pre * cw[FFN_CONV - 1:FFN_CONV, :]
    g_ref[...] = (_gelu(pc) * val_ref[...]).astype(BF16)

    last = ext_ref[tt:tt + HALO, :]
    sout_ref[0] = last
    ext_ref[0:HALO, :] = last


def _ffn_act(up, sprev, cw, cb, nseq, seqlen, tt, tc):
    m = nseq * seqlen
    nt = seqlen // tt
    nc = D_FF // tc
    return pl.pallas_call(
        functools.partial(_ffn_act_body, tt=tt),
        grid=(nseq, nc, nt),
        in_specs=[
            pl.BlockSpec((tt, tc), lambda b, c, t: (b * nt + t, c)),
            pl.BlockSpec((tt, tc), lambda b, c, t: (b * nt + t, nc + c)),
            pl.BlockSpec((1, HALO, tc), lambda b, c, t: (b, 0, c)),
            pl.BlockSpec((HALO, tc), lambda b, c, t: (0, c)),
            pl.BlockSpec((1, tc), lambda b, c, t: (0, c)),
        ],
        out_specs=[
            pl.BlockSpec((tt, tc), lambda b, c, t: (b * nt + t, c)),
            pl.BlockSpec((1, HALO, tc), lambda b, c, t: (b, 0, c)),
        ],
        out_shape=[
            jax.ShapeDtypeStruct((m, D_FF), BF16),
            jax.ShapeDtypeStruct((nseq, HALO, D_FF), F32),
        ],
        scratch_shapes=[pltpu.VMEM((tt + HALO, tc), F32)],
        compiler_params=_params("parallel", "parallel", "arbitrary"),
        name="ffn_act",
    )(up, up, sprev, cw, cb)


def _front_pad(x, rows):
    return jnp.pad(x, ((0, 0), (rows - x.shape[1], 0), (0, 0)))


def _layer(x3, kpast, vpast, h0, cprev, fprev, wts, tri, tm, tt_lru, tt_ffn):
    nseq, seqlen, _ = x3.shape
    m = nseq * seqlen
    x = x3.reshape(m, D_MODEL)

    proj = _norm_matmul(x, wts["norm_mix_g"], wts["w_in"], tm, 1024, "in_proj")
    y_lru, h_last, cstate = _lru(proj, h0[:, None, :], _front_pad(cprev, HALO), wts, nseq, seqlen, tt_lru)
    qb, kf, kb, vf, vb = _qkv(proj, wts["q_norm_g"], wts["k_norm_g"], tm)
    if kpast is None:
        y_sb = _attn_prompt(qb, kb, vb, tri, tri.shape[0])
    else:
        past = kpast.shape[1]
        y_sb = _attn_sample(qb, kb, vb, kpast.reshape(nseq, past, D_SB), vpast.reshape(nseq, past, D_SB),
                            tri[:seqlen, :seqlen], tri, nseq, seqlen, tri.shape[0])
    x1 = _outproj(y_lru, y_sb, wts["sb_out_g"], wts["w_out"], x, tm, 1024)

    up = _norm_matmul(x1, wts["norm_ffn_g"], wts["w_ffn_up"], tm, 1024, "ffn_up")
    g, fstate = _ffn_act(up, _front_pad(fprev, HALO), wts["ffn_conv_w"], wts["ffn_conv_b"],
                         nseq, seqlen, tt_ffn, 2048)
    out = _down(g, wts["w_ffn_down"], x1, tm, 1024, 2048)

    return (out.reshape(nseq, seqlen, D_MODEL),
            kf.reshape(nseq, seqlen, N_HEADS, HEAD_DIM),
            vf.reshape(nseq, seqlen, N_HEADS, HEAD_DIM),
            h_last[:, 0, :],
            cstate[:, HALO - (LRU_CONV - 1):, :],
            fstate[:, HALO - (FFN_CONV - 1):, :])


def _pair_blocks(w):
    z = jnp.zeros((N_PAIRS, LRU_BLOCK_DIM, LRU_BLOCK_DIM), w.dtype)
    top = jnp.concatenate([w[0::2], z], axis=2)
    bot = jnp.concatenate([z, w[1::2]], axis=2)
    return jnp.concatenate([top, bot], axis=1).astype(BF16)


def _prep_weights(norm_mix_g, w_in, lru_conv_w, lru_conv_b, lru_w_a, lru_b_a, lru_w_x, lru_b_x,
                  lru_lambda, q_norm_g, k_norm_g, lru_out_g, sb_out_g, w_out, norm_ffn_g,
                  w_ffn_up, ffn_conv_w, ffn_conv_b, w_ffn_down):
    row = lambda v: v.reshape(1, -1)
    pad_rows = lambda w: jnp.pad(w, ((0, HALO - w.shape[0]), (0, 0)))
    return {
        "norm_mix_g": row(norm_mix_g), "w_in": w_in.astype(BF16),
        "lru_conv_w": pad_rows(lru_conv_w), "lru_conv_b": row(lru_conv_b),
        "lru_w_a": _pair_blocks(lru_w_a), "lru_b_a": row(lru_b_a),
        "lru_w_x": _pair_blocks(lru_w_x), "lru_b_x": row(lru_b_x),
        "lru_lambda": row(lru_lambda),
        "q_norm_g": row(q_norm_g), "k_norm_g": row(k_norm_g),
        "lru_out_g": row(lru_out_g), "sb_out_g": row(sb_out_g), "w_out": w_out.astype(BF16),
        "norm_ffn_g": row(norm_ffn_g), "w_ffn_up": w_ffn_up.astype(BF16),
        "ffn_conv_w": pad_rows(ffn_conv_w), "ffn_conv_b": row(ffn_conv_b),
        "w_ffn_down": w_ffn_down.astype(BF16),
    }


def kernel(x_prompt, x_sample, cache_sb_k, cache_sb_v, state_lru_h, state_lru_conv, state_ffn_conv, norm_mix_g, w_in, lru_conv_w, lru_conv_b, lru_w_a, lru_b_a, lru_w_x, lru_b_x, lru_lambda, q_norm_g, k_norm_g, lru_out_g, sb_out_g, w_out, norm_ffn_g, w_ffn_up, ffn_conv_w, ffn_conv_b, w_ffn_down):
    depth = w_in.shape[0]
    bp = x_prompt.shape[0]
    tq = 256
    tri = jnp.tri(tq, k=-1, dtype=BF16)
    xp, xs = x_prompt, x_sample
    st_p, st_s = [], []
    for l in range(depth):
        wts = _prep_weights(norm_mix_g[l], w_in[l], lru_conv_w[l], lru_conv_b[l], lru_w_a[l], lru_b_a[l],
                            lru_w_x[l], lru_b_x[l], lru_lambda[l], q_norm_g[l], k_norm_g[l], lru_out_g[l],
                            sb_out_g[l], w_out[l], norm_ffn_g[l], w_ffn_up[l], ffn_conv_w[l],
                            ffn_conv_b[l], w_ffn_down[l])
        xp, *sp = _layer(xp, None, None,
                         jnp.zeros((bp, D_LRU), F32),
                         jnp.zeros((bp, LRU_CONV - 1, D_LRU), F32),
                         jnp.zeros((bp, FFN_CONV - 1, D_FF), F32),
                         wts, tri, tm=512, tt_lru=128, tt_ffn=256)
        st_p.append(sp)
        xs, *ss = _layer(xs, cache_sb_k[l], cache_sb_v[l], state_lru_h[l], state_lru_conv[l],
                         state_ffn_conv[l], wts, tri, tm=xs.shape[0] * xs.shape[1],
                         tt_lru=xs.shape[1], tt_ffn=xs.shape[1])
        st_s.append(ss)
    stack = lambda sts, i: jnp.stack([s[i] for s in sts])
    return (xp, xs) + tuple(stack(st_p, i) for i in range(5)) + tuple(stack(st_s, i) for i in range(5))
```

```python
import functools
import math

import jax
import jax.numpy as jnp
from jax import lax
from jax.experimental import pallas as pl
from jax.experimental.pallas import tpu as pltpu

F32 = jnp.float32
BF16 = jnp.bfloat16

D_MODEL = 4096
HEAD_DIM = 128
N_HEADS = 8
D_SB = N_HEADS * HEAD_DIM
D_LRU = D_MODEL - D_SB
LRU_BLOCKS = 16
LRU_BLOCK_DIM = D_LRU // LRU_BLOCKS
LRU_PAIR = 2 * LRU_BLOCK_DIM
N_PAIRS = LRU_BLOCKS // 2
LRU_CONV = 4
LRU_C = 8.0
D_FF = 3 * D_MODEL
FFN_CONV = 3
EPS = 1e-6
LOG2E = math.log2(math.e)
Q_SCALE = HEAD_DIM ** -0.5 * LOG2E
HALO = 8
SB_TILE = 256

VMEM_LIMIT = 56 * 1024 * 1024


def _params(*sem):
    return pltpu.CompilerParams(dimension_semantics=sem, vmem_limit_bytes=VMEM_LIMIT)


def _rms(x, g):
    ms = jnp.mean(x * x, axis=-1, keepdims=True)
    return (x * lax.rsqrt(ms + EPS)) * g


def _gelu(x):
    return x * (0.5 * (1.0 + jnp.tanh(0.7978845608028654 * (x + 0.044715 * (x * x * x)))))


def _sigmoid(x):
    return 1.0 / (1.0 + jnp.exp(-x))


def _softplus(x):
    return jnp.maximum(x, 0.0) + jnp.log1p(jnp.exp(-jnp.abs(x)))


def _norm_matmul_body(x_ref, g_ref, w_ref, o_ref, hn_ref):
    @pl.when(pl.program_id(1) == 0)
    def _():
        hn_ref[...] = _rms(x_ref[...], g_ref[...]).astype(BF16)

    o_ref[...] = jnp.dot(hn_ref[...], w_ref[...], preferred_element_type=F32)


def _norm_matmul(x, g, w, tm, tn, name):
    m, k = x.shape
    n = w.shape[1]
    return pl.pallas_call(
        _norm_matmul_body,
        grid=(m // tm, n // tn),
        in_specs=[
            pl.BlockSpec((tm, k), lambda i, j: (i, 0)),
            pl.BlockSpec((1, k), lambda i, j: (0, 0)),
            pl.BlockSpec((k, tn), lambda i, j: (0, j)),
        ],
        out_specs=pl.BlockSpec((tm, tn), lambda i, j: (i, j)),
        out_shape=jax.ShapeDtypeStruct((m, n), F32),
        scratch_shapes=[pltpu.VMEM((tm, k), BF16)],
        compiler_params=_params("parallel", "arbitrary"),
        name=name,
    )(x, g, w)


def _outproj_body(a_ref, y_ref, gsb_ref, w1_ref, w2_ref, res_ref, o_ref, yn_ref):
    @pl.when(pl.program_id(1) == 0)
    def _():
        yn_ref[...] = _rms(y_ref[...], gsb_ref[...]).astype(BF16)

    acc = jnp.dot(a_ref[...], w1_ref[...], preferred_element_type=F32)
    acc = acc + jnp.dot(yn_ref[...], w2_ref[...], preferred_element_type=F32)
    o_ref[...] = res_ref[...] + acc


def _outproj(a, y_sb, g_sb, w_out, res, tm, tn):
    m = a.shape[0]
    n = w_out.shape[1]
    return pl.pallas_call(
        _outproj_body,
        grid=(m // tm, n // tn),
        in_specs=[
            pl.BlockSpec((tm, D_LRU), lambda i, j: (i, 0)),
            pl.BlockSpec((tm, D_SB), lambda i, j: (i, 0)),
            pl.BlockSpec((1, D_SB), lambda i, j: (0, 0)),
            pl.BlockSpec((D_LRU, tn), lambda i, j: (0, j)),
            pl.BlockSpec((D_SB, tn), lambda i, j: (D_LRU // D_SB, j)),
            pl.BlockSpec((tm, tn), lambda i, j: (i, j)),
        ],
        out_specs=pl.BlockSpec((tm, tn), lambda i, j: (i, j)),
        out_shape=jax.ShapeDtypeStruct((m, n), F32),
        scratch_shapes=[pltpu.VMEM((tm, D_SB), BF16)],
        compiler_params=_params("parallel", "arbitrary"),
        name="out_proj",
    )(a, y_sb, g_sb, w_out, w_out, res)


def _down_body(g_ref, w_ref, res_ref, o_ref, acc_ref):
    k = pl.program_id(2)

    @pl.when(k == 0)
    def _():
        acc_ref[...] = jnp.zeros_like(acc_ref)

    acc_ref[...] += jnp.dot(g_ref[...], w_ref[...], preferred_element_type=F32)

    @pl.when(k == pl.num_programs(2) - 1)
    def _():
        o_ref[...] = res_ref[...] + acc_ref[...]


def _down(g, w, res, tm, tn, tk):
    m, kdim = g.shape
    n = w.shape[1]
    return pl.pallas_call(
        _down_body,
        grid=(m // tm, n // tn, kdim // tk),
        in_specs=[
            pl.BlockSpec((tm, tk), lambda i, j, k: (i, k)),
            pl.BlockSpec((tk, tn), lambda i, j, k: (k, j)),
            pl.BlockSpec((tm, tn), lambda i, j, k: (i, j)),
        ],
        out_specs=pl.BlockSpec((tm, tn), lambda i, j, k: (i, j)),
        out_shape=jax.ShapeDtypeStruct((m, n), F32),
        scratch_shapes=[pltpu.VMEM((tm, tn), F32)],
        compiler_params=_params("parallel", "parallel", "arbitrary"),
        name="ffn_down",
    )(g, w, res)


def _lru_body(u_ref, gate_ref, h0_ref, cprev_ref, cw_ref, cb_ref, wa_ref, ba_ref, wx_ref, bx_ref,
              lam_ref, og_ref, y_ref, hlast_ref, cstate_ref,
              ext_ref, uc_ref, a_ref, b_ref, hs_ref, hc_ref, *, tt):
    @pl.when(pl.program_id(1) == 0)
    def _():
        hc_ref[...] = h0_ref[0]
        ext_ref[0:HALO, :] = cprev_ref[0]

    u = u_ref[...]
    ext_ref[HALO:HALO + tt, :] = u
    cw = cw_ref[...]
    uc = cb_ref[...]
    for k in range(LRU_CONV - 1):
        lo = HALO - (LRU_CONV - 1) + k
        uc = uc + ext_ref[lo:lo + tt, :] * cw[k:k + 1, :]
    uc_ref[...] = uc + u * cw[LRU_CONV - 1:LRU_CONV, :]

    last = ext_ref[tt:tt + HALO, :]
    cstate_ref[0] = last
    ext_ref[0:HALO, :] = last

    for p in range(N_PAIRS):
        sl = slice(p * LRU_PAIR, (p + 1) * LRU_PAIR)
        ucp = uc_ref[:, sl]
        xb = ucp.astype(BF16)
        r = _sigmoid(jnp.dot(xb, wa_ref[p], preferred_element_type=F32) + ba_ref[:, sl])
        i = _sigmoid(jnp.dot(xb, wx_ref[p], preferred_element_type=F32) + bx_ref[:, sl])
        log_a = (LRU_C * r) * (-_softplus(-lam_ref[:, sl]))
        a = jnp.exp(log_a)
        a_ref[:, sl] = a
        b_ref[:, sl] = jnp.sqrt(-jnp.tanh(log_a) * (a * a + 1.0)) * (i * ucp)

    def step(t, h):
        h = a_ref[pl.ds(t, 1), :] * h + b_ref[pl.ds(t, 1), :]
        hs_ref[pl.ds(t, 1), :] = h
        return h

    h = lax.fori_loop(0, tt, step, hc_ref[...], unroll=8)
    hc_ref[...] = h
    hlast_ref[0] = h

    y = hs_ref[...] * _gelu(gate_ref[...])
    y_ref[...] = _rms(y, og_ref[...]).astype(BF16)


def _lru(proj, h0, cprev, wts, nseq, seqlen, tt):
    m = nseq * seqlen
    nt = seqlen // tt
    row = lambda b, t: (b * nt + t, 0)
    vec = lambda b, t: (0, 0)
    return pl.pallas_call(
        functools.partial(_lru_body, tt=tt),
        grid=(nseq, nt),
        in_specs=[
            pl.BlockSpec((tt, D_LRU), row),
            pl.BlockSpec((tt, D_LRU), lambda b, t: (b * nt + t, 1)),
            pl.BlockSpec((1, 1, D_LRU), lambda b, t: (b, 0, 0)),
            pl.BlockSpec((1, HALO, D_LRU), lambda b, t: (b, 0, 0)),
            pl.BlockSpec((HALO, D_LRU), vec),
            pl.BlockSpec((1, D_LRU), vec),
            pl.BlockSpec((N_PAIRS, LRU_PAIR, LRU_PAIR), lambda b, t: (0, 0, 0)),
            pl.BlockSpec((1, D_LRU), vec),
            pl.BlockSpec((N_PAIRS, LRU_PAIR, LRU_PAIR), lambda b, t: (0, 0, 0)),
            pl.BlockSpec((1, D_LRU), vec),
            pl.BlockSpec((1, D_LRU), vec),
            pl.BlockSpec((1, D_LRU), vec),
        ],
        out_specs=[
            pl.BlockSpec((tt, D_LRU), row),
            pl.BlockSpec((1, 1, D_LRU), lambda b, t: (b, 0, 0)),
            pl.BlockSpec((1, HALO, D_LRU), lambda b, t: (b, 0, 0)),
        ],
        out_shape=[
            jax.ShapeDtypeStruct((m, D_LRU), BF16),
            jax.ShapeDtypeStruct((nseq, 1, D_LRU), F32),
            jax.ShapeDtypeStruct((nseq, HALO, D_LRU), F32),
        ],
        scratch_shapes=[
            pltpu.VMEM((tt + HALO, D_LRU), F32),
            pltpu.VMEM((tt, D_LRU), F32),
            pltpu.VMEM((tt, D_LRU), F32),
            pltpu.VMEM((tt, D_LRU), F32),
            pltpu.VMEM((tt, D_LRU), F32),
            pltpu.VMEM((1, D_LRU), F32),
        ],
        compiler_params=_params("arbitrary", "arbitrary"),
        name="rg_lru",
    )(proj, proj, h0, cprev, wts["lru_conv_w"], wts["lru_conv_b"], wts["lru_w_a"], wts["lru_b_a"],
      wts["lru_w_x"], wts["lru_b_x"], wts["lru_lambda"], wts["lru_out_g"])


def _qkv_body(q_ref, k_ref, v_ref, gq_ref, gk_ref, qb_ref, kf_ref, kb_ref, vf_ref, vb_ref):
    for h in range(N_HEADS):
        sl = slice(h * HEAD_DIM, (h + 1) * HEAD_DIM)
        qb_ref[:, sl] = (_rms(q_ref[:, sl], gq_ref[...]) * Q_SCALE).astype(BF16)
        kn = _rms(k_ref[:, sl], gk_ref[...])
        kf_ref[:, sl] = kn
        kb_ref[:, sl] = kn.astype(BF16)
    v = v_ref[...]
    vf_ref[...] = v
    vb_ref[...] = v.astype(BF16)


def _qkv(proj, gq, gk, tm):
    m = proj.shape[0]
    col0 = 2 * D_LRU // D_SB
    blk = lambda c: pl.BlockSpec((tm, D_SB), lambda i: (i, c))
    vec = pl.BlockSpec((1, HEAD_DIM), lambda i: (0, 0))
    out = pl.BlockSpec((tm, D_SB), lambda i: (i, 0))
    return pl.pallas_call(
        _qkv_body,
        grid=(m // tm,),
        in_specs=[blk(col0), blk(col0 + 1), blk(col0 + 2), vec, vec],
        out_specs=[out] * 5,
        out_shape=[jax.ShapeDtypeStruct((m, D_SB), d) for d in (BF16, F32, BF16, F32, BF16)],
        compiler_params=_params("parallel"),
        name="qkv_norm",
    )(proj, proj, proj, gq, gk)


def _neg_abs(x):
    return lax.bitcast_convert_type(lax.bitcast_convert_type(x, jnp.int32) | jnp.int32(-2 ** 31), F32)


def _sb_step(q, k, v, tri, c_ref, crows, acc_ref, arows, acols, masked):
    z = lax.dot_general(q, k, (((1,), (1,)), ((), ())), preferred_element_type=F32)
    sp = jnp.maximum(z, 0.0) + jnp.log(1.0 + jnp.exp2(_neg_abs(z))) * LOG2E
    if masked:
        mask = (lax.broadcasted_iota(jnp.int32, z.shape, 1) < lax.broadcasted_iota(jnp.int32, z.shape, 0))
        sp = jnp.where(mask, sp, 0.0)
    inner = jnp.dot(sp.astype(BF16), tri, preferred_element_type=F32)
    c = c_ref[crows, :]
    w = jnp.exp2((z - sp) - (inner + c))
    if masked:
        w = jnp.where(mask, w, 0.0)
    acc_ref[arows, acols] += jnp.dot(w.astype(BF16), v, preferred_element_type=F32)
    c_ref[crows, :] = c + jnp.sum(sp, axis=-1, keepdims=True)


NEG_BIG = -1e30


def _sb_logits(q, k, z_ref, slot):
    z_ref[slot] = lax.dot_general(q, k, (((1,), (1,)), ((), ())), preferred_element_type=F32)


def _sb_scores(z_ref, c_ref, crows, t_ref, spb_ref, slot, mode):
    if mode == "none":
        t_ref[slot] = jnp.full(t_ref.shape[1:], NEG_BIG, F32)
        spb_ref[slot] = jnp.zeros(spb_ref.shape[1:], BF16)
        return
    z = z_ref[slot]
    sp = jnp.maximum(z, 0.0) + jnp.log(1.0 + jnp.exp2(_neg_abs(z))) * LOG2E
    c = c_ref[crows, :]
    t = (z - sp) - c
    if mode == "diag":
        mask = (lax.broadcasted_iota(jnp.int32, z.shape, 1) < lax.broadcasted_iota(jnp.int32, z.shape, 0))
        sp = jnp.where(mask, sp, 0.0)
        t = jnp.where(mask, t, NEG_BIG)
    t_ref[slot] = t
    spb_ref[slot] = sp.astype(BF16)
    c_ref[crows, :] = c + jnp.sum(sp, axis=-1, keepdims=True)


def _sb_weights(tri, t_ref, spb_ref, wb_ref, slot):
    inner = jnp.dot(spb_ref[slot], tri, preferred_element_type=F32)
    wb_ref[slot] = jnp.exp2(t_ref[slot] - inner).astype(BF16)


def _sb_values(v, wb_ref, slot, acc_ref, arows):
    acc_ref[arows, :] += jnp.dot(wb_ref[slot], v, preferred_element_type=F32)


def _attn_prompt_body(q_ref, k_ref, v_ref, tri_ref, o_ref, c_ref, z_ref, t_ref, spb_ref, wb_ref, *, nsub):
    ts = SB_TILE
    base = pl.program_id(1) * nsub
    tri = tri_ref[...]
    c_ref[...] = jnp.zeros_like(c_ref)
    o_ref[...] = jnp.zeros_like(o_ref)
    rows = [pl.ds(s * ts, ts) for s in range(nsub)]
    qs = [q_ref[r, :] for r in rows]
    cols = slice(None)

    @pl.when(base == 0)
    def _():
        for t in reversed(range(nsub)):
            k = k_ref[t * ts:(t + 1) * ts, :]
            v = v_ref[t * ts:(t + 1) * ts, :]
            for s in range(t, nsub):
                _sb_step(qs[s], k, v, tri, c_ref, rows[s], o_ref, rows[s], cols, masked=(s == t))

    @pl.when(base > 0)
    def _():
        n = base + nsub

        def tile(i):
            return pl.ds(pl.multiple_of((n - 1 - i) * ts, ts), ts)

        def modes(i):
            kt = nsub - 1 - i
            return tuple("full" if (kt < 0 or s > kt) else ("diag" if s == kt else "none")
                         for s in range(nsub))

        def iteration(i, first=1, last=4, static_i=None):
            p = (i & 1) if static_i is None else (static_i & 1)
            if last >= 4 and first <= 4:
                v = v_ref[tile(i - 3), :]
                for s in range(nsub):
                    _sb_values(v, wb_ref.at[1 - p], s, o_ref, rows[s])
            if last >= 3 and first <= 3:
                for s in range(nsub):
                    _sb_weights(tri, t_ref.at[1 - p], spb_ref.at[1 - p], wb_ref.at[p], s)
            if last >= 2 and first <= 2:
                md = modes(static_i - 1) if static_i is not None else ("full",) * nsub
                for s in range(nsub):
                    _sb_scores(z_ref.at[1 - p], c_ref, rows[s], t_ref.at[p], spb_ref.at[p], s, md[s])
            if last >= 1 and first <= 1:
                k = k_ref[tile(i), :]
                for s in range(nsub):
                    _sb_logits(qs[s], k, z_ref.at[p], s)

        for i in range(nsub + 1):
            iteration(i, last=min(i + 1, 4), static_i=i)

        def body(i, carry):
            iteration(i)
            return carry

        lax.fori_loop(nsub + 1, n, body, 0)
        for d in range(3):
            iteration(n + d, first=d + 2)


def _attn_prompt(qb, kb, vb, tri, nsub):
    t = qb.shape[0]
    tq = nsub * SB_TILE
    return pl.pallas_call(
        functools.partial(_attn_prompt_body, nsub=nsub),
        grid=(N_HEADS, t // tq),
        in_specs=[
            pl.BlockSpec((tq, HEAD_DIM), lambda h, i: (i, h)),
            pl.BlockSpec((t, HEAD_DIM), lambda h, i: (0, h)),
            pl.BlockSpec((t, HEAD_DIM), lambda h, i: (0, h)),
            pl.BlockSpec((SB_TILE, SB_TILE), lambda h, i: (0, 0)),
        ],
        out_specs=pl.BlockSpec((tq, HEAD_DIM), lambda h, i: (i, h)),
        out_shape=jax.ShapeDtypeStruct((t, D_SB), F32),
        scratch_shapes=[pltpu.VMEM((tq, 1), F32),
                        pltpu.VMEM((2, nsub, SB_TILE, SB_TILE), F32),
                        pltpu.VMEM((2, nsub, SB_TILE, SB_TILE), F32),
                        pltpu.VMEM((2, nsub, SB_TILE, SB_TILE), BF16),
                        pltpu.VMEM((2, nsub, SB_TILE, SB_TILE), BF16)],
        compiler_params=_params("parallel", "arbitrary"),
        name="sb_attn_prompt",
    )(qb, kb, vb, tri)


def _attn_sample_body(q_ref, kn_ref, vn_ref, kc_ref, vc_ref, trin_ref, tri_ref, o_ref, c_ref, *, tq, chunk):
    ts = SB_TILE
    tri = tri_ref[...]
    rows = [pl.ds(h * tq, tq) for h in range(N_HEADS)]
    cols = [slice(h * HEAD_DIM, (h + 1) * HEAD_DIM) for h in range(N_HEADS)]
    qs = [q_ref[:, cl] for cl in cols]

    @pl.when(pl.program_id(1) == 0)
    def _():
        c_ref[...] = jnp.zeros_like(c_ref)
        o_ref[...] = jnp.zeros_like(o_ref)
        trin = trin_ref[...]
        for h in range(N_HEADS):
            _sb_step(qs[h], kn_ref[:, cols[h]], vn_ref[:, cols[h]], trin, c_ref, rows[h], o_ref, slice(None), cols[h], True)

    for t in reversed(range(chunk // ts)):
        for h in range(N_HEADS):
            sel = pl.ds(t * ts * N_HEADS + h, ts, stride=N_HEADS)
            k = kc_ref[sel, :].astype(BF16)
            v = vc_ref[sel, :].astype(BF16)
            _sb_step(qs[h], k, v, tri, c_ref, rows[h], o_ref, slice(None), cols[h], False)


def _attn_sample(qb, kb, vb, kc, vc, tri_new, tri, nseq, tq, chunk):
    nchunk = kc.shape[1] // (chunk * N_HEADS)
    new = pl.BlockSpec((tq, D_SB), lambda b, c: (b, 0))
    cache = pl.BlockSpec((None, chunk * N_HEADS, HEAD_DIM), lambda b, c: (b, nchunk - 1 - c, 0))
    return pl.pallas_call(
        functools.partial(_attn_sample_body, tq=tq, chunk=chunk),
        grid=(nseq, nchunk),
        in_specs=[new, new, new, cache, cache,
                  pl.BlockSpec((tq, tq), lambda b, c: (0, 0)),
                  pl.BlockSpec((SB_TILE, SB_TILE), lambda b, c: (0, 0))],
        out_specs=new,
        out_shape=jax.ShapeDtypeStruct((nseq * tq, D_SB), F32),
        scratch_shapes=[pltpu.VMEM((N_HEADS * tq, 1), F32)],
        compiler_params=_params("parallel", "arbitrary"),
        name="sb_attn_sample",
    )(qb, kb, vb, kc, vc, tri_new, tri)


def _ffn_conv_act(pre, val, ext_ref, cw_ref, cb_ref, tt):
    ext_ref[HALO:HALO + tt, :] = pre
    cw = cw_ref[...]
    pc = cb_ref[...]
    for k in range(FFN_CONV - 1):
        lo = HALO - (FFN_CONV - 1) + k
        pc = pc + ext_ref[lo:lo + tt, :] * cw[k:k + 1, :]
    pc = pc + pre * cw[FFN_CONV - 1:FFN_CONV, :]
    return (_gelu(pc) * val).astype(BF16), ext_ref[tt:tt + HALO, :]


def _ffn_act_body(val_ref, pre_ref, sprev_ref, cw_ref, cb_ref, g_ref, sout_ref, ext_ref, *, tt):
    @pl.when(pl.program_id(2) == 0)
    def _():
        ext_ref[0:HALO, :] = sprev_ref[0]

    g, last = _ffn_conv_act(pre_ref[...], val_ref[...], ext_ref, cw_ref, cb_ref, tt)
    g_ref[...] = g
    sout_ref[0] = last
    ext_ref[0:HALO, :] = last


def _ffn_act(up, sprev, cw, cb, nseq, seqlen, tt, tc):
    m = nseq * seqlen
    nt = seqlen // tt
    nc = D_FF // tc
    return pl.pallas_call(
        functools.partial(_ffn_act_body, tt=tt),
        grid=(nseq, nc, nt),
        in_specs=[
            pl.BlockSpec((tt, tc), lambda b, c, t: (b * nt + t, c)),
            pl.BlockSpec((tt, tc), lambda b, c, t: (b * nt + t, nc + c)),
            pl.BlockSpec((1, HALO, tc), lambda b, c, t: (b, 0, c)),
            pl.BlockSpec((HALO, tc), lambda b, c, t: (0, c)),
            pl.BlockSpec((1, tc), lambda b, c, t: (0, c)),
        ],
        out_specs=[
            pl.BlockSpec((tt, tc), lambda b, c, t: (b * nt + t, c)),
            pl.BlockSpec((1, HALO, tc), lambda b, c, t: (b, 0, c)),
        ],
        out_shape=[
            jax.ShapeDtypeStruct((m, D_FF), BF16),
            jax.ShapeDtypeStruct((nseq, HALO, D_FF), F32),
        ],
        scratch_shapes=[pltpu.VMEM((tt + HALO, tc), F32)],
        compiler_params=_params("parallel", "parallel", "arbitrary"),
        name="ffn_act",
    )(up, up, sprev, cw, cb)


def _ffn_up_act_body(x_ref, g_ref, wv_ref, wp_ref, sprev_ref, cw_ref, cb_ref, o_ref, last_ref,
                     hn_ref, halo_ref, ext_ref, *, tm):
    i = pl.program_id(0)
    j = pl.program_id(1)

    @pl.when(j == 0)
    def _():
        hn_ref[...] = _rms(x_ref[...], g_ref[...]).astype(BF16)

    @pl.when(i == 0)
    def _():
        ext_ref[0:HALO, :] = sprev_ref[...]

    @pl.when(i > 0)
    def _():
        ext_ref[0:HALO, :] = halo_ref[j]

    hn = hn_ref[...]
    val = jnp.dot(hn, wv_ref[...], preferred_element_type=F32)
    pre = jnp.dot(hn, wp_ref[...], preferred_element_type=F32)
    g, last = _ffn_conv_act(pre, val, ext_ref, cw_ref, cb_ref, tm)
    o_ref[...] = g
    halo_ref[j] = last
    last_ref[0] = last


def _ffn_up_act(x, g, w_up, sprev, cw, cb, tm, tn):
    m, k = x.shape
    nj = D_FF // tn
    return pl.pallas_call(
        functools.partial(_ffn_up_act_body, tm=tm),
        grid=(m // tm, nj),
        in_specs=[
            pl.BlockSpec((tm, k), lambda i, j: (i, 0)),
            pl.BlockSpec((1, k), lambda i, j: (0, 0)),
            pl.BlockSpec((k, tn), lambda i, j: (0, j)),
            pl.BlockSpec((k, tn), lambda i, j: (0, nj + j)),
            pl.BlockSpec((HALO, tn), lambda i, j: (0, j)),
            pl.BlockSpec((HALO, tn), lambda i, j: (0, j)),
            pl.BlockSpec((1, tn), lambda i, j: (0, j)),
        ],
        out_specs=[
            pl.BlockSpec((tm, tn), lambda i, j: (i, j)),
            pl.BlockSpec((1, HALO, tn), lambda i, j: (i, 0, j)),
        ],
        out_shape=[
            jax.ShapeDtypeStruct((m, D_FF), BF16),
            jax.ShapeDtypeStruct((m // tm, HALO, D_FF), F32),
        ],
        scratch_shapes=[
            pltpu.VMEM((tm, k), BF16),
            pltpu.VMEM((nj, HALO, tn), F32),
            pltpu.VMEM((tm + HALO, tn), F32),
        ],
        compiler_params=_params("arbitrary", "arbitrary"),
        name="ffn_up_act",
    )(x, g, w_up, w_up, sprev, cw, cb)


def _front_pad(x, rows):
    return jnp.pad(x, ((0, 0), (rows - x.shape[1], 0), (0, 0)))


def _layer(x3, kpast, vpast, h0, cprev, fprev, wts, tri, tm, tt_lru, tt_ffn, attn_nsub=2, fuse_ffn=None):
    nseq, seqlen, _ = x3.shape
    m = nseq * seqlen
    x = x3.reshape(m, D_MODEL)
    if fuse_ffn is None:
        fuse_ffn = nseq == 1

    proj = _norm_matmul(x, wts["norm_mix_g"], wts["w_in"], tm, 1024, "in_proj")
    y_lru, h_last, cstate = _lru(proj, h0[:, None, :], _front_pad(cprev, HALO), wts, nseq, seqlen, tt_lru)
    qb, kf, kb, vf, vb = _qkv(proj, wts["q_norm_g"], wts["k_norm_g"], tm)
    if kpast is None:
        y_sb = _attn_prompt(qb, kb, vb, tri, attn_nsub)
    else:
        past = kpast.shape[1]
        y_sb = _attn_sample(qb, kb, vb, kpast.reshape(nseq, past * N_HEADS, HEAD_DIM),
                            vpast.reshape(nseq, past * N_HEADS, HEAD_DIM),
                            tri[:seqlen, :seqlen], tri, nseq, seqlen, min(past, 1024))
    x1 = _outproj(y_lru, y_sb, wts["sb_out_g"], wts["w_out"], x, tm, 1024)

    fprev8 = _front_pad(fprev, HALO)
    if fuse_ffn:
        g, lasts = _ffn_up_act(x1, wts["norm_ffn_g"], wts["w_ffn_up"], fprev8[0], wts["ffn_conv_w"],
                               wts["ffn_conv_b"], tm, 512)
        fstate = lasts[-1:]
    else:
        up = _norm_matmul(x1, wts["norm_ffn_g"], wts["w_ffn_up"], tm, 1024, "ffn_up")
        g, fstate = _ffn_act(up, fprev8, wts["ffn_conv_w"], wts["ffn_conv_b"], nseq, seqlen, tt_ffn, 2048)
    out = _down(g, wts["w_ffn_down"], x1, min(m, 1024), 1024, 2048)

    return (out.reshape(nseq, seqlen, D_MODEL),
            kf.reshape(nseq, seqlen, N_HEADS, HEAD_DIM),
            vf.reshape(nseq, seqlen, N_HEADS, HEAD_DIM),
            h_last[:, 0, :],
            cstate[:, HALO - (LRU_CONV - 1):, :],
            fstate[:, HALO - (FFN_CONV - 1):, :])


def _pair_blocks(w):
    z = jnp.zeros((N_PAIRS, LRU_BLOCK_DIM, LRU_BLOCK_DIM), w.dtype)
    top = jnp.concatenate([w[0::2], z], axis=2)
    bot = jnp.concatenate([z, w[1::2]], axis=2)
    return jnp.concatenate([top, bot], axis=1).astype(BF16)


def _prep_weights(norm_mix_g, w_in, lru_conv_w, lru_conv_b, lru_w_a, lru_b_a, lru_w_x, lru_b_x,
                  lru_lambda, q_norm_g, k_norm_g, lru_out_g, sb_out_g, w_out, norm_ffn_g,
                  w_ffn_up, ffn_conv_w, ffn_conv_b, w_ffn_down):
    row = lambda v: v.reshape(1, -1)
    pad_rows = lambda w: jnp.pad(w, ((0, HALO - w.shape[0]), (0, 0)))
    return {
        "norm_mix_g": row(norm_mix_g), "w_in": w_in.astype(BF16),
        "lru_conv_w": pad_rows(lru_conv_w), "lru_conv_b": row(lru_conv_b),
        "lru_w_a": _pair_blocks(lru_w_a), "lru_b_a": row(lru_b_a),
        "lru_w_x": _pair_blocks(lru_w_x), "lru_b_x": row(lru_b_x),
        "lru_lambda": row(lru_lambda),
        "q_norm_g": row(q_norm_g), "k_norm_g": row(k_norm_g),
        "lru_out_g": row(lru_out_g), "sb_out_g": row(sb_out_g), "w_out": w_out.astype(BF16),
        "norm_ffn_g": row(norm_ffn_g), "w_ffn_up": w_ffn_up.astype(BF16),
        "ffn_conv_w": pad_rows(ffn_conv_w), "ffn_conv_b": row(ffn_conv_b),
        "w_ffn_down": w_ffn_down.astype(BF16),
    }


def kernel(x_prompt, x_sample, cache_sb_k, cache_sb_v, state_lru_h, state_lru_conv, state_ffn_conv, norm_mix_g, w_in, lru_conv_w, lru_conv_b, lru_w_a, lru_b_a, lru_w_x, lru_b_x, lru_lambda, q_norm_g, k_norm_g, lru_out_g, sb_out_g, w_out, norm_ffn_g, w_ffn_up, ffn_conv_w, ffn_conv_b, w_ffn_down):
    depth = w_in.shape[0]
    bp = x_prompt.shape[0]
    tri = jnp.tri(SB_TILE, k=-1, dtype=BF16)
    xp, xs = x_prompt, x_sample
    st_p, st_s = [], []
    for l in range(depth):
        wts = _prep_weights(norm_mix_g[l], w_in[l], lru_conv_w[l], lru_conv_b[l], lru_w_a[l], lru_b_a[l],
                            lru_w_x[l], lru_b_x[l], lru_lambda[l], q_norm_g[l], k_norm_g[l], lru_out_g[l],
                            sb_out_g[l], w_out[l], norm_ffn_g[l], w_ffn_up[l], ffn_conv_w[l],
                            ffn_conv_b[l], w_ffn_down[l])
        xp, *sp = _layer(xp, None, None,
                         jnp.zeros((bp, D_LRU), F32),
                         jnp.zeros((bp, LRU_CONV - 1, D_LRU), F32),
                         jnp.zeros((bp, FFN_CONV - 1, D_FF), F32),
                         wts, tri, tm=512, tt_lru=128, tt_ffn=256)
        st_p.append(sp)
        xs, *ss = _layer(xs, cache_sb_k[l], cache_sb_v[l], state_lru_h[l], state_lru_conv[l],
                         state_ffn_conv[l], wts, tri, tm=xs.shape[0] * xs.shape[1],
                         tt_lru=xs.shape[1], tt_ffn=xs.shape[1])
        st_s.append(ss)
    stack = lambda sts, i: jnp.stack([s[i] for s in sts])
    return (xp, xs) + tuple(stack(st_p, i) for i in range(5)) + tuple(stack(st_s, i) for i in range(5))
```

```python
import functools
import math

import jax
import jax.numpy as jnp
from jax import lax
from jax.experimental import pallas as pl
from jax.experimental.pallas import tpu as pltpu

F32 = jnp.float32
BF16 = jnp.bfloat16

D_MODEL = 4096
HEAD_DIM = 128
N_HEADS = 8
D_SB = N_HEADS * HEAD_DIM
D_LRU = D_MODEL - D_SB
LRU_BLOCKS = 16
LRU_BLOCK_DIM = D_LRU // LRU_BLOCKS
LRU_PAIR = 2 * LRU_BLOCK_DIM
N_PAIRS = LRU_BLOCKS // 2
LRU_CONV = 4
LRU_C = 8.0
D_FF = 3 * D_MODEL
FFN_CONV = 3
EPS = 1e-6
LOG2E = math.log2(math.e)
Q_SCALE = HEAD_DIM ** -0.5 * LOG2E
HALO = 8
SB_TILE = 256

VMEM_LIMIT = 56 * 1024 * 1024


def _params(*sem):
    return pltpu.CompilerParams(dimension_semantics=sem, vmem_limit_bytes=VMEM_LIMIT)


def _rms(x, g):
    ms = jnp.mean(x * x, axis=-1, keepdims=True)
    return (x * lax.rsqrt(ms + EPS)) * g


def _gelu(x):
    return x * (0.5 * (1.0 + jnp.tanh(0.7978845608028654 * (x + 0.044715 * (x * x * x)))))


def _sigmoid(x):
    return 1.0 / (1.0 + jnp.exp(-x))


def _softplus(x):
    return jnp.maximum(x, 0.0) + jnp.log1p(jnp.exp(-jnp.abs(x)))


def _norm_matmul_body(x_ref, g_ref, w_ref, o_ref, hn_ref):
    @pl.when(pl.program_id(1) == 0)
    def _():
        hn_ref[...] = _rms(x_ref[...], g_ref[...]).astype(BF16)

    o_ref[...] = jnp.dot(hn_ref[...], w_ref[...], preferred_element_type=F32)


def _norm_matmul(x, g, w, tm, tn, name):
    m, k = x.shape
    n = w.shape[1]
    return pl.pallas_call(
        _norm_matmul_body,
        grid=(m // tm, n // tn),
        in_specs=[
            pl.BlockSpec((tm, k), lambda i, j: (i, 0)),
            pl.BlockSpec((1, k), lambda i, j: (0, 0)),
            pl.BlockSpec((k, tn), lambda i, j: (0, j)),
        ],
        out_specs=pl.BlockSpec((tm, tn), lambda i, j: (i, j)),
        out_shape=jax.ShapeDtypeStruct((m, n), F32),
        scratch_shapes=[pltpu.VMEM((tm, k), BF16)],
        compiler_params=_params("parallel", "arbitrary"),
        name=name,
    )(x, g, w)


def _outproj_body(a_ref, y_ref, gsb_ref, w1_ref, w2_ref, res_ref, o_ref, yn_ref):
    @pl.when(pl.program_id(1) == 0)
    def _():
        yn_ref[...] = _rms(y_ref[...], gsb_ref[...]).astype(BF16)

    acc = jnp.dot(a_ref[...], w1_ref[...], preferred_element_type=F32)
    acc = acc + jnp.dot(yn_ref[...], w2_ref[...], preferred_element_type=F32)
    o_ref[...] = res_ref[...] + acc


def _outproj(a, y_sb, g_sb, w_out, res, tm, tn):
    m = a.shape[0]
    n = w_out.shape[1]
    return pl.pallas_call(
        _outproj_body,
        grid=(m // tm, n // tn),
        in_specs=[
            pl.BlockSpec((tm, D_LRU), lambda i, j: (i, 0)),
            pl.BlockSpec((tm, D_SB), lambda i, j: (i, 0)),
            pl.BlockSpec((1, D_SB), lambda i, j: (0, 0)),
            pl.BlockSpec((D_LRU, tn), lambda i, j: (0, j)),
            pl.BlockSpec((D_SB, tn), lambda i, j: (D_LRU // D_SB, j)),
            pl.BlockSpec((tm, tn), lambda i, j: (i, j)),
        ],
        out_specs=pl.BlockSpec((tm, tn), lambda i, j: (i, j)),
        out_shape=jax.ShapeDtypeStruct((m, n), F32),
        scratch_shapes=[pltpu.VMEM((tm, D_SB), BF16)],
        compiler_params=_params("parallel", "arbitrary"),
        name="out_proj",
    )(a, y_sb, g_sb, w_out, w_out, res)


def _down_body(g_ref, w_ref, res_ref, o_ref, acc_ref):
    k = pl.program_id(2)

    @pl.when(k == 0)
    def _():
        acc_ref[...] = jnp.zeros_like(acc_ref)

    acc_ref[...] += jnp.dot(g_ref[...], w_ref[...], preferred_element_type=F32)

    @pl.when(k == pl.num_programs(2) - 1)
    def _():
        o_ref[...] = res_ref[...] + acc_ref[...]


def _down(g, w, res, tm, tn, tk):
    m, kdim = g.shape
    n = w.shape[1]
    return pl.pallas_call(
        _down_body,
        grid=(m // tm, n // tn, kdim // tk),
        in_specs=[
            pl.BlockSpec((tm, tk), lambda i, j, k: (i, k)),
            pl.BlockSpec((tk, tn), lambda i, j, k: (k, j)),
            pl.BlockSpec((tm, tn), lambda i, j, k: (i, j)),
        ],
        out_specs=pl.BlockSpec((tm, tn), lambda i, j, k: (i, j)),
        out_shape=jax.ShapeDtypeStruct((m, n), F32),
        scratch_shapes=[pltpu.VMEM((tm, tn), F32)],
        compiler_params=_params("parallel", "parallel", "arbitrary"),
        name="ffn_down",
    )(g, w, res)


def _lru_body(u_ref, gate_ref, h0_ref, cprev_ref, cw_ref, cb_ref, wa_ref, ba_ref, wx_ref, bx_ref,
              lam_ref, og_ref, y_ref, hlast_ref, cstate_ref,
              ext_ref, uc_ref, a_ref, b_ref, hs_ref, hc_ref, *, tt):
    @pl.when(pl.program_id(1) == 0)
    def _():
        hc_ref[...] = h0_ref[0]
        ext_ref[0:HALO, :] = cprev_ref[0]

    u = u_ref[...]
    ext_ref[HALO:HALO + tt, :] = u
    cw = cw_ref[...]
    uc = cb_ref[...]
    for k in range(LRU_CONV - 1):
        lo = HALO - (LRU_CONV - 1) + k
        uc = uc + ext_ref[lo:lo + tt, :] * cw[k:k + 1, :]
    uc_ref[...] = uc + u * cw[LRU_CONV - 1:LRU_CONV, :]

    last = ext_ref[tt:tt + HALO, :]
    cstate_ref[0] = last
    ext_ref[0:HALO, :] = last

    for p in range(N_PAIRS):
        sl = slice(p * LRU_PAIR, (p + 1) * LRU_PAIR)
        ucp = uc_ref[:, sl]
        xb = ucp.astype(BF16)
        r = _sigmoid(jnp.dot(xb, wa_ref[p], preferred_element_type=F32) + ba_ref[:, sl])
        i = _sigmoid(jnp.dot(xb, wx_ref[p], preferred_element_type=F32) + bx_ref[:, sl])
        log_a = (LRU_C * r) * (-_softplus(-lam_ref[:, sl]))
        a = jnp.exp(log_a)
        a_ref[:, sl] = a
        b_ref[:, sl] = jnp.sqrt(-jnp.tanh(log_a) * (a * a + 1.0)) * (i * ucp)

    def step(t, h):
        h = a_ref[pl.ds(t, 1), :] * h + b_ref[pl.ds(t, 1), :]
        hs_ref[pl.ds(t, 1), :] = h
        return h

    h = lax.fori_loop(0, tt, step, hc_ref[...], unroll=8)
    hc_ref[...] = h
    hlast_ref[0] = h

    y = hs_ref[...] * _gelu(gate_ref[...])
    y_ref[...] = _rms(y, og_ref[...]).astype(BF16)


def _lru(proj, h0, cprev, wts, nseq, seqlen, tt):
    m = nseq * seqlen
    nt = seqlen // tt
    row = lambda b, t: (b * nt + t, 0)
    vec = lambda b, t: (0, 0)
    return pl.pallas_call(
        functools.partial(_lru_body, tt=tt),
        grid=(nseq, nt),
        in_specs=[
            pl.BlockSpec((tt, D_LRU), row),
            pl.BlockSpec((tt, D_LRU), lambda b, t: (b * nt + t, 1)),
            pl.BlockSpec((1, 1, D_LRU), lambda b, t: (b, 0, 0)),
            pl.BlockSpec((1, HALO, D_LRU), lambda b, t: (b, 0, 0)),
            pl.BlockSpec((HALO, D_LRU), vec),
            pl.BlockSpec((1, D_LRU), vec),
            pl.BlockSpec((N_PAIRS, LRU_PAIR, LRU_PAIR), lambda b, t: (0, 0, 0)),
            pl.BlockSpec((1, D_LRU), vec),
            pl.BlockSpec((N_PAIRS, LRU_PAIR, LRU_PAIR), lambda b, t: (0, 0, 0)),
            pl.BlockSpec((1, D_LRU), vec),
            pl.BlockSpec((1, D_LRU), vec),
            pl.BlockSpec((1, D_LRU), vec),
        ],
        out_specs=[
            pl.BlockSpec((tt, D_LRU), row),
            pl.BlockSpec((1, 1, D_LRU), lambda b, t: (b, 0, 0)),
            pl.BlockSpec((1, HALO, D_LRU), lambda b, t: (b, 0, 0)),
        ],
        out_shape=[
            jax.ShapeDtypeStruct((m, D_LRU), BF16),
            jax.ShapeDtypeStruct((nseq, 1, D_LRU), F32),
            jax.ShapeDtypeStruct((nseq, HALO, D_LRU), F32),
        ],
        scratch_shapes=[
            pltpu.VMEM((tt + HALO, D_LRU), F32),
            pltpu.VMEM((tt, D_LRU), F32),
            pltpu.VMEM((tt, D_LRU), F32),
            pltpu.VMEM((tt, D_LRU), F32),
            pltpu.VMEM((tt, D_LRU), F32),
            pltpu.VMEM((1, D_LRU), F32),
        ],
        compiler_params=_params("arbitrary", "arbitrary"),
        name="rg_lru",
    )(proj, proj, h0, cprev, wts["lru_conv_w"], wts["lru_conv_b"], wts["lru_w_a"], wts["lru_b_a"],
      wts["lru_w_x"], wts["lru_b_x"], wts["lru_lambda"], wts["lru_out_g"])


def _qkv_body(q_ref, k_ref, v_ref, gq_ref, gk_ref, qb_ref, kf_ref, kb_ref, vf_ref, vb_ref):
    for h in range(N_HEADS):
        sl = slice(h * HEAD_DIM, (h + 1) * HEAD_DIM)
        qb_ref[:, sl] = (_rms(q_ref[:, sl], gq_ref[...]) * Q_SCALE).astype(BF16)
        kn = _rms(k_ref[:, sl], gk_ref[...])
        kf_ref[:, sl] = kn
        kb_ref[:, sl] = kn.astype(BF16)
    v = v_ref[...]
    vf_ref[...] = v
    vb_ref[...] = v.astype(BF16)


def _qkv(proj, gq, gk, tm):
    m = proj.shape[0]
    col0 = 2 * D_LRU // D_SB
    blk = lambda c: pl.BlockSpec((tm, D_SB), lambda i: (i, c))
    vec = pl.BlockSpec((1, HEAD_DIM), lambda i: (0, 0))
    out = pl.BlockSpec((tm, D_SB), lambda i: (i, 0))
    return pl.pallas_call(
        _qkv_body,
        grid=(m // tm,),
        in_specs=[blk(col0), blk(col0 + 1), blk(col0 + 2), vec, vec],
        out_specs=[out] * 5,
        out_shape=[jax.ShapeDtypeStruct((m, D_SB), d) for d in (BF16, F32, BF16, F32, BF16)],
        compiler_params=_params("parallel"),
        name="qkv_norm",
    )(proj, proj, proj, gq, gk)


def _neg_abs(x):
    return lax.bitcast_convert_type(lax.bitcast_convert_type(x, jnp.int32) | jnp.int32(-2 ** 31), F32)


def _sb_step(q, k, v, tri, c_ref, crows, acc_ref, arows, acols, masked):
    z = lax.dot_general(q, k, (((1,), (1,)), ((), ())), preferred_element_type=F32)
    sp = jnp.maximum(z, 0.0) + jnp.log(1.0 + jnp.exp2(_neg_abs(z))) * LOG2E
    if masked:
        mask = (lax.broadcasted_iota(jnp.int32, z.shape, 1) < lax.broadcasted_iota(jnp.int32, z.shape, 0))
        sp = jnp.where(mask, sp, 0.0)
    inner = jnp.dot(sp.astype(BF16), tri, preferred_element_type=F32)
    c = c_ref[crows, :]
    w = jnp.exp2((z - sp) - (inner + c))
    if masked:
        w = jnp.where(mask, w, 0.0)
    acc_ref[arows, acols] += jnp.dot(w.astype(BF16), v, preferred_element_type=F32)
    c_ref[crows, :] = c + jnp.sum(sp, axis=-1, keepdims=True)


NEG_BIG = -1e30


def _sb_logits(q, k, z_ref, slot):
    z_ref[slot] = lax.dot_general(q, k, (((1,), (1,)), ((), ())), preferred_element_type=F32)


def _sb_scores(z_ref, c_ref, crows, t_ref, spb_ref, slot, mode):
    if mode == "none":
        t_ref[slot] = jnp.full(t_ref.shape[1:], NEG_BIG, F32)
        spb_ref[slot] = jnp.zeros(spb_ref.shape[1:], BF16)
        return
    z = z_ref[slot]
    sp = jnp.maximum(z, 0.0) + jnp.log(1.0 + jnp.exp2(_neg_abs(z))) * LOG2E
    c = c_ref[crows, :]
    t = (z - sp) - c
    if mode == "diag":
        mask = (lax.broadcasted_iota(jnp.int32, z.shape, 1) < lax.broadcasted_iota(jnp.int32, z.shape, 0))
        sp = jnp.where(mask, sp, 0.0)
        t = jnp.where(mask, t, NEG_BIG)
    t_ref[slot] = t
    spb_ref[slot] = sp.astype(BF16)
    c_ref[crows, :] = c + jnp.sum(sp, axis=-1, keepdims=True)


def _sb_weights(tri, t_ref, spb_ref, wb_ref, slot):
    inner = jnp.dot(spb_ref[slot], tri, preferred_element_type=F32)
    wb_ref[slot] = jnp.exp2(t_ref[slot] - inner).astype(BF16)


def _sb_values(v, wb_ref, slot, acc_ref, arows):
    acc_ref[arows, :] += jnp.dot(wb_ref[slot], v, preferred_element_type=F32)


def _attn_prompt_body(q_ref, k_ref, v_ref, tri_ref, o_ref, c_ref, z_ref, t_ref, spb_ref, wb_ref, *, nsub):
    ts = SB_TILE
    base = pl.program_id(1) * nsub
    tri = tri_ref[...]
    c_ref[...] = jnp.zeros_like(c_ref)
    o_ref[...] = jnp.zeros_like(o_ref)
    rows = [pl.ds(s * ts, ts) for s in range(nsub)]
    qs = [q_ref[r, :] for r in rows]
    cols = slice(None)

    @pl.when(base == 0)
    def _():
        for t in reversed(range(nsub)):
            k = k_ref[t * ts:(t + 1) * ts, :]
            v = v_ref[t * ts:(t + 1) * ts, :]
            for s in range(t, nsub):
                _sb_step(qs[s], k, v, tri, c_ref, rows[s], o_ref, rows[s], cols, masked=(s == t))

    @pl.when(base > 0)
    def _():
        n = base + nsub

        def tile(i):
            return pl.ds(pl.multiple_of((n - 1 - i) * ts, ts), ts)

        def modes(i):
            kt = nsub - 1 - i
            return tuple("full" if (kt < 0 or s > kt) else ("diag" if s == kt else "none")
                         for s in range(nsub))

        def iteration(i, p, first=1, last=4, static_i=None):
            if last >= 4 and first <= 4:
                v = v_ref[tile(i - 3), :]
                for s in range(nsub):
                    _sb_values(v, wb_ref.at[1 - p], s, o_ref, rows[s])
            if last >= 3 and first <= 3:
                for s in range(nsub):
                    _sb_weights(tri, t_ref.at[1 - p], spb_ref.at[1 - p], wb_ref.at[p], s)
            if last >= 2 and first <= 2:
                md = modes(static_i - 1) if static_i is not None else ("full",) * nsub
                for s in range(nsub):
                    _sb_scores(z_ref.at[1 - p], c_ref, rows[s], t_ref.at[p], spb_ref.at[p], s, md[s])
            if last >= 1 and first <= 1:
                k = k_ref[tile(i), :]
                for s in range(nsub):
                    _sb_logits(qs[s], k, z_ref.at[p], s)

        nfill = 4
        assert nsub == 2
        for i in range(nfill):
            iteration(i, i & 1, last=min(i + 1, 4), static_i=i)

        def body(j, carry):
            i = nfill + 2 * j
            iteration(i, 0)
            iteration(i + 1, 1)
            return carry

        lax.fori_loop(0, (n - nfill) // 2, body, 0)
        for d in range(3):
            iteration(n + d, d & 1, first=d + 2)


def _attn_prompt(qb, kb, vb, tri, nsub):
    t = qb.shape[0]
    tq = nsub * SB_TILE
    return pl.pallas_call(
        functools.partial(_attn_prompt_body, nsub=nsub),
        grid=(N_HEADS, t // tq),
        in_specs=[
            pl.BlockSpec((tq, HEAD_DIM), lambda h, i: (i, h)),
            pl.BlockSpec((t, HEAD_DIM), lambda h, i: (0, h)),
            pl.BlockSpec((t, HEAD_DIM), lambda h, i: (0, h)),
            pl.BlockSpec((SB_TILE, SB_TILE), lambda h, i: (0, 0)),
        ],
        out_specs=pl.BlockSpec((tq, HEAD_DIM), lambda h, i: (i, h)),
        out_shape=jax.ShapeDtypeStruct((t, D_SB), F32),
        scratch_shapes=[pltpu.VMEM((tq, 1), F32),
                        pltpu.VMEM((2, nsub, SB_TILE, SB_TILE), F32),
                        pltpu.VMEM((2, nsub, SB_TILE, SB_TILE), F32),
                        pltpu.VMEM((2, nsub, SB_TILE, SB_TILE), BF16),
                        pltpu.VMEM((2, nsub, SB_TILE, SB_TILE), BF16)],
        compiler_params=_params("parallel", "arbitrary"),
        name="sb_attn_prompt",
    )(qb, kb, vb, tri)


def _attn_sample_body(q_ref, kn_ref, vn_ref, kc_ref, vc_ref, trin_ref, tri_ref, o_ref, c_ref, *, tq, chunk):
    ts = SB_TILE
    tri = tri_ref[...]
    rows = [pl.ds(h * tq, tq) for h in range(N_HEADS)]
    cols = [slice(h * HEAD_DIM, (h + 1) * HEAD_DIM) for h in range(N_HEADS)]
    qs = [q_ref[:, cl] for cl in cols]

    @pl.when(pl.program_id(1) == 0)
    def _():
        c_ref[...] = jnp.zeros_like(c_ref)
        o_ref[...] = jnp.zeros_like(o_ref)
        trin = trin_ref[...]
        for h in range(N_HEADS):
            _sb_step(qs[h], kn_ref[:, cols[h]], vn_ref[:, cols[h]], trin, c_ref, rows[h], o_ref, slice(None), cols[h], True)

    for t in reversed(range(chunk // ts)):
        for h in range(N_HEADS):
            sel = pl.ds(t * ts * N_HEADS + h, ts, stride=N_HEADS)
            k = kc_ref[sel, :].astype(BF16)
            v = vc_ref[sel, :].astype(BF16)
            _sb_step(qs[h], k, v, tri, c_ref, rows[h], o_ref, slice(None), cols[h], False)


def _attn_sample(qb, kb, vb, kc, vc, tri_new, tri, nseq, tq, chunk):
    nchunk = kc.shape[1] // (chunk * N_HEADS)
    new = pl.BlockSpec((tq, D_SB), lambda b, c: (b, 0))
    cache = pl.BlockSpec((None, chunk * N_HEADS, HEAD_DIM), lambda b, c: (b, nchunk - 1 - c, 0))
    return pl.pallas_call(
        functools.partial(_attn_sample_body, tq=tq, chunk=chunk),
        grid=(nseq, nchunk),
        in_specs=[new, new, new, cache, cache,
                  pl.BlockSpec((tq, tq), lambda b, c: (0, 0)),
                  pl.BlockSpec((SB_TILE, SB_TILE), lambda b, c: (0, 0))],
        out_specs=new,
        out_shape=jax.ShapeDtypeStruct((nseq * tq, D_SB), F32),
        scratch_shapes=[pltpu.VMEM((N_HEADS * tq, 1), F32)],
        compiler_params=_params("parallel", "arbitrary"),
        name="sb_attn_sample",
    )(qb, kb, vb, kc, vc, tri_new, tri)


def _ffn_conv_act(pre, val, ext_ref, cw_ref, cb_ref, tt):
    ext_ref[HALO:HALO + tt, :] = pre
    cw = cw_ref[...]
    pc = cb_ref[...]
    for k in range(FFN_CONV - 1):
        lo = HALO - (FFN_CONV - 1) + k
        pc = pc + ext_ref[lo:lo + tt, :] * cw[k:k + 1, :]
    pc = pc + pre * cw[FFN_CONV - 1:FFN_CONV, :]
    return (_gelu(pc) * val).astype(BF16), ext_ref[tt:tt + HALO, :]


def _ffn_act_body(val_ref, pre_ref, sprev_ref, cw_ref, cb_ref, g_ref, sout_ref, ext_ref, *, tt):
    @pl.when(pl.program_id(2) == 0)
    def _():
        ext_ref[0:HALO, :] = sprev_ref[0]

    g, last = _ffn_conv_act(pre_ref[...], val_ref[...], ext_ref, cw_ref, cb_ref, tt)
    g_ref[...] = g
    sout_ref[0] = last
    ext_ref[0:HALO, :] = last


def _ffn_act(up, sprev, cw, cb, nseq, seqlen, tt, tc):
    m = nseq * seqlen
    nt = seqlen // tt
    nc = D_FF // tc
    return pl.pallas_call(
        functools.partial(_ffn_act_body, tt=tt),
        grid=(nseq, nc, nt),
        in_specs=[
            pl.BlockSpec((tt, tc), lambda b, c, t: (b * nt + t, c)),
            pl.BlockSpec((tt, tc), lambda b, c, t: (b * nt + t, nc + c)),
            pl.BlockSpec((1, HALO, tc), lambda b, c, t: (b, 0, c)),
            pl.BlockSpec((HALO, tc), lambda b, c, t: (0, c)),
            pl.BlockSpec((1, tc), lambda b, c, t: (0, c)),
        ],
        out_specs=[
            pl.BlockSpec((tt, tc), lambda b, c, t: (b * nt + t, c)),
            pl.BlockSpec((1, HALO, tc), lambda b, c, t: (b, 0, c)),
        ],
        out_shape=[
            jax.ShapeDtypeStruct((m, D_FF), BF16),
            jax.ShapeDtypeStruct((nseq, HALO, D_FF), F32),
        ],
        scratch_shapes=[pltpu.VMEM((tt + HALO, tc), F32)],
        compiler_params=_params("parallel", "parallel", "arbitrary"),
        name="ffn_act",
    )(up, up, sprev, cw, cb)


def _ffn_up_act_body(x_ref, g_ref, wv_ref, wp_ref, sprev_ref, cw_ref, cb_ref, o_ref, last_ref,
                     hn_ref, halo_ref, ext_ref, *, tm):
    i = pl.program_id(0)
    j = pl.program_id(1)

    @pl.when(j == 0)
    def _():
        hn_ref[...] = _rms(x_ref[...], g_ref[...]).astype(BF16)

    @pl.when(i == 0)
    def _():
        ext_ref[0:HALO, :] = sprev_ref[...]

    @pl.when(i > 0)
    def _():
        ext_ref[0:HALO, :] = halo_ref[j]

    hn = hn_ref[...]
    val = jnp.dot(hn, wv_ref[...], preferred_element_type=F32)
    pre = jnp.dot(hn, wp_ref[...], preferred_element_type=F32)
    g, last = _ffn_conv_act(pre, val, ext_ref, cw_ref, cb_ref, tm)
    o_ref[...] = g
    halo_ref[j] = last
    last_ref[0] = last


def _ffn_up_act(x, g, w_up, sprev, cw, cb, tm, tn):
    m, k = x.shape
    nj = D_FF // tn
    return pl.pallas_call(
        functools.partial(_ffn_up_act_body, tm=tm),
        grid=(m // tm, nj),
        in_specs=[
            pl.BlockSpec((tm, k), lambda i, j: (i, 0)),
            pl.BlockSpec((1, k), lambda i, j: (0, 0)),
            pl.BlockSpec((k, tn), lambda i, j: (0, j)),
            pl.BlockSpec((k, tn), lambda i, j: (0, nj + j)),
            pl.BlockSpec((HALO, tn), lambda i, j: (0, j)),
            pl.BlockSpec((HALO, tn), lambda i, j: (0, j)),
            pl.BlockSpec((1, tn), lambda i, j: (0, j)),
        ],
        out_specs=[
            pl.BlockSpec((tm, tn), lambda i, j: (i, j)),
            pl.BlockSpec((1, HALO, tn), lambda i, j: (i, 0, j)),
        ],
        out_shape=[
            jax.ShapeDtypeStruct((m, D_FF), BF16),
            jax.ShapeDtypeStruct((m // tm, HALO, D_FF), F32),
        ],
        scratch_shapes=[
            pltpu.VMEM((tm, k), BF16),
            pltpu.VMEM((nj, HALO, tn), F32),
            pltpu.VMEM((tm + HALO, tn), F32),
        ],
        compiler_params=_params("arbitrary", "arbitrary"),
        name="ffn_up_act",
    )(x, g, w_up, w_up, sprev, cw, cb)


def _front_pad(x, rows):
    return jnp.pad(x, ((0, 0), (rows - x.shape[1], 0), (0, 0)))


def _layer(x3, kpast, vpast, h0, cprev, fprev, wts, tri, tm, tt_lru, tt_ffn, attn_nsub=2, fuse_ffn=None):
    nseq, seqlen, _ = x3.shape
    m = nseq * seqlen
    x = x3.reshape(m, D_MODEL)
    if fuse_ffn is None:
        fuse_ffn = nseq == 1

    proj = _norm_matmul(x, wts["norm_mix_g"], wts["w_in"], tm, 1024, "in_proj")
    y_lru, h_last, cstate = _lru(proj, h0[:, None, :], _front_pad(cprev, HALO), wts, nseq, seqlen, tt_lru)
    qb, kf, kb, vf, vb = _qkv(proj, wts["q_norm_g"], wts["k_norm_g"], tm)
    if kpast is None:
        y_sb = _attn_prompt(qb, kb, vb, tri, attn_nsub)
    else:
        past = kpast.shape[1]
        y_sb = _attn_sample(qb, kb, vb, kpast.reshape(nseq, past * N_HEADS, HEAD_DIM),
                            vpast.reshape(nseq, past * N_HEADS, HEAD_DIM),
                            tri[:seqlen, :seqlen], tri, nseq, seqlen, min(past, 1024))
    x1 = _outproj(y_lru, y_sb, wts["sb_out_g"], wts["w_out"], x, tm, 1024)

    fprev8 = _front_pad(fprev, HALO)
    if fuse_ffn:
        g, lasts = _ffn_up_act(x1, wts["norm_ffn_g"], wts["w_ffn_up"], fprev8[0], wts["ffn_conv_w"],
                               wts["ffn_conv_b"], tm, 512)
        fstate = lasts[-1:]
    else:
        up = _norm_matmul(x1, wts["norm_ffn_g"], wts["w_ffn_up"], tm, 1024, "ffn_up")
        g, fstate = _ffn_act(up, fprev8, wts["ffn_conv_w"], wts["ffn_conv_b"], nseq, seqlen, tt_ffn, 2048)
    out = _down(g, wts["w_ffn_down"], x1, min(m, 1024), 1024, 2048)

    return (out.reshape(nseq, seqlen, D_MODEL),
            kf.reshape(nseq, seqlen, N_HEADS, HEAD_DIM),
            vf.reshape(nseq, seqlen, N_HEADS, HEAD_DIM),
            h_last[:, 0, :],
            cstate[:, HALO - (LRU_CONV - 1):, :],
            fstate[:, HALO - (FFN_CONV - 1):, :])


def _pair_blocks(w):
    z = jnp.zeros((N_PAIRS, LRU_BLOCK_DIM, LRU_BLOCK_DIM), w.dtype)
    top = jnp.concatenate([w[0::2], z], axis=2)
    bot = jnp.concatenate([z, w[1::2]], axis=2)
    return jnp.concatenate([top, bot], axis=1).astype(BF16)


def _prep_weights(norm_mix_g, w_in, lru_conv_w, lru_conv_b, lru_w_a, lru_b_a, lru_w_x, lru_b_x,
                  lru_lambda, q_norm_g, k_norm_g, lru_out_g, sb_out_g, w_out, norm_ffn_g,
                  w_ffn_up, ffn_conv_w, ffn_conv_b, w_ffn_down):
    row = lambda v: v.reshape(1, -1)
    pad_rows = lambda w: jnp.pad(w, ((0, HALO - w.shape[0]), (0, 0)))
    return {
        "norm_mix_g": row(norm_mix_g), "w_in": w_in.astype(BF16),
        "lru_conv_w": pad_rows(lru_conv_w), "lru_conv_b": row(lru_conv_b),
        "lru_w_a": _pair_blocks(lru_w_a), "lru_b_a": row(lru_b_a),
        "lru_w_x": _pair_blocks(lru_w_x), "lru_b_x": row(lru_b_x),
        "lru_lambda": row(lru_lambda),
        "q_norm_g": row(q_norm_g), "k_norm_g": row(k_norm_g),
        "lru_out_g": row(lru_out_g), "sb_out_g": row(sb_out_g), "w_out": w_out.astype(BF16),
        "norm_ffn_g": row(norm_ffn_g), "w_ffn_up": w_ffn_up.astype(BF16),
        "ffn_conv_w": pad_rows(ffn_conv_w), "ffn_conv_b": row(ffn_conv_b),
        "w_ffn_down": w_ffn_down.astype(BF16),
    }


def kernel(x_prompt, x_sample, cache_sb_k, cache_sb_v, state_lru_h, state_lru_conv, state_ffn_conv, norm_mix_g, w_in, lru_conv_w, lru_conv_b, lru_w_a, lru_b_a, lru_w_x, lru_b_x, lru_lambda, q_norm_g, k_norm_g, lru_out_g, sb_out_g, w_out, norm_ffn_g, w_ffn_up, ffn_conv_w, ffn_conv_b, w_ffn_down):
    depth = w_in.shape[0]
    bp = x_prompt.shape[0]
    tri = jnp.tri(SB_TILE, k=-1, dtype=BF16)
    xp, xs = x_prompt, x_sample
    st_p, st_s = [], []
    for l in range(depth):
        wts = _prep_weights(norm_mix_g[l], w_in[l], lru_conv_w[l], lru_conv_b[l], lru_w_a[l], lru_b_a[l],
                            lru_w_x[l], lru_b_x[l], lru_lambda[l], q_norm_g[l], k_norm_g[l], lru_out_g[l],
                            sb_out_g[l], w_out[l], norm_ffn_g[l], w_ffn_up[l], ffn_conv_w[l],
                            ffn_conv_b[l], w_ffn_down[l])
        xp, *sp = _layer(xp, None, None,
                         jnp.zeros((bp, D_LRU), F32),
                         jnp.zeros((bp, LRU_CONV - 1, D_LRU), F32),
                         jnp.zeros((bp, FFN_CONV - 1, D_FF), F32),
                         wts, tri, tm=512, tt_lru=128, tt_ffn=256)
        st_p.append(sp)
        xs, *ss = _layer(xs, cache_sb_k[l], cache_sb_v[l], state_lru_h[l], state_lru_conv[l],
                         state_ffn_conv[l], wts, tri, tm=xs.shape[0] * xs.shape[1],
                         tt_lru=xs.shape[1], tt_ffn=xs.shape[1])
        st_s.append(ss)
    stack = lambda sts, i: jnp.stack([s[i] for s in sts])
    return (xp, xs) + tuple(stack(st_p, i) for i in range(5)) + tuple(stack(st_s, i) for i in range(5))
```

```python
import functools
import math

import jax
import jax.numpy as jnp
from jax import lax
from jax.experimental import pallas as pl
from jax.experimental.pallas import tpu as pltpu

F32 = jnp.float32
BF16 = jnp.bfloat16

D_MODEL = 4096
HEAD_DIM = 128
N_HEADS = 8
D_SB = N_HEADS * HEAD_DIM
D_LRU = D_MODEL - D_SB
LRU_BLOCKS = 16
LRU_BLOCK_DIM = D_LRU // LRU_BLOCKS
LRU_PAIR = 2 * LRU_BLOCK_DIM
N_PAIRS = LRU_BLOCKS // 2
LRU_CONV = 4
LRU_C = 8.0
D_FF = 3 * D_MODEL
FFN_CONV = 3
EPS = 1e-6
LOG2E = math.log2(math.e)
Q_SCALE = HEAD_DIM ** -0.5 * LOG2E
HALO = 8
SB_TILE = 256

VMEM_LIMIT = 56 * 1024 * 1024


def _params(*sem):
    return pltpu.CompilerParams(dimension_semantics=sem, vmem_limit_bytes=VMEM_LIMIT)


def _rms(x, g):
    ms = jnp.mean(x * x, axis=-1, keepdims=True)
    return (x * lax.rsqrt(ms + EPS)) * g


def _gelu(x):
    return x * (0.5 * (1.0 + jnp.tanh(0.7978845608028654 * (x + 0.044715 * (x * x * x)))))


def _sigmoid(x):
    return 1.0 / (1.0 + jnp.exp(-x))


def _softplus(x):
    return jnp.maximum(x, 0.0) + jnp.log1p(jnp.exp(-jnp.abs(x)))


def _norm_matmul_body(x_ref, g_ref, w_ref, o_ref, hn_ref):
    @pl.when(pl.program_id(1) == 0)
    def _():
        hn_ref[...] = _rms(x_ref[...], g_ref[...]).astype(BF16)

    o_ref[...] = jnp.dot(hn_ref[...], w_ref[...], preferred_element_type=F32)


def _norm_matmul(x, g, w, tm, tn, name):
    m, k = x.shape
    n = w.shape[1]
    return pl.pallas_call(
        _norm_matmul_body,
        grid=(m // tm, n // tn),
        in_specs=[
            pl.BlockSpec((tm, k), lambda i, j: (i, 0)),
            pl.BlockSpec((1, k), lambda i, j: (0, 0)),
            pl.BlockSpec((k, tn), lambda i, j: (0, j)),
        ],
        out_specs=pl.BlockSpec((tm, tn), lambda i, j: (i, j)),
        out_shape=jax.ShapeDtypeStruct((m, n), F32),
        scratch_shapes=[pltpu.VMEM((tm, k), BF16)],
        compiler_params=_params("parallel", "arbitrary"),
        name=name,
    )(x, g, w)


def _outproj_body(a_ref, y_ref, gsb_ref, w1_ref, w2_ref, res_ref, o_ref, yn_ref):
    @pl.when(pl.program_id(1) == 0)
    def _():
        yn_ref[...] = _rms(y_ref[...], gsb_ref[...]).astype(BF16)

    acc = jnp.dot(a_ref[...], w1_ref[...], preferred_element_type=F32)
    acc = acc + jnp.dot(yn_ref[...], w2_ref[...], preferred_element_type=F32)
    o_ref[...] = res_ref[...] + acc


def _outproj(a, y_sb, g_sb, w_out, res, tm, tn):
    m = a.shape[0]
    n = w_out.shape[1]
    return pl.pallas_call(
        _outproj_body,
        grid=(m // tm, n // tn),
        in_specs=[
            pl.BlockSpec((tm, D_LRU), lambda i, j: (i, 0)),
            pl.BlockSpec((tm, D_SB), lambda i, j: (i, 0)),
            pl.BlockSpec((1, D_SB), lambda i, j: (0, 0)),
            pl.BlockSpec((D_LRU, tn), lambda i, j: (0, j)),
            pl.BlockSpec((D_SB, tn), lambda i, j: (D_LRU // D_SB, j)),
            pl.BlockSpec((tm, tn), lambda i, j: (i, j)),
        ],
        out_specs=pl.BlockSpec((tm, tn), lambda i, j: (i, j)),
        out_shape=jax.ShapeDtypeStruct((m, n), F32),
        scratch_shapes=[pltpu.VMEM((tm, D_SB), BF16)],
        compiler_params=_params("parallel", "arbitrary"),
        name="out_proj",
    )(a, y_sb, g_sb, w_out, w_out, res)


def _down_body(g_ref, w_ref, res_ref, o_ref, acc_ref):
    k = pl.program_id(2)

    @pl.when(k == 0)
    def _():
        acc_ref[...] = jnp.zeros_like(acc_ref)

    acc_ref[...] += jnp.dot(g_ref[...], w_ref[...], preferred_element_type=F32)

    @pl.when(k == pl.num_programs(2) - 1)
    def _():
        o_ref[...] = res_ref[...] + acc_ref[...]


def _down(g, w, res, tm, tn, tk):
    m, kdim = g.shape
    n = w.shape[1]
    return pl.pallas_call(
        _down_body,
        grid=(m // tm, n // tn, kdim // tk),
        in_specs=[
            pl.BlockSpec((tm, tk), lambda i, j, k: (i, k)),
            pl.BlockSpec((tk, tn), lambda i, j, k: (k, j)),
            pl.BlockSpec((tm, tn), lambda i, j, k: (i, j)),
        ],
        out_specs=pl.BlockSpec((tm, tn), lambda i, j, k: (i, j)),
        out_shape=jax.ShapeDtypeStruct((m, n), F32),
        scratch_shapes=[pltpu.VMEM((tm, tn), F32)],
        compiler_params=_params("parallel", "parallel", "arbitrary"),
        name="ffn_down",
    )(g, w, res)


def _lru_body(u_ref, gate_ref, h0_ref, cprev_ref, cw_ref, cb_ref, wa_ref, ba_ref, wx_ref, bx_ref,
              lam_ref, og_ref, y_ref, hlast_ref, cstate_ref,
              ext_ref, uc_ref, a_ref, b_ref, hs_ref, hc_ref, *, tt):
    @pl.when(pl.program_id(1) == 0)
    def _():
        hc_ref[...] = h0_ref[0]
        ext_ref[0:HALO, :] = cprev_ref[0]

    u = u_ref[...]
    ext_ref[HALO:HALO + tt, :] = u
    cw = cw_ref[...]
    uc = cb_ref[...]
    for k in range(LRU_CONV - 1):
        lo = HALO - (LRU_CONV - 1) + k
        uc = uc + ext_ref[lo:lo + tt, :] * cw[k:k + 1, :]
    uc_ref[...] = uc + u * cw[LRU_CONV - 1:LRU_CONV, :]

    last = ext_ref[tt:tt + HALO, :]
    cstate_ref[0] = last
    ext_ref[0:HALO, :] = last

    for p in range(N_PAIRS):
        sl = slice(p * LRU_PAIR, (p + 1) * LRU_PAIR)
        ucp = uc_ref[:, sl]
        xb = ucp.astype(BF16)
        r = _sigmoid(jnp.dot(xb, wa_ref[p], preferred_element_type=F32) + ba_ref[:, sl])
        i = _sigmoid(jnp.dot(xb, wx_ref[p], preferred_element_type=F32) + bx_ref[:, sl])
        log_a = (LRU_C * r) * (-_softplus(-lam_ref[:, sl]))
        a = jnp.exp(log_a)
        a_ref[:, sl] = a
        b_ref[:, sl] = jnp.sqrt(-jnp.tanh(log_a) * (a * a + 1.0)) * (i * ucp)

    def step(t, h):
        h = a_ref[pl.ds(t, 1), :] * h + b_ref[pl.ds(t, 1), :]
        hs_ref[pl.ds(t, 1), :] = h
        return h

    h = lax.fori_loop(0, tt, step, hc_ref[...], unroll=8)
    hc_ref[...] = h
    hlast_ref[0] = h

    y = hs_ref[...] * _gelu(gate_ref[...])
    y_ref[...] = _rms(y, og_ref[...]).astype(BF16)


def _lru(proj, h0, cprev, wts, nseq, seqlen, tt):
    m = nseq * seqlen
    nt = seqlen // tt
    row = lambda b, t: (b * nt + t, 0)
    vec = lambda b, t: (0, 0)
    return pl.pallas_call(
        functools.partial(_lru_body, tt=tt),
        grid=(nseq, nt),
        in_specs=[
            pl.BlockSpec((tt, D_LRU), row),
            pl.BlockSpec((tt, D_LRU), lambda b, t: (b * nt + t, 1)),
            pl.BlockSpec((1, 1, D_LRU), lambda b, t: (b, 0, 0)),
            pl.BlockSpec((1, HALO, D_LRU), lambda b, t: (b, 0, 0)),
            pl.BlockSpec((HALO, D_LRU), vec),
            pl.BlockSpec((1, D_LRU), vec),
            pl.BlockSpec((N_PAIRS, LRU_PAIR, LRU_PAIR), lambda b, t: (0, 0, 0)),
            pl.BlockSpec((1, D_LRU), vec),
            pl.BlockSpec((N_PAIRS, LRU_PAIR, LRU_PAIR), lambda b, t: (0, 0, 0)),
            pl.BlockSpec((1, D_LRU), vec),
            pl.BlockSpec((1, D_LRU), vec),
            pl.BlockSpec((1, D_LRU), vec),
        ],
        out_specs=[
            pl.BlockSpec((tt, D_LRU), row),
            pl.BlockSpec((1, 1, D_LRU), lambda b, t: (b, 0, 0)),
            pl.BlockSpec((1, HALO, D_LRU), lambda b, t: (b, 0, 0)),
        ],
        out_shape=[
            jax.ShapeDtypeStruct((m, D_LRU), BF16),
            jax.ShapeDtypeStruct((nseq, 1, D_LRU), F32),
            jax.ShapeDtypeStruct((nseq, HALO, D_LRU), F32),
        ],
        scratch_shapes=[
            pltpu.VMEM((tt + HALO, D_LRU), F32),
            pltpu.VMEM((tt, D_LRU), F32),
            pltpu.VMEM((tt, D_LRU), F32),
            pltpu.VMEM((tt, D_LRU), F32),
            pltpu.VMEM((tt, D_LRU), F32),
            pltpu.VMEM((1, D_LRU), F32),
        ],
        compiler_params=_params("arbitrary", "arbitrary"),
        name="rg_lru",
    )(proj, proj, h0, cprev, wts["lru_conv_w"], wts["lru_conv_b"], wts["lru_w_a"], wts["lru_b_a"],
      wts["lru_w_x"], wts["lru_b_x"], wts["lru_lambda"], wts["lru_out_g"])


def _qkv_body(q_ref, k_ref, v_ref, gq_ref, gk_ref, qb_ref, kf_ref, kb_ref, vf_ref, vb_ref):
    for h in range(N_HEADS):
        sl = slice(h * HEAD_DIM, (h + 1) * HEAD_DIM)
        qb_ref[:, sl] = (_rms(q_ref[:, sl], gq_ref[...]) * Q_SCALE).astype(BF16)
        kn = _rms(k_ref[:, sl], gk_ref[...])
        kf_ref[:, sl] = kn
        kb_ref[:, sl] = kn.astype(BF16)
    v = v_ref[...]
    vf_ref[...] = v
    vb_ref[...] = v.astype(BF16)


def _qkv(proj, gq, gk, tm):
    m = proj.shape[0]
    col0 = 2 * D_LRU // D_SB
    blk = lambda c: pl.BlockSpec((tm, D_SB), lambda i: (i, c))
    vec = pl.BlockSpec((1, HEAD_DIM), lambda i: (0, 0))
    out = pl.BlockSpec((tm, D_SB), lambda i: (i, 0))
    return pl.pallas_call(
        _qkv_body,
        grid=(m // tm,),
        in_specs=[blk(col0), blk(col0 + 1), blk(col0 + 2), vec, vec],
        out_specs=[out] * 5,
        out_shape=[jax.ShapeDtypeStruct((m, D_SB), d) for d in (BF16, F32, BF16, F32, BF16)],
        compiler_params=_params("parallel"),
        name="qkv_norm",
    )(proj, proj, proj, gq, gk)


def _neg_abs(x):
    return lax.bitcast_convert_type(lax.bitcast_convert_type(x, jnp.int32) | jnp.int32(-2 ** 31), F32)


def _sb_step(q, k, v, tri, c_ref, crows, acc_ref, arows, acols, masked):
    z = lax.dot_general(q, k, (((1,), (1,)), ((), ())), preferred_element_type=F32)
    sp = jnp.maximum(z, 0.0) + jnp.log(1.0 + jnp.exp2(_neg_abs(z))) * LOG2E
    if masked:
        mask = (lax.broadcasted_iota(jnp.int32, z.shape, 1) < lax.broadcasted_iota(jnp.int32, z.shape, 0))
        sp = jnp.where(mask, sp, 0.0)
    inner = jnp.dot(sp.astype(BF16), tri, preferred_element_type=F32)
    c = c_ref[crows, :]
    w = jnp.exp2((z - sp) - (inner + c))
    if masked:
        w = jnp.where(mask, w, 0.0)
    acc_ref[arows, acols] += jnp.dot(w.astype(BF16), v, preferred_element_type=F32)
    c_ref[crows, :] = c + jnp.sum(sp, axis=-1, keepdims=True)


NEG_BIG = -1e30
SKIP_LOG2 = 160.0


def _sb_logits(q, k, z_ref, slot):
    z_ref[slot] = lax.dot_general(q, k, (((1,), (1,)), ((), ())), preferred_element_type=F32)


def _sb_scores(z_ref, c_ref, crows, t_ref, spb_ref, slot, mode):
    if mode == "none":
        t_ref[slot] = jnp.full(t_ref.shape[1:], NEG_BIG, F32)
        spb_ref[slot] = jnp.zeros(spb_ref.shape[1:], BF16)
        return
    z = z_ref[slot]
    sp = jnp.maximum(z, 0.0) + jnp.log(1.0 + jnp.exp2(_neg_abs(z))) * LOG2E
    c = c_ref[crows, :]
    t = (z - sp) - c
    if mode == "diag":
        mask = (lax.broadcasted_iota(jnp.int32, z.shape, 1) < lax.broadcasted_iota(jnp.int32, z.shape, 0))
        sp = jnp.where(mask, sp, 0.0)
        t = jnp.where(mask, t, NEG_BIG)
    t_ref[slot] = t
    spb_ref[slot] = sp.astype(BF16)
    c_ref[crows, :] = c + jnp.sum(sp, axis=-1, keepdims=True)


def _sb_weights(tri, t_ref, spb_ref, wb_ref, slot):
    inner = jnp.dot(spb_ref[slot], tri, preferred_element_type=F32)
    wb_ref[slot] = jnp.exp2(t_ref[slot] - inner).astype(BF16)


def _sb_values(v, wb_ref, slot, acc_ref, arows):
    acc_ref[arows, :] += jnp.dot(wb_ref[slot], v, preferred_element_type=F32)


def _attn_prompt_body(q_ref, k_ref, v_ref, tri_ref, o_ref, c_ref, z_ref, t_ref, spb_ref, wb_ref, *, nsub):
    ts = SB_TILE
    base = pl.program_id(1) * nsub
    tri = tri_ref[...]
    c_ref[...] = jnp.zeros_like(c_ref)
    o_ref[...] = jnp.zeros_like(o_ref)
    rows = [pl.ds(s * ts, ts) for s in range(nsub)]
    qs = [q_ref[r, :] for r in rows]
    cols = slice(None)

    @pl.when(base == 0)
    def _():
        for t in reversed(range(nsub)):
            k = k_ref[t * ts:(t + 1) * ts, :]
            v = v_ref[t * ts:(t + 1) * ts, :]
            for s in range(t, nsub):
                _sb_step(qs[s], k, v, tri, c_ref, rows[s], o_ref, rows[s], cols, masked=(s == t))

    @pl.when(base > 0)
    def _():
        n = base + nsub

        def tile(i):
            return pl.ds(pl.multiple_of((n - 1 - i) * ts, ts), ts)

        def modes(i):
            kt = nsub - 1 - i
            return tuple("full" if (kt < 0 or s > kt) else ("diag" if s == kt else "none")
                         for s in range(nsub))

        def iteration(i, p, first=1, last=4, static_i=None):
            if last >= 4 and first <= 4:
                v = v_ref[tile(i - 3), :]
                for s in range(nsub):
                    _sb_values(v, wb_ref.at[1 - p], s, o_ref, rows[s])
            if last >= 3 and first <= 3:
                for s in range(nsub):
                    _sb_weights(tri, t_ref.at[1 - p], spb_ref.at[1 - p], wb_ref.at[p], s)
            if last >= 2 and first <= 2:
                md = modes(static_i - 1) if static_i is not None else ("full",) * nsub
                for s in range(nsub):
                    _sb_scores(z_ref.at[1 - p], c_ref, rows[s], t_ref.at[p], spb_ref.at[p], s, md[s])
            if last >= 1 and first <= 1:
                k = k_ref[tile(i), :]
                for s in range(nsub):
                    _sb_logits(qs[s], k, z_ref.at[p], s)

        nfill = 4
        assert nsub == 2
        for i in range(nfill):
            iteration(i, i & 1, last=min(i + 1, 4), static_i=i)

        def cond(carry):
            j, cmin = carry
            return jnp.logical_and(j < (n - nfill) // 2, cmin < SKIP_LOG2)

        def body(carry):
            j, _ = carry
            i = nfill + 2 * j
            iteration(i, 0)
            iteration(i + 1, 1)
            return j + 1, jnp.min(c_ref[...])

        trips, _ = lax.while_loop(cond, body, (jnp.int32(0), jnp.min(c_ref[...])))
        issued = nfill + 2 * trips
        for d in range(3):
            iteration(issued + d, d & 1, first=d + 2)


def _attn_prompt(qb, kb, vb, tri, nsub):
    t = qb.shape[0]
    tq = nsub * SB_TILE
    return pl.pallas_call(
        functools.partial(_attn_prompt_body, nsub=nsub),
        grid=(N_HEADS, t // tq),
        in_specs=[
            pl.BlockSpec((tq, HEAD_DIM), lambda h, i: (i, h)),
            pl.BlockSpec((t, HEAD_DIM), lambda h, i: (0, h)),
            pl.BlockSpec((t, HEAD_DIM), lambda h, i: (0, h)),
            pl.BlockSpec((SB_TILE, SB_TILE), lambda h, i: (0, 0)),
        ],
        out_specs=pl.BlockSpec((tq, HEAD_DIM), lambda h, i: (i, h)),
        out_shape=jax.ShapeDtypeStruct((t, D_SB), F32),
        scratch_shapes=[pltpu.VMEM((tq, 1), F32),
                        pltpu.VMEM((2, nsub, SB_TILE, SB_TILE), F32),
                        pltpu.VMEM((2, nsub, SB_TILE, SB_TILE), F32),
                        pltpu.VMEM((2, nsub, SB_TILE, SB_TILE), BF16),
                        pltpu.VMEM((2, nsub, SB_TILE, SB_TILE), BF16)],
        compiler_params=_params("parallel", "arbitrary"),
        name="sb_attn_prompt",
    )(qb, kb, vb, tri)


def _attn_sample_body(q_ref, kn_ref, vn_ref, kc_ref, vc_ref, trin_ref, tri_ref, o_ref, c_ref, *, tq, chunk):
    ts = SB_TILE
    tri = tri_ref[...]
    rows = [pl.ds(h * tq, tq) for h in range(N_HEADS)]
    cols = [slice(h * HEAD_DIM, (h + 1) * HEAD_DIM) for h in range(N_HEADS)]
    qs = [q_ref[:, cl] for cl in cols]

    @pl.when(pl.program_id(1) == 0)
    def _():
        c_ref[...] = jnp.zeros_like(c_ref)
        o_ref[...] = jnp.zeros_like(o_ref)
        trin = trin_ref[...]
        for h in range(N_HEADS):
            _sb_step(qs[h], kn_ref[:, cols[h]], vn_ref[:, cols[h]], trin, c_ref, rows[h], o_ref, slice(None), cols[h], True)

    for t in reversed(range(chunk // ts)):
        for h in range(N_HEADS):
            sel = pl.ds(t * ts * N_HEADS + h, ts, stride=N_HEADS)
            k = kc_ref[sel, :].astype(BF16)
            v = vc_ref[sel, :].astype(BF16)
            _sb_step(qs[h], k, v, tri, c_ref, rows[h], o_ref, slice(None), cols[h], False)


def _attn_sample(qb, kb, vb, kc, vc, tri_new, tri, nseq, tq, chunk):
    nchunk = kc.shape[1] // (chunk * N_HEADS)
    new = pl.BlockSpec((tq, D_SB), lambda b, c: (b, 0))
    cache = pl.BlockSpec((None, chunk * N_HEADS, HEAD_DIM), lambda b, c: (b, nchunk - 1 - c, 0))
    return pl.pallas_call(
        functools.partial(_attn_sample_body, tq=tq, chunk=chunk),
        grid=(nseq, nchunk),
        in_specs=[new, new, new, cache, cache,
                  pl.BlockSpec((tq, tq), lambda b, c: (0, 0)),
                  pl.BlockSpec((SB_TILE, SB_TILE), lambda b, c: (0, 0))],
        out_specs=new,
        out_shape=jax.ShapeDtypeStruct((nseq * tq, D_SB), F32),
        scratch_shapes=[pltpu.VMEM((N_HEADS * tq, 1), F32)],
        compiler_params=_params("parallel", "arbitrary"),
        name="sb_attn_sample",
    )(qb, kb, vb, kc, vc, tri_new, tri)


def _ffn_conv_act(pre, val, ext_ref, cw_ref, cb_ref, tt):
    ext_ref[HALO:HALO + tt, :] = pre
    cw = cw_ref[...]
    pc = cb_ref[...]
    for k in range(FFN_CONV - 1):
        lo = HALO - (FFN_CONV - 1) + k
        pc = pc + ext_ref[lo:lo + tt, :] * cw[k:k + 1, :]
    pc = pc + pre * cw[FFN_CONV - 1:FFN_CONV, :]
    return (_gelu(pc) * val).astype(BF16), ext_ref[tt:tt + HALO, :]


def _ffn_act_body(val_ref, pre_ref, sprev_ref, cw_ref, cb_ref, g_ref, sout_ref, ext_ref, *, tt):
    @pl.when(pl.program_id(2) == 0)
    def _():
        ext_ref[0:HALO, :] = sprev_ref[0]

    g, last = _ffn_conv_act(pre_ref[...], val_ref[...], ext_ref, cw_ref, cb_ref, tt)
    g_ref[...] = g
    sout_ref[0] = last
    ext_ref[0:HALO, :] = last


def _ffn_act(up, sprev, cw, cb, nseq, seqlen, tt, tc):
    m = nseq * seqlen
    nt = seqlen // tt
    nc = D_FF // tc
    return pl.pallas_call(
        functools.partial(_ffn_act_body, tt=tt),
        grid=(nseq, nc, nt),
        in_specs=[
            pl.BlockSpec((tt, tc), lambda b, c, t: (b * nt + t, c)),
            pl.BlockSpec((tt, tc), lambda b, c, t: (b * nt + t, nc + c)),
            pl.BlockSpec((1, HALO, tc), lambda b, c, t: (b, 0, c)),
            pl.BlockSpec((HALO, tc), lambda b, c, t: (0, c)),
            pl.BlockSpec((1, tc), lambda b, c, t: (0, c)),
        ],
        out_specs=[
            pl.BlockSpec((tt, tc), lambda b, c, t: (b * nt + t, c)),
            pl.BlockSpec((1, HALO, tc), lambda b, c, t: (b, 0, c)),
        ],
        out_shape=[
            jax.ShapeDtypeStruct((m, D_FF), BF16),
            jax.ShapeDtypeStruct((nseq, HALO, D_FF), F32),
        ],
        scratch_shapes=[pltpu.VMEM((tt + HALO, tc), F32)],
        compiler_params=_params("parallel", "parallel", "arbitrary"),
        name="ffn_act",
    )(up, up, sprev, cw, cb)


def _ffn_up_act_body(x_ref, g_ref, wv_ref, wp_ref, sprev_ref, cw_ref, cb_ref, o_ref, last_ref,
                     hn_ref, halo_ref, ext_ref, *, tm):
    i = pl.program_id(0)
    j = pl.program_id(1)

    @pl.when(j == 0)
    def _():
        hn_ref[...] = _rms(x_ref[...], g_ref[...]).astype(BF16)

    @pl.when(i == 0)
    def _():
        ext_ref[0:HALO, :] = sprev_ref[...]

    @pl.when(i > 0)
    def _():
        ext_ref[0:HALO, :] = halo_ref[j]

    hn = hn_ref[...]
    val = jnp.dot(hn, wv_ref[...], preferred_element_type=F32)
    pre = jnp.dot(hn, wp_ref[...], preferred_element_type=F32)
    g, last = _ffn_conv_act(pre, val, ext_ref, cw_ref, cb_ref, tm)
    o_ref[...] = g
    halo_ref[j] = last
    last_ref[0] = last


def _ffn_up_act(x, g, w_up, sprev, cw, cb, tm, tn):
    m, k = x.shape
    nj = D_FF // tn
    return pl.pallas_call(
        functools.partial(_ffn_up_act_body, tm=tm),
        grid=(m // tm, nj),
        in_specs=[
            pl.BlockSpec((tm, k), lambda i, j: (i, 0)),
            pl.BlockSpec((1, k), lambda i, j: (0, 0)),
            pl.BlockSpec((k, tn), lambda i, j: (0, j)),
            pl.BlockSpec((k, tn), lambda i, j: (0, nj + j)),
            pl.BlockSpec((HALO, tn), lambda i, j: (0, j)),
            pl.BlockSpec((HALO, tn), lambda i, j: (0, j)),
            pl.BlockSpec((1, tn), lambda i, j: (0, j)),
        ],
        out_specs=[
            pl.BlockSpec((tm, tn), lambda i, j: (i, j)),
            pl.BlockSpec((1, HALO, tn), lambda i, j: (i, 0, j)),
        ],
        out_shape=[
            jax.ShapeDtypeStruct((m, D_FF), BF16),
            jax.ShapeDtypeStruct((m // tm, HALO, D_FF), F32),
        ],
        scratch_shapes=[
            pltpu.VMEM((tm, k), BF16),
            pltpu.VMEM((nj, HALO, tn), F32),
            pltpu.VMEM((tm + HALO, tn), F32),
        ],
        compiler_params=_params("arbitrary", "arbitrary"),
        name="ffn_up_act",
    )(x, g, w_up, w_up, sprev, cw, cb)


def _front_pad(x, rows):
    return jnp.pad(x, ((0, 0), (rows - x.shape[1], 0), (0, 0)))


def _layer(x3, kpast, vpast, h0, cprev, fprev, wts, tri, tm, tt_lru, tt_ffn, attn_nsub=2, fuse_ffn=None):
    nseq, seqlen, _ = x3.shape
    m = nseq * seqlen
    x = x3.reshape(m, D_MODEL)
    if fuse_ffn is None:
        fuse_ffn = nseq == 1

    proj = _norm_matmul(x, wts["norm_mix_g"], wts["w_in"], tm, 1024, "in_proj")
    y_lru, h_last, cstate = _lru(proj, h0[:, None, :], _front_pad(cprev, HALO), wts, nseq, seqlen, tt_lru)
    qb, kf, kb, vf, vb = _qkv(proj, wts["q_norm_g"], wts["k_norm_g"], tm)
    if kpast is None:
        y_sb = _attn_prompt(qb, kb, vb, tri, attn_nsub)
    else:
        past = kpast.shape[1]
        y_sb = _attn_sample(qb, kb, vb, kpast.reshape(nseq, past * N_HEADS, HEAD_DIM),
                            vpast.reshape(nseq, past * N_HEADS, HEAD_DIM),
                            tri[:seqlen, :seqlen], tri, nseq, seqlen, min(past, 1024))
    x1 = _outproj(y_lru, y_sb, wts["sb_out_g"], wts["w_out"], x, tm, 1024)

    fprev8 = _front_pad(fprev, HALO)
    if fuse_ffn:
        g, lasts = _ffn_up_act(x1, wts["norm_ffn_g"], wts["w_ffn_up"], fprev8[0], wts["ffn_conv_w"],
                               wts["ffn_conv_b"], tm, 512)
        fstate = lasts[-1:]
    else:
        up = _norm_matmul(x1, wts["norm_ffn_g"], wts["w_ffn_up"], tm, 1024, "ffn_up")
        g, fstate = _ffn_act(up, fprev8, wts["ffn_conv_w"], wts["ffn_conv_b"], nseq, seqlen, tt_ffn, 2048)
    out = _down(g, wts["w_ffn_down"], x1, min(m, 1024), 1024, 2048)

    return (out.reshape(nseq, seqlen, D_MODEL),
            kf.reshape(nseq, seqlen, N_HEADS, HEAD_DIM),
            vf.reshape(nseq, seqlen, N_HEADS, HEAD_DIM),
            h_last[:, 0, :],
            cstate[:, HALO - (LRU_CONV - 1):, :],
            fstate[:, HALO - (FFN_CONV - 1):, :])


def _pair_blocks(w):
    z = jnp.zeros((N_PAIRS, LRU_BLOCK_DIM, LRU_BLOCK_DIM), w.dtype)
    top = jnp.concatenate([w[0::2], z], axis=2)
    bot = jnp.concatenate([z, w[1::2]], axis=2)
    return jnp.concatenate([top, bot], axis=1).astype(BF16)


def _prep_weights(norm_mix_g, w_in, lru_conv_w, lru_conv_b, lru_w_a, lru_b_a, lru_w_x, lru_b_x,
                  lru_lambda, q_norm_g, k_norm_g, lru_out_g, sb_out_g, w_out, norm_ffn_g,
                  w_ffn_up, ffn_conv_w, ffn_conv_b, w_ffn_down):
    row = lambda v: v.reshape(1, -1)
    pad_rows = lambda w: jnp.pad(w, ((0, HALO - w.shape[0]), (0, 0)))
    return {
        "norm_mix_g": row(norm_mix_g), "w_in": w_in.astype(BF16),
        "lru_conv_w": pad_rows(lru_conv_w), "lru_conv_b": row(lru_conv_b),
        "lru_w_a": _pair_blocks(lru_w_a), "lru_b_a": row(lru_b_a),
        "lru_w_x": _pair_blocks(lru_w_x), "lru_b_x": row(lru_b_x),
        "lru_lambda": row(lru_lambda),
        "q_norm_g": row(q_norm_g), "k_norm_g": row(k_norm_g),
        "lru_out_g": row(lru_out_g), "sb_out_g": row(sb_out_g), "w_out": w_out.astype(BF16),
        "norm_ffn_g": row(norm_ffn_g), "w_ffn_up": w_ffn_up.astype(BF16),
        "ffn_conv_w": pad_rows(ffn_conv_w), "ffn_conv_b": row(ffn_conv_b),
        "w_ffn_down": w_ffn_down.astype(BF16),
    }


def kernel(x_prompt, x_sample, cache_sb_k, cache_sb_v, state_lru_h, state_lru_conv, state_ffn_conv, norm_mix_g, w_in, lru_conv_w, lru_conv_b, lru_w_a, lru_b_a, lru_w_x, lru_b_x, lru_lambda, q_norm_g, k_norm_g, lru_out_g, sb_out_g, w_out, norm_ffn_g, w_ffn_up, ffn_conv_w, ffn_conv_b, w_ffn_down):
    depth = w_in.shape[0]
    bp = x_prompt.shape[0]
    tri = jnp.tri(SB_TILE, k=-1, dtype=BF16)
    xp, xs = x_prompt, x_sample
    st_p, st_s = [], []
    for l in range(depth):
        wts = _prep_weights(norm_mix_g[l], w_in[l], lru_conv_w[l], lru_conv_b[l], lru_w_a[l], lru_b_a[l],
                            lru_w_x[l], lru_b_x[l], lru_lambda[l], q_norm_g[l], k_norm_g[l], lru_out_g[l],
                            sb_out_g[l], w_out[l], norm_ffn_g[l], w_ffn_up[l], ffn_conv_w[l],
                            ffn_conv_b[l], w_ffn_down[l])
        xp, *sp = _layer(xp, None, None,
                         jnp.zeros((bp, D_LRU), F32),
                         jnp.zeros((bp, LRU_CONV - 1, D_LRU), F32),
                         jnp.zeros((bp, FFN_CONV - 1, D_FF), F32),
                         wts, tri, tm=512, tt_lru=128, tt_ffn=256)
        st_p.append(sp)
        xs, *ss = _layer(xs, cache_sb_k[l], cache_sb_v[l], state_lru_h[l], state_lru_conv[l],
                         state_ffn_conv[l], wts, tri, tm=xs.shape[0] * xs.shape[1],
                         tt_lru=xs.shape[1], tt_ffn=xs.shape[1])
        st_s.append(ss)
    stack = lambda sts, i: jnp.stack([s[i] for s in sts])
    return (xp, xs) + tuple(stack(st_p, i) for i in range(5)) + tuple(stack(st_s, i) for i in range(5))
```

```python
import functools
import math

import jax
import jax.numpy as jnp
from jax import lax
from jax.experimental import pallas as pl
from jax.experimental.pallas import tpu as pltpu

F32 = jnp.float32
BF16 = jnp.bfloat16

D_MODEL = 4096
HEAD_DIM = 128
N_HEADS = 8
D_SB = N_HEADS * HEAD_DIM
D_LRU = D_MODEL - D_SB
LRU_BLOCKS = 16
LRU_BLOCK_DIM = D_LRU // LRU_BLOCKS
LRU_PAIR = 2 * LRU_BLOCK_DIM
N_PAIRS = LRU_BLOCKS // 2
LRU_CONV = 4
LRU_C = 8.0
D_FF = 3 * D_MODEL
FFN_CONV = 3
EPS = 1e-6
LOG2E = math.log2(math.e)
Q_SCALE = HEAD_DIM ** -0.5 * LOG2E
HALO = 8
SB_TILE = 256
MXU_COLS = 256

VMEM_LIMIT = 56 * 1024 * 1024


def _params(*sem):
    return pltpu.CompilerParams(dimension_semantics=sem, vmem_limit_bytes=VMEM_LIMIT)


def _rms(x, g):
    ms = jnp.mean(x * x, axis=-1, keepdims=True)
    return (x * lax.rsqrt(ms + EPS)) * g


def _gelu(x):
    return x * (0.5 * (1.0 + jnp.tanh(0.7978845608028654 * (x + 0.044715 * (x * x * x)))))


def _sigmoid(x):
    return 1.0 / (1.0 + jnp.exp(-x))


def _softplus(x):
    return jnp.maximum(x, 0.0) + jnp.log1p(jnp.exp(-jnp.abs(x)))


def _weight_block(w_ref, wb_ref):
    w = w_ref[...]
    if wb_ref is not None:
        w = w.astype(BF16)
        wb_ref[...] = w
    return w


def _norm_matmul_body(x_ref, g_ref, w_ref, o_ref, *rest, emit):
    wb_ref, hn_ref = rest if emit else (None,) + rest

    @pl.when(pl.program_id(1) == 0)
    def _():
        hn_ref[...] = _rms(x_ref[...], g_ref[...]).astype(BF16)

    o_ref[...] = jnp.dot(hn_ref[...], _weight_block(w_ref, wb_ref), preferred_element_type=F32)


def _norm_matmul(x, g, w, tm, tn, name, emit=False):
    m, k = x.shape
    n = w.shape[1]
    assert not emit or m == tm
    wspec = pl.BlockSpec((k, tn), lambda i, j: (0, j))
    ospec = pl.BlockSpec((tm, tn), lambda i, j: (i, j))
    oshape = jax.ShapeDtypeStruct((m, n), F32)
    return pl.pallas_call(
        functools.partial(_norm_matmul_body, emit=emit),
        grid=(m // tm, n // tn),
        in_specs=[
            pl.BlockSpec((tm, k), lambda i, j: (i, 0)),
            pl.BlockSpec((1, k), lambda i, j: (0, 0)),
            wspec,
        ],
        out_specs=[ospec, wspec] if emit else ospec,
        out_shape=[oshape, jax.ShapeDtypeStruct((k, n), BF16)] if emit else oshape,
        scratch_shapes=[pltpu.VMEM((tm, k), BF16)],
        compiler_params=_params("parallel", "arbitrary"),
        name=name,
    )(x, g, w)


def _outproj_body(a_ref, y_ref, gsb_ref, w1_ref, w2_ref, res_ref, o_ref, *rest, emit):
    w1b_ref, w2b_ref, yn_ref = rest if emit else (None, None) + rest

    @pl.when(pl.program_id(1) == 0)
    def _():
        yn_ref[...] = _rms(y_ref[...], gsb_ref[...]).astype(BF16)

    acc = jnp.dot(a_ref[...], _weight_block(w1_ref, w1b_ref), preferred_element_type=F32)
    acc = acc + jnp.dot(yn_ref[...], _weight_block(w2_ref, w2b_ref), preferred_element_type=F32)
    o_ref[...] = res_ref[...] + acc


def _outproj(a, y_sb, g_sb, w1, w2, res, tm, tn, emit=False):
    m = a.shape[0]
    n = w1.shape[1]
    assert not emit or m == tm
    w2_row_block = (w2.shape[0] - D_SB) // D_SB
    ospec = pl.BlockSpec((tm, tn), lambda i, j: (i, j))
    oshape = jax.ShapeDtypeStruct((m, n), F32)
    wbspecs = [pl.BlockSpec((D_LRU, tn), lambda i, j: (0, j)), pl.BlockSpec((D_SB, tn), lambda i, j: (0, j))]
    wbshapes = [jax.ShapeDtypeStruct((D_LRU, n), BF16), jax.ShapeDtypeStruct((D_SB, n), BF16)]
    return pl.pallas_call(
        functools.partial(_outproj_body, emit=emit),
        grid=(m // tm, n // tn),
        in_specs=[
            pl.BlockSpec((tm, D_LRU), lambda i, j: (i, 0)),
            pl.BlockSpec((tm, D_SB), lambda i, j: (i, 0)),
            pl.BlockSpec((1, D_SB), lambda i, j: (0, 0)),
            pl.BlockSpec((D_LRU, tn), lambda i, j: (0, j)),
            pl.BlockSpec((D_SB, tn), lambda i, j: (w2_row_block, j)),
            ospec,
        ],
        out_specs=[ospec] + wbspecs if emit else ospec,
        out_shape=[oshape] + wbshapes if emit else oshape,
        scratch_shapes=[pltpu.VMEM((tm, D_SB), BF16)],
        compiler_params=_params("parallel", "arbitrary"),
        name="out_proj",
    )(a, y_sb, g_sb, w1, w2, res)


def _down_body(g_ref, w_ref, res_ref, o_ref, *rest, emit):
    wb_ref, acc_ref = rest if emit else (None,) + rest
    k = pl.program_id(2)

    @pl.when(k == 0)
    def _():
        acc_ref[...] = jnp.zeros_like(acc_ref)

    acc_ref[...] += jnp.dot(g_ref[...], _weight_block(w_ref, wb_ref), preferred_element_type=F32)

    @pl.when(k == pl.num_programs(2) - 1)
    def _():
        o_ref[...] = res_ref[...] + acc_ref[...]


def _down(g, w, res, tm, tn, tk, emit=False):
    m, kdim = g.shape
    n = w.shape[1]
    assert not emit or m == tm
    wspec = pl.BlockSpec((tk, tn), lambda i, j, k: (k, j))
    ospec = pl.BlockSpec((tm, tn), lambda i, j, k: (i, j))
    oshape = jax.ShapeDtypeStruct((m, n), F32)
    return pl.pallas_call(
        functools.partial(_down_body, emit=emit),
        grid=(m // tm, n // tn, kdim // tk),
        in_specs=[pl.BlockSpec((tm, tk), lambda i, j, k: (i, k)), wspec, ospec],
        out_specs=[ospec, wspec] if emit else ospec,
        out_shape=[oshape, jax.ShapeDtypeStruct((kdim, n), BF16)] if emit else oshape,
        scratch_shapes=[pltpu.VMEM((tm, tn), F32)],
        compiler_params=_params("parallel", "parallel", "arbitrary"),
        name="ffn_down",
    )(g, w, res)


def _lru_body(u_ref, gate_ref, h0_ref, cprev_ref, cw_ref, cb_ref, wa_ref, ba_ref, wx_ref, bx_ref,
              lam_ref, og_ref, y_ref, hlast_ref, cstate_ref,
              ext_ref, uc_ref, a_ref, b_ref, hs_ref, hc_ref, *, tt):
    @pl.when(pl.program_id(1) == 0)
    def _():
        hc_ref[...] = h0_ref[0]
        ext_ref[0:HALO, :] = cprev_ref[0]

    u = u_ref[...]
    ext_ref[HALO:HALO + tt, :] = u
    cw = cw_ref[...]
    uc = cb_ref[...]
    for k in range(LRU_CONV - 1):
        lo = HALO - (LRU_CONV - 1) + k
        uc = uc + ext_ref[lo:lo + tt, :] * cw[k:k + 1, :]
    uc_ref[...] = uc + u * cw[LRU_CONV - 1:LRU_CONV, :]

    last = ext_ref[tt:tt + HALO, :]
    cstate_ref[0] = last
    ext_ref[0:HALO, :] = last

    for p in range(N_PAIRS):
        sl = slice(p * LRU_PAIR, (p + 1) * LRU_PAIR)
        ucp = uc_ref[:, sl]
        xb = ucp.astype(BF16)
        r = _sigmoid(jnp.dot(xb, wa_ref[p], preferred_element_type=F32) + ba_ref[:, sl])
        i = _sigmoid(jnp.dot(xb, wx_ref[p], preferred_element_type=F32) + bx_ref[:, sl])
        log_a = (LRU_C * r) * (-_softplus(-lam_ref[:, sl]))
        a = jnp.exp(log_a)
        a_ref[:, sl] = a
        b_ref[:, sl] = jnp.sqrt(-jnp.tanh(log_a) * (a * a + 1.0)) * (i * ucp)

    def step(t, h):
        h = a_ref[pl.ds(t, 1), :] * h + b_ref[pl.ds(t, 1), :]
        hs_ref[pl.ds(t, 1), :] = h
        return h

    h = lax.fori_loop(0, tt, step, hc_ref[...], unroll=8)
    hc_ref[...] = h
    hlast_ref[0] = h

    y = hs_ref[...] * _gelu(gate_ref[...])
    y_ref[...] = _rms(y, og_ref[...]).astype(BF16)


def _lru(proj, h0, cprev, wts, nseq, seqlen, tt):
    m = nseq * seqlen
    nt = seqlen // tt
    row = lambda b, t: (b * nt + t, 0)
    vec = lambda b, t: (0, 0)
    return pl.pallas_call(
        functools.partial(_lru_body, tt=tt),
        grid=(nseq, nt),
        in_specs=[
            pl.BlockSpec((tt, D_LRU), row),
            pl.BlockSpec((tt, D_LRU), lambda b, t: (b * nt + t, 1)),
            pl.BlockSpec((1, 1, D_LRU), lambda b, t: (b, 0, 0)),
            pl.BlockSpec((1, HALO, D_LRU), lambda b, t: (b, 0, 0)),
            pl.BlockSpec((HALO, D_LRU), vec),
            pl.BlockSpec((1, D_LRU), vec),
            pl.BlockSpec((N_PAIRS, LRU_PAIR, LRU_PAIR), lambda b, t: (0, 0, 0)),
            pl.BlockSpec((1, D_LRU), vec),
            pl.BlockSpec((N_PAIRS, LRU_PAIR, LRU_PAIR), lambda b, t: (0, 0, 0)),
            pl.BlockSpec((1, D_LRU), vec),
            pl.BlockSpec((1, D_LRU), vec),
            pl.BlockSpec((1, D_LRU), vec),
        ],
        out_specs=[
            pl.BlockSpec((tt, D_LRU), row),
            pl.BlockSpec((1, 1, D_LRU), lambda b, t: (b, 0, 0)),
            pl.BlockSpec((1, HALO, D_LRU), lambda b, t: (b, 0, 0)),
        ],
        out_shape=[
            jax.ShapeDtypeStruct((m, D_LRU), BF16),
            jax.ShapeDtypeStruct((nseq, 1, D_LRU), F32),
            jax.ShapeDtypeStruct((nseq, HALO, D_LRU), F32),
        ],
        scratch_shapes=[
            pltpu.VMEM((tt + HALO, D_LRU), F32),
            pltpu.VMEM((tt, D_LRU), F32),
            pltpu.VMEM((tt, D_LRU), F32),
            pltpu.VMEM((tt, D_LRU), F32),
            pltpu.VMEM((tt, D_LRU), F32),
            pltpu.VMEM((1, D_LRU), F32),
        ],
        compiler_params=_params("arbitrary", "arbitrary"),
        name="rg_lru",
    )(proj, proj, h0, cprev, wts["lru_conv_w"], wts["lru_conv_b"], wts["lru_w_a"], wts["lru_b_a"],
      wts["lru_w_x"], wts["lru_b_x"], wts["lru_lambda"], wts["lru_out_g"])


def _qkv_body(q_ref, k_ref, v_ref, gq_ref, gk_ref, qb_ref, kf_ref, kb_ref, vf_ref, vb_ref):
    for h in range(N_HEADS):
        sl = slice(h * HEAD_DIM, (h + 1) * HEAD_DIM)
        qb_ref[:, sl] = (_rms(q_ref[:, sl], gq_ref[...]) * Q_SCALE).astype(BF16)
        kn = _rms(k_ref[:, sl], gk_ref[...])
        kf_ref[:, sl] = kn
        kb_ref[:, sl] = kn.astype(BF16)
    v = v_ref[...]
    vf_ref[...] = v
    vb_ref[...] = v.astype(BF16)


def _qkv(proj, gq, gk, tm):
    m = proj.shape[0]
    col0 = 2 * D_LRU // D_SB
    blk = lambda c: pl.BlockSpec((tm, D_SB), lambda i: (i, c))
    vec = pl.BlockSpec((1, HEAD_DIM), lambda i: (0, 0))
    out = pl.BlockSpec((tm, D_SB), lambda i: (i, 0))
    return pl.pallas_call(
        _qkv_body,
        grid=(m // tm,),
        in_specs=[blk(col0), blk(col0 + 1), blk(col0 + 2), vec, vec],
        out_specs=[out] * 5,
        out_shape=[jax.ShapeDtypeStruct((m, D_SB), d) for d in (BF16, F32, BF16, F32, BF16)],
        compiler_params=_params("parallel"),
        name="qkv_norm",
    )(proj, proj, proj, gq, gk)


def _neg_abs(x):
    return lax.bitcast_convert_type(lax.bitcast_convert_type(x, jnp.int32) | jnp.int32(-2 ** 31), F32)


def _sb_step(q, k, v, tri, c_ref, crows, acc_ref, arows, acols, masked):
    z = lax.dot_general(q, k, (((1,), (1,)), ((), ())), preferred_element_type=F32)
    sp = jnp.maximum(z, 0.0) + jnp.log(1.0 + jnp.exp2(_neg_abs(z))) * LOG2E
    if masked:
        mask = (lax.broadcasted_iota(jnp.int32, z.shape, 1) < lax.broadcasted_iota(jnp.int32, z.shape, 0))
        sp = jnp.where(mask, sp, 0.0)
    inner = jnp.dot(sp.astype(BF16), tri, preferred_element_type=F32)
    c = c_ref[crows, :]
    w = jnp.exp2((z - sp) - (inner + c))
    if masked:
        w = jnp.where(mask, w, 0.0)
    acc_ref[arows, acols] += jnp.dot(w.astype(BF16), v, preferred_element_type=F32)
    c_ref[crows, :] = c + jnp.sum(sp, axis=-1, keepdims=True)


NEG_BIG = -1e30
SKIP_LOG2 = 160.0


def _sb_logits(q, k, z_ref, slot):
    z_ref[slot] = lax.dot_general(q, k, (((1,), (1,)), ((), ())), preferred_element_type=F32)


def _sb_scores(z_ref, c_ref, crows, t_ref, spb_ref, slot, mode):
    if mode == "none":
        t_ref[slot] = jnp.full(t_ref.shape[1:], NEG_BIG, F32)
        spb_ref[slot] = jnp.zeros(spb_ref.shape[1:], BF16)
        return
    z = z_ref[slot]
    sp = jnp.maximum(z, 0.0) + jnp.log(1.0 + jnp.exp2(_neg_abs(z))) * LOG2E
    c = c_ref[crows, :]
    t = (z - sp) - c
    if mode == "diag":
        mask = (lax.broadcasted_iota(jnp.int32, z.shape, 1) < lax.broadcasted_iota(jnp.int32, z.shape, 0))
        sp = jnp.where(mask, sp, 0.0)
        t = jnp.where(mask, t, NEG_BIG)
    t_ref[slot] = t
    spb_ref[slot] = sp.astype(BF16)
    c_ref[crows, :] = c + jnp.sum(sp, axis=-1, keepdims=True)


def _sb_weights(tri, t_ref, spb_ref, wb_ref, slot):
    inner = jnp.dot(spb_ref[slot], tri, preferred_element_type=F32)
    wb_ref[slot] = jnp.exp2(t_ref[slot] - inner).astype(BF16)


def _sb_values(v, wb_ref, slot, acc_ref, arows):
    acc_ref[arows, :] += jnp.dot(wb_ref[slot], v, preferred_element_type=F32)


def _attn_prompt_body(q_ref, k_ref, v_ref, tri_ref, o_ref, c_ref, z_ref, t_ref, spb_ref, wb_ref, *, nsub):
    ts = SB_TILE
    base = pl.program_id(1) * nsub
    tri = tri_ref[...]
    c_ref[...] = jnp.zeros_like(c_ref)
    o_ref[...] = jnp.zeros_like(o_ref)
    rows = [pl.ds(s * ts, ts) for s in range(nsub)]
    qs = [q_ref[r, :] for r in rows]
    cols = slice(None)

    @pl.when(base == 0)
    def _():
        for t in reversed(range(nsub)):
            k = k_ref[t * ts:(t + 1) * ts, :]
            v = v_ref[t * ts:(t + 1) * ts, :]
            for s in range(t, nsub):
                _sb_step(qs[s], k, v, tri, c_ref, rows[s], o_ref, rows[s], cols, masked=(s == t))

    @pl.when(base > 0)
    def _():
        n = base + nsub

        def tile(i):
            return pl.ds(pl.multiple_of((n - 1 - i) * ts, ts), ts)

        def modes(i):
            kt = nsub - 1 - i
            return tuple("full" if (kt < 0 or s > kt) else ("diag" if s == kt else "none")
                         for s in range(nsub))

        def iteration(i, p, first=1, last=4, static_i=None):
            if last >= 4 and first <= 4:
                v = v_ref[tile(i - 3), :]
                for s in range(nsub):
                    _sb_values(v, wb_ref.at[1 - p], s, o_ref, rows[s])
            if last >= 3 and first <= 3:
                for s in range(nsub):
                    _sb_weights(tri, t_ref.at[1 - p], spb_ref.at[1 - p], wb_ref.at[p], s)
            if last >= 2 and first <= 2:
                md = modes(static_i - 1) if static_i is not None else ("full",) * nsub
                for s in range(nsub):
                    _sb_scores(z_ref.at[1 - p], c_ref, rows[s], t_ref.at[p], spb_ref.at[p], s, md[s])
            if last >= 1 and first <= 1:
                k = k_ref[tile(i), :]
                for s in range(nsub):
                    _sb_logits(qs[s], k, z_ref.at[p], s)

        nfill = 4
        assert nsub == 2
        for i in range(nfill):
            iteration(i, i & 1, last=min(i + 1, 4), static_i=i)

        def cond(carry):
            j, cmin = carry
            return jnp.logical_and(j < (n - nfill) // 2, cmin < SKIP_LOG2)

        def body(carry):
            j, _ = carry
            i = nfill + 2 * j
            iteration(i, 0)
            iteration(i + 1, 1)
            return j + 1, jnp.min(c_ref[...])

        trips, _ = lax.while_loop(cond, body, (jnp.int32(0), jnp.min(c_ref[...])))
        issued = nfill + 2 * trips
        for d in range(3):
            iteration(issued + d, d & 1, first=d + 2)


def _attn_prompt(qb, kb, vb, tri, nsub):
    t = qb.shape[0]
    tq = nsub * SB_TILE
    return pl.pallas_call(
        functools.partial(_attn_prompt_body, nsub=nsub),
        grid=(N_HEADS, t // tq),
        in_specs=[
            pl.BlockSpec((tq, HEAD_DIM), lambda h, i: (i, h)),
            pl.BlockSpec((t, HEAD_DIM), lambda h, i: (0, h)),
            pl.BlockSpec((t, HEAD_DIM), lambda h, i: (0, h)),
            pl.BlockSpec((SB_TILE, SB_TILE), lambda h, i: (0, 0)),
        ],
        out_specs=pl.BlockSpec((tq, HEAD_DIM), lambda h, i: (i, h)),
        out_shape=jax.ShapeDtypeStruct((t, D_SB), F32),
        scratch_shapes=[pltpu.VMEM((tq, 1), F32),
                        pltpu.VMEM((2, nsub, SB_TILE, SB_TILE), F32),
                        pltpu.VMEM((2, nsub, SB_TILE, SB_TILE), F32),
                        pltpu.VMEM((2, nsub, SB_TILE, SB_TILE), BF16),
                        pltpu.VMEM((2, nsub, SB_TILE, SB_TILE), BF16)],
        compiler_params=_params("parallel", "arbitrary"),
        name="sb_attn_prompt",
    )(qb, kb, vb, tri)


def _attn_sample_body(q_ref, kn_ref, vn_ref, kc_ref, vc_ref, trin_ref, tri_ref, o_ref, c_ref, *, tq, chunk):
    ts = SB_TILE
    tri = tri_ref[...]
    rows = [pl.ds(h * tq, tq) for h in range(N_HEADS)]
    cols = [slice(h * HEAD_DIM, (h + 1) * HEAD_DIM) for h in range(N_HEADS)]
    qs = [q_ref[:, cl] for cl in cols]

    @pl.when(pl.program_id(1) == 0)
    def _():
        c_ref[...] = jnp.zeros_like(c_ref)
        o_ref[...] = jnp.zeros_like(o_ref)
        trin = trin_ref[...]
        for h in range(N_HEADS):
            _sb_step(qs[h], kn_ref[:, cols[h]], vn_ref[:, cols[h]], trin, c_ref, rows[h], o_ref, slice(None), cols[h], True)

    @pl.when(jnp.min(c_ref[...]) < SKIP_LOG2)
    def _():
        for t in reversed(range(chunk // ts)):
            for h in range(N_HEADS):
                sel = pl.ds(t * ts * N_HEADS + h, ts, stride=N_HEADS)
                k = kc_ref[sel, :].astype(BF16)
                v = vc_ref[sel, :].astype(BF16)
                _sb_step(qs[h], k, v, tri, c_ref, rows[h], o_ref, slice(None), cols[h], False)


def _attn_sample(qb, kb, vb, kc, vc, tri_new, tri, nseq, tq, chunk):
    nchunk = kc.shape[1] // (chunk * N_HEADS)
    new = pl.BlockSpec((tq, D_SB), lambda b, c: (b, 0))
    cache = pl.BlockSpec((None, chunk * N_HEADS, HEAD_DIM), lambda b, c: (b, nchunk - 1 - c, 0))
    return pl.pallas_call(
        functools.partial(_attn_sample_body, tq=tq, chunk=chunk),
        grid=(nseq, nchunk),
        in_specs=[new, new, new, cache, cache,
                  pl.BlockSpec((tq, tq), lambda b, c: (0, 0)),
                  pl.BlockSpec((SB_TILE, SB_TILE), lambda b, c: (0, 0))],
        out_specs=new,
        out_shape=jax.ShapeDtypeStruct((nseq * tq, D_SB), F32),
        scratch_shapes=[pltpu.VMEM((N_HEADS * tq, 1), F32)],
        compiler_params=_params("parallel", "arbitrary"),
        name="sb_attn_sample",
    )(qb, kb, vb, kc, vc, tri_new, tri)


def _ffn_conv_act(pre, val, ext_ref, cw_ref, cb_ref, tt, cols=slice(None)):
    ext_ref[HALO:HALO + tt, cols] = pre
    cw = cw_ref[:, cols]
    pc = cb_ref[:, cols]
    for k in range(FFN_CONV - 1):
        lo = HALO - (FFN_CONV - 1) + k
        pc = pc + ext_ref[lo:lo + tt, cols] * cw[k:k + 1, :]
    pc = pc + pre * cw[FFN_CONV - 1:FFN_CONV, :]
    return (_gelu(pc) * val).astype(BF16), ext_ref[tt:tt + HALO, cols]


def _ffn_act_body(val_ref, pre_ref, sprev_ref, cw_ref, cb_ref, g_ref, sout_ref, ext_ref, *, tt):
    @pl.when(pl.program_id(2) == 0)
    def _():
        ext_ref[0:HALO, :] = sprev_ref[0]

    g, last = _ffn_conv_act(pre_ref[...], val_ref[...], ext_ref, cw_ref, cb_ref, tt)
    g_ref[...] = g
    sout_ref[0] = last
    ext_ref[0:HALO, :] = last


def _ffn_act(up, sprev, cw, cb, nseq, seqlen, tt, tc):
    m = nseq * seqlen
    nt = seqlen // tt
    nc = D_FF // tc
    return pl.pallas_call(
        functools.partial(_ffn_act_body, tt=tt),
        grid=(nseq, nc, nt),
        in_specs=[
            pl.BlockSpec((tt, tc), lambda b, c, t: (b * nt + t, c)),
            pl.BlockSpec((tt, tc), lambda b, c, t: (b * nt + t, nc + c)),
            pl.BlockSpec((1, HALO, tc), lambda b, c, t: (b, 0, c)),
            pl.BlockSpec((HALO, tc), lambda b, c, t: (0, c)),
            pl.BlockSpec((1, tc), lambda b, c, t: (0, c)),
        ],
        out_specs=[
            pl.BlockSpec((tt, tc), lambda b, c, t: (b * nt + t, c)),
            pl.BlockSpec((1, HALO, tc), lambda b, c, t: (b, 0, c)),
        ],
        out_shape=[
            jax.ShapeDtypeStruct((m, D_FF), BF16),
            jax.ShapeDtypeStruct((nseq, HALO, D_FF), F32),
        ],
        scratch_shapes=[pltpu.VMEM((tt + HALO, tc), F32)],
        compiler_params=_params("parallel", "parallel", "arbitrary"),
        name="ffn_act",
    )(up, up, sprev, cw, cb)


def _ffn_up_act_body(x_ref, g_ref, wv_ref, wp_ref, sprev_ref, cw_ref, cb_ref, o_ref, last_ref,
                     hn_ref, halo_ref, ext_ref, *, tm):
    i = pl.program_id(0)
    j = pl.program_id(1)

    @pl.when(j == 0)
    def _():
        hn_ref[...] = _rms(x_ref[...], g_ref[...]).astype(BF16)

    @pl.when(i == 0)
    def _():
        ext_ref[0:HALO, :] = sprev_ref[...]

    @pl.when(i > 0)
    def _():
        ext_ref[0:HALO, :] = halo_ref[j]

    hn = hn_ref[...]
    tn = o_ref.shape[1]
    halves = [slice(h * MXU_COLS, (h + 1) * MXU_COLS) for h in range(tn // MXU_COLS)]
    pres = [jnp.dot(hn, wp_ref[:, cs], preferred_element_type=F32) for cs in halves]
    vals = [jnp.dot(hn, wv_ref[:, cs], preferred_element_type=F32) for cs in halves]
    for cs, pre, val in zip(halves, pres, vals):
        g, last = _ffn_conv_act(pre, val, ext_ref, cw_ref, cb_ref, tm, cs)
        o_ref[:, cs] = g
        halo_ref[j, :, cs] = last
        last_ref[0, :, cs] = last


def _ffn_up_act(x, g, w_up, sprev, cw, cb, tm, tn):
    m, k = x.shape
    nj = D_FF // tn
    return pl.pallas_call(
        functools.partial(_ffn_up_act_body, tm=tm),
        grid=(m // tm, nj),
        in_specs=[
            pl.BlockSpec((tm, k), lambda i, j: (i, 0)),
            pl.BlockSpec((1, k), lambda i, j: (0, 0)),
            pl.BlockSpec((k, tn), lambda i, j: (0, j)),
            pl.BlockSpec((k, tn), lambda i, j: (0, nj + j)),
            pl.BlockSpec((HALO, tn), lambda i, j: (0, j)),
            pl.BlockSpec((HALO, tn), lambda i, j: (0, j)),
            pl.BlockSpec((1, tn), lambda i, j: (0, j)),
        ],
        out_specs=[
            pl.BlockSpec((tm, tn), lambda i, j: (i, j)),
            pl.BlockSpec((1, HALO, tn), lambda i, j: (i, 0, j)),
        ],
        out_shape=[
            jax.ShapeDtypeStruct((m, D_FF), BF16),
            jax.ShapeDtypeStruct((m // tm, HALO, D_FF), F32),
        ],
        scratch_shapes=[
            pltpu.VMEM((tm, k), BF16),
            pltpu.VMEM((nj, HALO, tn), F32),
            pltpu.VMEM((tm + HALO, tn), F32),
        ],
        compiler_params=_params("arbitrary", "arbitrary"),
        name="ffn_up_act",
    )(x, g, w_up, w_up, sprev, cw, cb)


def _front_pad(x, rows):
    return jnp.pad(x, ((0, 0), (rows - x.shape[1], 0), (0, 0)))


def _layer(x3, kpast, vpast, h0, cprev, fprev, wts, tri, tm, tt_lru, tt_ffn, attn_nsub=2, fuse_ffn=None,
           emit=False):
    nseq, seqlen, _ = x3.shape
    m = nseq * seqlen
    x = x3.reshape(m, D_MODEL)
    if fuse_ffn is None:
        fuse_ffn = nseq == 1
    assert not (emit and fuse_ffn)
    tn = 512 if emit else 1024
    wb = {}

    proj = _norm_matmul(x, wts["norm_mix_g"], wts["w_in"], tm, tn, "in_proj", emit)
    if emit:
        proj, wb["w_in"] = proj
    y_lru, h_last, cstate = _lru(proj, h0[:, None, :], _front_pad(cprev, HALO), wts, nseq, seqlen, tt_lru)
    qb, kf, kb, vf, vb = _qkv(proj, wts["q_norm_g"], wts["k_norm_g"], tm)
    if kpast is None:
        y_sb = _attn_prompt(qb, kb, vb, tri, attn_nsub)
    else:
        past = kpast.shape[1]
        y_sb = _attn_sample(qb, kb, vb, kpast.reshape(nseq, past * N_HEADS, HEAD_DIM),
                            vpast.reshape(nseq, past * N_HEADS, HEAD_DIM),
                            tri[:seqlen, :seqlen], tri, nseq, seqlen, min(past, 1024))
    if emit:
        x1, wb["w_out_lru"], wb["w_out_sb"] = _outproj(y_lru, y_sb, wts["sb_out_g"], wts["w_out"], wts["w_out"],
                                                        x, tm, tn, emit)
    else:
        x1 = _outproj(y_lru, y_sb, wts["sb_out_g"], wts["w_out_lru"], wts["w_out_sb"], x, tm, tn)

    fprev8 = _front_pad(fprev, HALO)
    if fuse_ffn:
        g, lasts = _ffn_up_act(x1, wts["norm_ffn_g"], wts["w_ffn_up"], fprev8[0], wts["ffn_conv_w"],
                               wts["ffn_conv_b"], tm, 512)
        fstate = lasts[-1:]
    else:
        up = _norm_matmul(x1, wts["norm_ffn_g"], wts["w_ffn_up"], tm, tn, "ffn_up", emit)
        if emit:
            up, wb["w_ffn_up"] = up
        g, fstate = _ffn_act(up, fprev8, wts["ffn_conv_w"], wts["ffn_conv_b"], nseq, seqlen, tt_ffn, 2048)
    out = _down(g, wts["w_ffn_down"], x1, min(m, 1024), 1024, 2048, emit)
    if emit:
        out, wb["w_ffn_down"] = out

    return (out.reshape(nseq, seqlen, D_MODEL),
            kf.reshape(nseq, seqlen, N_HEADS, HEAD_DIM),
            vf.reshape(nseq, seqlen, N_HEADS, HEAD_DIM),
            h_last[:, 0, :],
            cstate[:, HALO - (LRU_CONV - 1):, :],
            fstate[:, HALO - (FFN_CONV - 1):, :],
            wb)


def _pair_blocks(w):
    z = jnp.zeros((N_PAIRS, LRU_BLOCK_DIM, LRU_BLOCK_DIM), w.dtype)
    top = jnp.concatenate([w[0::2], z], axis=2)
    bot = jnp.concatenate([z, w[1::2]], axis=2)
    return jnp.concatenate([top, bot], axis=1).astype(BF16)


def _prep_weights(norm_mix_g, w_in, lru_conv_w, lru_conv_b, lru_w_a, lru_b_a, lru_w_x, lru_b_x,
                  lru_lambda, q_norm_g, k_norm_g, lru_out_g, sb_out_g, w_out, norm_ffn_g,
                  w_ffn_up, ffn_conv_w, ffn_conv_b, w_ffn_down):
    row = lambda v: v.reshape(1, -1)
    pad_rows = lambda w: jnp.pad(w, ((0, HALO - w.shape[0]), (0, 0)))
    return {
        "norm_mix_g": row(norm_mix_g), "w_in": w_in,
        "lru_conv_w": pad_rows(lru_conv_w), "lru_conv_b": row(lru_conv_b),
        "lru_w_a": _pair_blocks(lru_w_a), "lru_b_a": row(lru_b_a),
        "lru_w_x": _pair_blocks(lru_w_x), "lru_b_x": row(lru_b_x),
        "lru_lambda": row(lru_lambda),
        "q_norm_g": row(q_norm_g), "k_norm_g": row(k_norm_g),
        "lru_out_g": row(lru_out_g), "sb_out_g": row(sb_out_g), "w_out": w_out,
        "norm_ffn_g": row(norm_ffn_g), "w_ffn_up": w_ffn_up,
        "ffn_conv_w": pad_rows(ffn_conv_w), "ffn_conv_b": row(ffn_conv_b),
        "w_ffn_down": w_ffn_down,
    }


def kernel(x_prompt, x_sample, cache_sb_k, cache_sb_v, state_lru_h, state_lru_conv, state_ffn_conv, norm_mix_g, w_in, lru_conv_w, lru_conv_b, lru_w_a, lru_b_a, lru_w_x, lru_b_x, lru_lambda, q_norm_g, k_norm_g, lru_out_g, sb_out_g, w_out, norm_ffn_g, w_ffn_up, ffn_conv_w, ffn_conv_b, w_ffn_down):
    depth = w_in.shape[0]
    bp = x_prompt.shape[0]
    tri = jnp.tri(SB_TILE, k=-1, dtype=BF16)
    xp, xs = x_prompt, x_sample
    st_p, st_s = [], []
    for l in range(depth):
        wts = _prep_weights(norm_mix_g[l], w_in[l], lru_conv_w[l], lru_conv_b[l], lru_w_a[l], lru_b_a[l],
                            lru_w_x[l], lru_b_x[l], lru_lambda[l], q_norm_g[l], k_norm_g[l], lru_out_g[l],
                            sb_out_g[l], w_out[l], norm_ffn_g[l], w_ffn_up[l], ffn_conv_w[l],
                            ffn_conv_b[l], w_ffn_down[l])
        xs, *ss, wb = _layer(xs, cache_sb_k[l], cache_sb_v[l], state_lru_h[l], state_lru_conv[l],
                             state_ffn_conv[l], wts, tri, tm=xs.shape[0] * xs.shape[1],
                             tt_lru=xs.shape[1], tt_ffn=xs.shape[1], emit=True)
        st_s.append(ss)
        xp, *sp, _ = _layer(xp, None, None,
                            jnp.zeros((bp, D_LRU), F32),
                            jnp.zeros((bp, LRU_CONV - 1, D_LRU), F32),
                            jnp.zeros((bp, FFN_CONV - 1, D_FF), F32),
                            dict(wts, **wb), tri, tm=512, tt_lru=128, tt_ffn=256)
        st_p.append(sp)
    stack = lambda sts, i: jnp.stack([s[i] for s in sts])
    return (xp, xs) + tuple(stack(st_p, i) for i in range(5)) + tuple(stack(st_s, i) for i in range(5))
```

```python
import functools
import math

import jax
import jax.numpy as jnp
from jax import lax
from jax.experimental import pallas as pl
from jax.experimental.pallas import tpu as pltpu

F32 = jnp.float32
BF16 = jnp.bfloat16

D_MODEL = 4096
HEAD_DIM = 128
N_HEADS = 8
D_SB = N_HEADS * HEAD_DIM
D_LRU = D_MODEL - D_SB
LRU_BLOCKS = 16
LRU_BLOCK_DIM = D_LRU // LRU_BLOCKS
LRU_PAIR = 2 * LRU_BLOCK_DIM
N_PAIRS = LRU_BLOCKS // 2
LRU_CONV = 4
LRU_C = 8.0
D_FF = 3 * D_MODEL
FFN_CONV = 3
EPS = 1e-6
LOG2E = math.log2(math.e)
Q_SCALE = HEAD_DIM ** -0.5 * LOG2E
HALO = 8
SB_TILE = 256
MXU_COLS = 256
ROW_TILE_SINGLE_BUFFER = 1024

VMEM_LIMIT = 60 * 1024 * 1024


def _params(*sem):
    return pltpu.CompilerParams(dimension_semantics=sem, vmem_limit_bytes=VMEM_LIMIT)


def _rms(x, g):
    ms = jnp.mean(x * x, axis=-1, keepdims=True)
    return (x * lax.rsqrt(ms + EPS)) * g


def _rms_rows(x_ref, g_ref, o_ref, chunk=256):
    g = g_ref[...]
    for r in range(0, x_ref.shape[0], chunk):
        rows = slice(r, min(r + chunk, x_ref.shape[0]))
        o_ref[rows, :] = _rms(x_ref[rows, :], g).astype(BF16)


def _gelu(x):
    return x * (0.5 * (1.0 + jnp.tanh(0.7978845608028654 * (x + 0.044715 * (x * x * x)))))


def _sigmoid(x):
    return 1.0 / (1.0 + jnp.exp(-x))


def _softplus(x):
    return jnp.maximum(x, 0.0) + jnp.log1p(jnp.exp(-jnp.abs(x)))


def _row_tile_spec(tm, k):
    mode = {"pipeline_mode": pl.Buffered(1)} if tm >= ROW_TILE_SINGLE_BUFFER else {}
    return pl.BlockSpec((tm, k), lambda i, j: (i, 0), **mode)


def _weight_block(w_ref, wb_ref):
    w = w_ref[...]
    if wb_ref is not None:
        w = w.astype(BF16)
        wb_ref[...] = w
    return w


def _norm_matmul_body(x_ref, g_ref, w_ref, o_ref, *rest, emit):
    wb_ref, hn_ref = rest if emit else (None,) + rest

    @pl.when(pl.program_id(1) == 0)
    def _():
        _rms_rows(x_ref, g_ref, hn_ref)

    o_ref[...] = jnp.dot(hn_ref[...], _weight_block(w_ref, wb_ref), preferred_element_type=F32)


def _norm_matmul(x, g, w, tm, tn, name, emit=False):
    m, k = x.shape
    n = w.shape[1]
    assert not emit or m == tm
    wspec = pl.BlockSpec((k, tn), lambda i, j: (0, j))
    ospec = pl.BlockSpec((tm, tn), lambda i, j: (i, j))
    oshape = jax.ShapeDtypeStruct((m, n), F32)
    return pl.pallas_call(
        functools.partial(_norm_matmul_body, emit=emit),
        grid=(m // tm, n // tn),
        in_specs=[
            _row_tile_spec(tm, k),
            pl.BlockSpec((1, k), lambda i, j: (0, 0)),
            wspec,
        ],
        out_specs=[ospec, wspec] if emit else ospec,
        out_shape=[oshape, jax.ShapeDtypeStruct((k, n), BF16)] if emit else oshape,
        scratch_shapes=[pltpu.VMEM((tm, k), BF16)],
        compiler_params=_params("parallel", "arbitrary"),
        name=name,
    )(x, g, w)


def _outproj_body(a_ref, y_ref, gsb_ref, w1_ref, w2_ref, res_ref, o_ref, *rest, emit):
    w1b_ref, w2b_ref, yn_ref = rest if emit else (None, None) + rest

    @pl.when(pl.program_id(1) == 0)
    def _():
        yn_ref[...] = _rms(y_ref[...], gsb_ref[...]).astype(BF16)

    acc = jnp.dot(a_ref[...], _weight_block(w1_ref, w1b_ref), preferred_element_type=F32)
    acc = acc + jnp.dot(yn_ref[...], _weight_block(w2_ref, w2b_ref), preferred_element_type=F32)
    o_ref[...] = res_ref[...] + acc


def _outproj(a, y_sb, g_sb, w1, w2, res, tm, tn, emit=False):
    m = a.shape[0]
    n = w1.shape[1]
    assert not emit or m == tm
    w2_row_block = (w2.shape[0] - D_SB) // D_SB
    ospec = pl.BlockSpec((tm, tn), lambda i, j: (i, j))
    oshape = jax.ShapeDtypeStruct((m, n), F32)
    wbspecs = [pl.BlockSpec((D_LRU, tn), lambda i, j: (0, j)), pl.BlockSpec((D_SB, tn), lambda i, j: (0, j))]
    wbshapes = [jax.ShapeDtypeStruct((D_LRU, n), BF16), jax.ShapeDtypeStruct((D_SB, n), BF16)]
    return pl.pallas_call(
        functools.partial(_outproj_body, emit=emit),
        grid=(m // tm, n // tn),
        in_specs=[
            pl.BlockSpec((tm, D_LRU), lambda i, j: (i, 0)),
            pl.BlockSpec((tm, D_SB), lambda i, j: (i, 0)),
            pl.BlockSpec((1, D_SB), lambda i, j: (0, 0)),
            pl.BlockSpec((D_LRU, tn), lambda i, j: (0, j)),
            pl.BlockSpec((D_SB, tn), lambda i, j: (w2_row_block, j)),
            ospec,
        ],
        out_specs=[ospec] + wbspecs if emit else ospec,
        out_shape=[oshape] + wbshapes if emit else oshape,
        scratch_shapes=[pltpu.VMEM((tm, D_SB), BF16)],
        compiler_params=_params("parallel", "arbitrary"),
        name="out_proj",
    )(a, y_sb, g_sb, w1, w2, res)


def _down_body(g_ref, w_ref, res_ref, o_ref, *rest, emit):
    wb_ref, acc_ref = rest if emit else (None,) + rest
    k = pl.program_id(2)

    @pl.when(k == 0)
    def _():
        acc_ref[...] = jnp.zeros_like(acc_ref)

    acc_ref[...] += jnp.dot(g_ref[...], _weight_block(w_ref, wb_ref), preferred_element_type=F32)

    @pl.when(k == pl.num_programs(2) - 1)
    def _():
        o_ref[...] = res_ref[...] + acc_ref[...]


def _down(g, w, res, tm, tn, tk, emit=False):
    m, kdim = g.shape
    n = w.shape[1]
    assert not emit or m == tm
    wspec = pl.BlockSpec((tk, tn), lambda i, j, k: (k, j))
    ospec = pl.BlockSpec((tm, tn), lambda i, j, k: (i, j))
    oshape = jax.ShapeDtypeStruct((m, n), F32)
    return pl.pallas_call(
        functools.partial(_down_body, emit=emit),
        grid=(m // tm, n // tn, kdim // tk),
        in_specs=[pl.BlockSpec((tm, tk), lambda i, j, k: (i, k)), wspec, ospec],
        out_specs=[ospec, wspec] if emit else ospec,
        out_shape=[oshape, jax.ShapeDtypeStruct((kdim, n), BF16)] if emit else oshape,
        scratch_shapes=[pltpu.VMEM((tm, tn), F32)],
        compiler_params=_params("parallel", "parallel", "arbitrary"),
        name="ffn_down",
    )(g, w, res)


def _lru_body(u_ref, gate_ref, h0_ref, cprev_ref, cw_ref, cb_ref, wa_ref, ba_ref, wx_ref, bx_ref,
              lam_ref, og_ref, y_ref, hlast_ref, cstate_ref,
              ext_ref, uc_ref, a_ref, b_ref, hs_ref, hc_ref, *, tt):
    @pl.when(pl.program_id(1) == 0)
    def _():
        hc_ref[...] = h0_ref[0]
        ext_ref[0:HALO, :] = cprev_ref[0]

    u = u_ref[...]
    ext_ref[HALO:HALO + tt, :] = u
    cw = cw_ref[...]
    uc = cb_ref[...]
    for k in range(LRU_CONV - 1):
        lo = HALO - (LRU_CONV - 1) + k
        uc = uc + ext_ref[lo:lo + tt, :] * cw[k:k + 1, :]
    uc_ref[...] = uc + u * cw[LRU_CONV - 1:LRU_CONV, :]

    last = ext_ref[tt:tt + HALO, :]
    cstate_ref[0] = last
    ext_ref[0:HALO, :] = last

    for p in range(N_PAIRS):
        sl = slice(p * LRU_PAIR, (p + 1) * LRU_PAIR)
        ucp = uc_ref[:, sl]
        xb = ucp.astype(BF16)
        r = _sigmoid(jnp.dot(xb, wa_ref[p], preferred_element_type=F32) + ba_ref[:, sl])
        i = _sigmoid(jnp.dot(xb, wx_ref[p], preferred_element_type=F32) + bx_ref[:, sl])
        log_a = (LRU_C * r) * (-_softplus(-lam_ref[:, sl]))
        a = jnp.exp(log_a)
        a_ref[:, sl] = a
        b_ref[:, sl] = jnp.sqrt(-jnp.tanh(log_a) * (a * a + 1.0)) * (i * ucp)

    def step(t, h):
        h = a_ref[pl.ds(t, 1), :] * h + b_ref[pl.ds(t, 1), :]
        hs_ref[pl.ds(t, 1), :] = h
        return h

    h = lax.fori_loop(0, tt, step, hc_ref[...], unroll=8)
    hc_ref[...] = h
    hlast_ref[0] = h

    y = hs_ref[...] * _gelu(gate_ref[...])
    y_ref[...] = _rms(y, og_ref[...]).astype(BF16)


def _lru(proj, h0, cprev, wts, nseq, seqlen, tt):
    m = nseq * seqlen
    nt = seqlen // tt
    row = lambda b, t: (b * nt + t, 0)
    vec = lambda b, t: (0, 0)
    return pl.pallas_call(
        functools.partial(_lru_body, tt=tt),
        grid=(nseq, nt),
        in_specs=[
            pl.BlockSpec((tt, D_LRU), row),
            pl.BlockSpec((tt, D_LRU), lambda b, t: (b * nt + t, 1)),
            pl.BlockSpec((1, 1, D_LRU), lambda b, t: (b, 0, 0)),
            pl.BlockSpec((1, HALO, D_LRU), lambda b, t: (b, 0, 0)),
            pl.BlockSpec((HALO, D_LRU), vec),
            pl.BlockSpec((1, D_LRU), vec),
            pl.BlockSpec((N_PAIRS, LRU_PAIR, LRU_PAIR), lambda b, t: (0, 0, 0)),
            pl.BlockSpec((1, D_LRU), vec),
            pl.BlockSpec((N_PAIRS, LRU_PAIR, LRU_PAIR), lambda b, t: (0, 0, 0)),
            pl.BlockSpec((1, D_LRU), vec),
            pl.BlockSpec((1, D_LRU), vec),
            pl.BlockSpec((1, D_LRU), vec),
        ],
        out_specs=[
            pl.BlockSpec((tt, D_LRU), row),
            pl.BlockSpec((1, 1, D_LRU), lambda b, t: (b, 0, 0)),
            pl.BlockSpec((1, HALO, D_LRU), lambda b, t: (b, 0, 0)),
        ],
        out_shape=[
            jax.ShapeDtypeStruct((m, D_LRU), BF16),
            jax.ShapeDtypeStruct((nseq, 1, D_LRU), F32),
            jax.ShapeDtypeStruct((nseq, HALO, D_LRU), F32),
        ],
        scratch_shapes=[
            pltpu.VMEM((tt + HALO, D_LRU), F32),
            pltpu.VMEM((tt, D_LRU), F32),
            pltpu.VMEM((tt, D_LRU), F32),
            pltpu.VMEM((tt, D_LRU), F32),
            pltpu.VMEM((tt, D_LRU), F32),
            pltpu.VMEM((1, D_LRU), F32),
        ],
        compiler_params=_params("arbitrary", "arbitrary"),
        name="rg_lru",
    )(proj, proj, h0, cprev, wts["lru_conv_w"], wts["lru_conv_b"], wts["lru_w_a"], wts["lru_b_a"],
      wts["lru_w_x"], wts["lru_b_x"], wts["lru_lambda"], wts["lru_out_g"])


def _qkv_body(q_ref, k_ref, v_ref, gq_ref, gk_ref, qb_ref, kf_ref, kb_ref, vf_ref, vb_ref):
    for h in range(N_HEADS):
        sl = slice(h * HEAD_DIM, (h + 1) * HEAD_DIM)
        qb_ref[:, sl] = (_rms(q_ref[:, sl], gq_ref[...]) * Q_SCALE).astype(BF16)
        kn = _rms(k_ref[:, sl], gk_ref[...])
        kf_ref[:, sl] = kn
        kb_ref[:, sl] = kn.astype(BF16)
    v = v_ref[...]
    vf_ref[...] = v
    vb_ref[...] = v.astype(BF16)


def _qkv(proj, gq, gk, tm):
    m = proj.shape[0]
    col0 = 2 * D_LRU // D_SB
    blk = lambda c: pl.BlockSpec((tm, D_SB), lambda i: (i, c))
    vec = pl.BlockSpec((1, HEAD_DIM), lambda i: (0, 0))
    out = pl.BlockSpec((tm, D_SB), lambda i: (i, 0))
    return pl.pallas_call(
        _qkv_body,
        grid=(m // tm,),
        in_specs=[blk(col0), blk(col0 + 1), blk(col0 + 2), vec, vec],
        out_specs=[out] * 5,
        out_shape=[jax.ShapeDtypeStruct((m, D_SB), d) for d in (BF16, F32, BF16, F32, BF16)],
        compiler_params=_params("parallel"),
        name="qkv_norm",
    )(proj, proj, proj, gq, gk)


def _neg_abs(x):
    return lax.bitcast_convert_type(lax.bitcast_convert_type(x, jnp.int32) | jnp.int32(-2 ** 31), F32)


def _sb_step(q, k, v, tri, c_ref, crows, acc_ref, arows, acols, masked):
    z = lax.dot_general(q, k, (((1,), (1,)), ((), ())), preferred_element_type=F32)
    sp = jnp.maximum(z, 0.0) + jnp.log(1.0 + jnp.exp2(_neg_abs(z))) * LOG2E
    if masked:
        mask = (lax.broadcasted_iota(jnp.int32, z.shape, 1) < lax.broadcasted_iota(jnp.int32, z.shape, 0))
        sp = jnp.where(mask, sp, 0.0)
    inner = jnp.dot(sp.astype(BF16), tri, preferred_element_type=F32)
    c = c_ref[crows, :]
    w = jnp.exp2((z - sp) - (inner + c))
    if masked:
        w = jnp.where(mask, w, 0.0)
    acc_ref[arows, acols] += jnp.dot(w.astype(BF16), v, preferred_element_type=F32)
    c_ref[crows, :] = c + jnp.sum(sp, axis=-1, keepdims=True)


NEG_BIG = -1e30
SKIP_LOG2 = 160.0


def _sb_logits(q, k, z_ref, slot):
    z_ref[slot] = lax.dot_general(q, k, (((1,), (1,)), ((), ())), preferred_element_type=F32)


def _sb_scores(z_ref, c_ref, crows, t_ref, spb_ref, slot, mode):
    if mode == "none":
        t_ref[slot] = jnp.full(t_ref.shape[1:], NEG_BIG, F32)
        spb_ref[slot] = jnp.zeros(spb_ref.shape[1:], BF16)
        return
    z = z_ref[slot]
    sp = jnp.maximum(z, 0.0) + jnp.log(1.0 + jnp.exp2(_neg_abs(z))) * LOG2E
    c = c_ref[crows, :]
    t = (z - sp) - c
    if mode == "diag":
        mask = (lax.broadcasted_iota(jnp.int32, z.shape, 1) < lax.broadcasted_iota(jnp.int32, z.shape, 0))
        sp = jnp.where(mask, sp, 0.0)
        t = jnp.where(mask, t, NEG_BIG)
    t_ref[slot] = t
    spb_ref[slot] = sp.astype(BF16)
    c_ref[crows, :] = c + jnp.sum(sp, axis=-1, keepdims=True)


def _sb_weights(tri, t_ref, spb_ref, wb_ref, slot):
    inner = jnp.dot(spb_ref[slot], tri, preferred_element_type=F32)
    wb_ref[slot] = jnp.exp2(t_ref[slot] - inner).astype(BF16)


def _sb_values(v, wb_ref, slot, acc_ref, arows):
    acc_ref[arows, :] += jnp.dot(wb_ref[slot], v, preferred_element_type=F32)


def _attn_prompt_body(q_ref, k_ref, v_ref, tri_ref, o_ref, c_ref, z_ref, t_ref, spb_ref, wb_ref, *, nsub):
    ts = SB_TILE
    base = pl.program_id(1) * nsub
    tri = tri_ref[...]
    c_ref[...] = jnp.zeros_like(c_ref)
    o_ref[...] = jnp.zeros_like(o_ref)
    rows = [pl.ds(s * ts, ts) for s in range(nsub)]
    qs = [q_ref[r, :] for r in rows]
    cols = slice(None)

    @pl.when(base == 0)
    def _():
        for t in reversed(range(nsub)):
            k = k_ref[t * ts:(t + 1) * ts, :]
            v = v_ref[t * ts:(t + 1) * ts, :]
            for s in range(t, nsub):
                _sb_step(qs[s], k, v, tri, c_ref, rows[s], o_ref, rows[s], cols, masked=(s == t))

    @pl.when(base > 0)
    def _():
        n = base + nsub

        def tile(i):
            return pl.ds(pl.multiple_of((n - 1 - i) * ts, ts), ts)

        def modes(i):
            kt = nsub - 1 - i
            return tuple("full" if (kt < 0 or s > kt) else ("diag" if s == kt else "none")
                         for s in range(nsub))

        def iteration(i, p, first=1, last=4, static_i=None):
            if last >= 4 and first <= 4:
                v = v_ref[tile(i - 3), :]
                for s in range(nsub):
                    _sb_values(v, wb_ref.at[1 - p], s, o_ref, rows[s])
            if last >= 3 and first <= 3:
                for s in range(nsub):
                    _sb_weights(tri, t_ref.at[1 - p], spb_ref.at[1 - p], wb_ref.at[p], s)
            if last >= 2 and first <= 2:
                md = modes(static_i - 1) if static_i is not None else ("full",) * nsub
                for s in range(nsub):
                    _sb_scores(z_ref.at[1 - p], c_ref, rows[s], t_ref.at[p], spb_ref.at[p], s, md[s])
            if last >= 1 and first <= 1:
                k = k_ref[tile(i), :]
                for s in range(nsub):
                    _sb_logits(qs[s], k, z_ref.at[p], s)

        nfill = 4
        assert nsub == 2
        for i in range(nfill):
            iteration(i, i & 1, last=min(i + 1, 4), static_i=i)

        def cond(carry):
            j, cmin = carry
            return jnp.logical_and(j < (n - nfill) // 2, cmin < SKIP_LOG2)

        def body(carry):
            j, _ = carry
            i = nfill + 2 * j
            iteration(i, 0)
            iteration(i + 1, 1)
            return j + 1, jnp.min(c_ref[...])

        trips, _ = lax.while_loop(cond, body, (jnp.int32(0), jnp.min(c_ref[...])))
        issued = nfill + 2 * trips
        for d in range(3):
            iteration(issued + d, d & 1, first=d + 2)


def _attn_prompt(qb, kb, vb, tri, nsub):
    t = qb.shape[0]
    tq = nsub * SB_TILE
    return pl.pallas_call(
        functools.partial(_attn_prompt_body, nsub=nsub),
        grid=(N_HEADS, t // tq),
        in_specs=[
            pl.BlockSpec((tq, HEAD_DIM), lambda h, i: (i, h)),
            pl.BlockSpec((t, HEAD_DIM), lambda h, i: (0, h)),
            pl.BlockSpec((t, HEAD_DIM), lambda h, i: (0, h)),
            pl.BlockSpec((SB_TILE, SB_TILE), lambda h, i: (0, 0)),
        ],
        out_specs=pl.BlockSpec((tq, HEAD_DIM), lambda h, i: (i, h)),
        out_shape=jax.ShapeDtypeStruct((t, D_SB), F32),
        scratch_shapes=[pltpu.VMEM((tq, 1), F32),
                        pltpu.VMEM((2, nsub, SB_TILE, SB_TILE), F32),
                        pltpu.VMEM((2, nsub, SB_TILE, SB_TILE), F32),
                        pltpu.VMEM((2, nsub, SB_TILE, SB_TILE), BF16),
                        pltpu.VMEM((2, nsub, SB_TILE, SB_TILE), BF16)],
        compiler_params=_params("parallel", "arbitrary"),
        name="sb_attn_prompt",
    )(qb, kb, vb, tri)


def _attn_sample_body(q_ref, kn_ref, vn_ref, kc_ref, vc_ref, trin_ref, tri_ref, o_ref, c_ref, *, tq, chunk):
    ts = SB_TILE
    tri = tri_ref[...]
    rows = [pl.ds(h * tq, tq) for h in range(N_HEADS)]
    cols = [slice(h * HEAD_DIM, (h + 1) * HEAD_DIM) for h in range(N_HEADS)]
    qs = [q_ref[:, cl] for cl in cols]

    @pl.when(pl.program_id(1) == 0)
    def _():
        c_ref[...] = jnp.zeros_like(c_ref)
        o_ref[...] = jnp.zeros_like(o_ref)
        trin = trin_ref[...]
        for h in range(N_HEADS):
            _sb_step(qs[h], kn_ref[:, cols[h]], vn_ref[:, cols[h]], trin, c_ref, rows[h], o_ref, slice(None), cols[h], True)

    @pl.when(jnp.min(c_ref[...]) < SKIP_LOG2)
    def _():
        for t in reversed(range(chunk // ts)):
            for h in range(N_HEADS):
                sel = pl.ds(t * ts * N_HEADS + h, ts, stride=N_HEADS)
                k = kc_ref[sel, :].astype(BF16)
                v = vc_ref[sel, :].astype(BF16)
                _sb_step(qs[h], k, v, tri, c_ref, rows[h], o_ref, slice(None), cols[h], False)


def _attn_sample(qb, kb, vb, kc, vc, tri_new, tri, nseq, tq, chunk):
    nchunk = kc.shape[1] // (chunk * N_HEADS)
    new = pl.BlockSpec((tq, D_SB), lambda b, c: (b, 0))
    cache = pl.BlockSpec((None, chunk * N_HEADS, HEAD_DIM), lambda b, c: (b, nchunk - 1 - c, 0))
    return pl.pallas_call(
        functools.partial(_attn_sample_body, tq=tq, chunk=chunk),
        grid=(nseq, nchunk),
        in_specs=[new, new, new, cache, cache,
                  pl.BlockSpec((tq, tq), lambda b, c: (0, 0)),
                  pl.BlockSpec((SB_TILE, SB_TILE), lambda b, c: (0, 0))],
        out_specs=new,
        out_shape=jax.ShapeDtypeStruct((nseq * tq, D_SB), F32),
        scratch_shapes=[pltpu.VMEM((N_HEADS * tq, 1), F32)],
        compiler_params=_params("parallel", "arbitrary"),
        name="sb_attn_sample",
    )(qb, kb, vb, kc, vc, tri_new, tri)


def _ffn_conv_act(pre, val, ext_ref, cw_ref, cb_ref, tt, cols=slice(None)):
    ext_ref[HALO:HALO + tt, cols] = pre
    cw = cw_ref[:, cols]
    pc = cb_ref[:, cols]
    for k in range(FFN_CONV - 1):
        lo = HALO - (FFN_CONV - 1) + k
        pc = pc + ext_ref[lo:lo + tt, cols] * cw[k:k + 1, :]
    pc = pc + pre * cw[FFN_CONV - 1:FFN_CONV, :]
    return (_gelu(pc) * val).astype(BF16), ext_ref[tt:tt + HALO, cols]


def _ffn_act_body(val_ref, pre_ref, sprev_ref, cw_ref, cb_ref, g_ref, sout_ref, ext_ref, *, tt):
    @pl.when(pl.program_id(2) == 0)
    def _():
        ext_ref[0:HALO, :] = sprev_ref[0]

    g, last = _ffn_conv_act(pre_ref[...], val_ref[...], ext_ref, cw_ref, cb_ref, tt)
    g_ref[...] = g
    sout_ref[0] = last
    ext_ref[0:HALO, :] = last


def _ffn_act(up, sprev, cw, cb, nseq, seqlen, tt, tc):
    m = nseq * seqlen
    nt = seqlen // tt
    nc = D_FF // tc
    return pl.pallas_call(
        functools.partial(_ffn_act_body, tt=tt),
        grid=(nseq, nc, nt),
        in_specs=[
            pl.BlockSpec((tt, tc), lambda b, c, t: (b * nt + t, c)),
            pl.BlockSpec((tt, tc), lambda b, c, t: (b * nt + t, nc + c)),
            pl.BlockSpec((1, HALO, tc), lambda b, c, t: (b, 0, c)),
            pl.BlockSpec((HALO, tc), lambda b, c, t: (0, c)),
            pl.BlockSpec((1, tc), lambda b, c, t: (0, c)),
        ],
        out_specs=[
            pl.BlockSpec((tt, tc), lambda b, c, t: (b * nt + t, c)),
            pl.BlockSpec((1, HALO, tc), lambda b, c, t: (b, 0, c)),
        ],
        out_shape=[
            jax.ShapeDtypeStruct((m, D_FF), BF16),
            jax.ShapeDtypeStruct((nseq, HALO, D_FF), F32),
        ],
        scratch_shapes=[pltpu.VMEM((tt + HALO, tc), F32)],
        compiler_params=_params("parallel", "parallel", "arbitrary"),
        name="ffn_act",
    )(up, up, sprev, cw, cb)


def _ffn_up_act_body(x_ref, g_ref, wv_ref, wp_ref, sprev_ref, cw_ref, cb_ref, o_ref, last_ref,
                     hn_ref, halo_ref, ext_ref, *, tm):
    i = pl.program_id(0)
    j = pl.program_id(1)

    @pl.when(j == 0)
    def _():
        _rms_rows(x_ref, g_ref, hn_ref)

    @pl.when(i == 0)
    def _():
        ext_ref[0:HALO, :] = sprev_ref[...]

    @pl.when(i > 0)
    def _():
        ext_ref[0:HALO, :] = halo_ref[j]

    hn = hn_ref[...]
    tn = o_ref.shape[1]
    halves = [slice(h * MXU_COLS, (h + 1) * MXU_COLS) for h in range(tn // MXU_COLS)]
    pres = [jnp.dot(hn, wp_ref[:, cs], preferred_element_type=F32) for cs in halves]
    vals = [jnp.dot(hn, wv_ref[:, cs], preferred_element_type=F32) for cs in halves]
    for cs, pre, val in zip(halves, pres, vals):
        g, last = _ffn_conv_act(pre, val, ext_ref, cw_ref, cb_ref, tm, cs)
        o_ref[:, cs] = g
        halo_ref[j, :, cs] = last
        last_ref[0, :, cs] = last


def _ffn_up_act(x, g, w_up, sprev, cw, cb, tm, tn):
    m, k = x.shape
    nj = D_FF // tn
    return pl.pallas_call(
        functools.partial(_ffn_up_act_body, tm=tm),
        grid=(m // tm, nj),
        in_specs=[
            _row_tile_spec(tm, k),
            pl.BlockSpec((1, k), lambda i, j: (0, 0)),
            pl.BlockSpec((k, tn), lambda i, j: (0, j)),
            pl.BlockSpec((k, tn), lambda i, j: (0, nj + j)),
            pl.BlockSpec((HALO, tn), lambda i, j: (0, j)),
            pl.BlockSpec((HALO, tn), lambda i, j: (0, j)),
            pl.BlockSpec((1, tn), lambda i, j: (0, j)),
        ],
        out_specs=[
            pl.BlockSpec((tm, tn), lambda i, j: (i, j)),
            pl.BlockSpec((1, HALO, tn), lambda i, j: (i, 0, j)),
        ],
        out_shape=[
            jax.ShapeDtypeStruct((m, D_FF), BF16),
            jax.ShapeDtypeStruct((m // tm, HALO, D_FF), F32),
        ],
        scratch_shapes=[
            pltpu.VMEM((tm, k), BF16),
            pltpu.VMEM((nj, HALO, tn), F32),
            pltpu.VMEM((tm + HALO, tn), F32),
        ],
        compiler_params=_params("arbitrary", "arbitrary"),
        name="ffn_up_act",
    )(x, g, w_up, w_up, sprev, cw, cb)


def _front_pad(x, rows):
    return jnp.pad(x, ((0, 0), (rows - x.shape[1], 0), (0, 0)))


def _layer(x3, kpast, vpast, h0, cprev, fprev, wts, tri, tm, tt_lru, tt_ffn, attn_nsub=2, fuse_ffn=None,
           emit=False):
    nseq, seqlen, _ = x3.shape
    m = nseq * seqlen
    x = x3.reshape(m, D_MODEL)
    if fuse_ffn is None:
        fuse_ffn = nseq == 1
    assert not (emit and fuse_ffn)
    tn = 512 if emit else 1024
    tm_wide = ROW_TILE_SINGLE_BUFFER if m % ROW_TILE_SINGLE_BUFFER == 0 else tm
    wb = {}

    proj = _norm_matmul(x, wts["norm_mix_g"], wts["w_in"], tm_wide, 512, "in_proj", emit)
    if emit:
        proj, wb["w_in"] = proj
    y_lru, h_last, cstate = _lru(proj, h0[:, None, :], _front_pad(cprev, HALO), wts, nseq, seqlen, tt_lru)
    qb, kf, kb, vf, vb = _qkv(proj, wts["q_norm_g"], wts["k_norm_g"], tm)
    if kpast is None:
        y_sb = _attn_prompt(qb, kb, vb, tri, attn_nsub)
    else:
        past = kpast.shape[1]
        y_sb = _attn_sample(qb, kb, vb, kpast.reshape(nseq, past * N_HEADS, HEAD_DIM),
                            vpast.reshape(nseq, past * N_HEADS, HEAD_DIM),
                            tri[:seqlen, :seqlen], tri, nseq, seqlen, min(past, 1024))
    if emit:
        x1, wb["w_out_lru"], wb["w_out_sb"] = _outproj(y_lru, y_sb, wts["sb_out_g"], wts["w_out"], wts["w_out"],
                                                        x, tm, tn, emit)
    else:
        x1 = _outproj(y_lru, y_sb, wts["sb_out_g"], wts["w_out_lru"], wts["w_out_sb"], x, tm, tn)

    fprev8 = _front_pad(fprev, HALO)
    if fuse_ffn:
        g, lasts = _ffn_up_act(x1, wts["norm_ffn_g"], wts["w_ffn_up"], fprev8[0], wts["ffn_conv_w"],
                               wts["ffn_conv_b"], tm_wide, 512)
        fstate = lasts[-1:]
    else:
        up = _norm_matmul(x1, wts["norm_ffn_g"], wts["w_ffn_up"], tm, tn, "ffn_up", emit)
        if emit:
            up, wb["w_ffn_up"] = up
        g, fstate = _ffn_act(up, fprev8, wts["ffn_conv_w"], wts["ffn_conv_b"], nseq, seqlen, tt_ffn, 2048)
    out = _down(g, wts["w_ffn_down"], x1, min(m, 1024), 1024, 2048, emit)
    if emit:
        out, wb["w_ffn_down"] = out

    return (out.reshape(nseq, seqlen, D_MODEL),
            kf.reshape(nseq, seqlen, N_HEADS, HEAD_DIM),
            vf.reshape(nseq, seqlen, N_HEADS, HEAD_DIM),
            h_last[:, 0, :],
            cstate[:, HALO - (LRU_CONV - 1):, :],
            fstate[:, HALO - (FFN_CONV - 1):, :],
            wb)


def _pair_blocks(w):
    z = jnp.zeros((N_PAIRS, LRU_BLOCK_DIM, LRU_BLOCK_DIM), w.dtype)
    top = jnp.concatenate([w[0::2], z], axis=2)
    bot = jnp.concatenate([z, w[1::2]], axis=2)
    return jnp.concatenate([top, bot], axis=1).astype(BF16)


def _prep_weights(norm_mix_g, w_in, lru_conv_w, lru_conv_b, lru_w_a, lru_b_a, lru_w_x, lru_b_x,
                  lru_lambda, q_norm_g, k_norm_g, lru_out_g, sb_out_g, w_out, norm_ffn_g,
                  w_ffn_up, ffn_conv_w, ffn_conv_b, w_ffn_down):
    row = lambda v: v.reshape(1, -1)
    pad_rows = lambda w: jnp.pad(w, ((0, HALO - w.shape[0]), (0, 0)))
    return {
        "norm_mix_g": row(norm_mix_g), "w_in": w_in,
        "lru_conv_w": pad_rows(lru_conv_w), "lru_conv_b": row(lru_conv_b),
        "lru_w_a": _pair_blocks(lru_w_a), "lru_b_a": row(lru_b_a),
        "lru_w_x": _pair_blocks(lru_w_x), "lru_b_x": row(lru_b_x),
        "lru_lambda": row(lru_lambda),
        "q_norm_g": row(q_norm_g), "k_norm_g": row(k_norm_g),
        "lru_out_g": row(lru_out_g), "sb_out_g": row(sb_out_g), "w_out": w_out,
        "norm_ffn_g": row(norm_ffn_g), "w_ffn_up": w_ffn_up,
        "ffn_conv_w": pad_rows(ffn_conv_w), "ffn_conv_b": row(ffn_conv_b),
        "w_ffn_down": w_ffn_down,
    }


def kernel(x_prompt, x_sample, cache_sb_k, cache_sb_v, state_lru_h, state_lru_conv, state_ffn_conv, norm_mix_g, w_in, lru_conv_w, lru_conv_b, lru_w_a, lru_b_a, lru_w_x, lru_b_x, lru_lambda, q_norm_g, k_norm_g, lru_out_g, sb_out_g, w_out, norm_ffn_g, w_ffn_up, ffn_conv_w, ffn_conv_b, w_ffn_down):
    depth = w_in.shape[0]
    bp = x_prompt.shape[0]
    tri = jnp.tri(SB_TILE, k=-1, dtype=BF16)
    xp, xs = x_prompt, x_sample
    st_p, st_s = [], []
    for l in range(depth):
        wts = _prep_weights(norm_mix_g[l], w_in[l], lru_conv_w[l], lru_conv_b[l], lru_w_a[l], lru_b_a[l],
                            lru_w_x[l], lru_b_x[l], lru_lambda[l], q_norm_g[l], k_norm_g[l], lru_out_g[l],
                            sb_out_g[l], w_out[l], norm_ffn_g[l], w_ffn_up[l], ffn_conv_w[l],
                            ffn_conv_b[l], w_ffn_down[l])
        xs, *ss, wb = _layer(xs, cache_sb_k[l], cache_sb_v[l], state_lru_h[l], state_lru_conv[l],
                             state_ffn_conv[l], wts, tri, tm=xs.shape[0] * xs.shape[1],
                             tt_lru=xs.shape[1], tt_ffn=xs.shape[1], emit=True)
        st_s.append(ss)
        xp, *sp, _ = _layer(xp, None, None,
                            jnp.zeros((bp, D_LRU), F32),
                            jnp.zeros((bp, LRU_CONV - 1, D_LRU), F32),
                            jnp.zeros((bp, FFN_CONV - 1, D_FF), F32),
                            dict(wts, **wb), tri, tm=512, tt_lru=128, tt_ffn=256)
        st_p.append(sp)
    stack = lambda sts, i: jnp.stack([s[i] for s in sts])
    return (xp, xs) + tuple(stack(st_p, i) for i in range(5)) + tuple(stack(st_s, i) for i in range(5))
```

```python
import functools
import math

import jax
import jax.numpy as jnp
from jax import lax
from jax.experimental import pallas as pl
from jax.experimental.pallas import tpu as pltpu

F32 = jnp.float32
BF16 = jnp.bfloat16

D_MODEL = 4096
HEAD_DIM = 128
N_HEADS = 8
D_SB = N_HEADS * HEAD_DIM
D_LRU = D_MODEL - D_SB
LRU_BLOCKS = 16
LRU_BLOCK_DIM = D_LRU // LRU_BLOCKS
LRU_PAIR = 2 * LRU_BLOCK_DIM
N_PAIRS = LRU_BLOCKS // 2
LRU_CONV = 4
LRU_C = 8.0
D_FF = 3 * D_MODEL
FFN_CONV = 3
EPS = 1e-6
LOG2E = math.log2(math.e)
Q_SCALE = HEAD_DIM ** -0.5 * LOG2E
HALO = 8
SB_TILE = 256
MXU_COLS = 256
ROW_TILE_SINGLE_BUFFER = 1024

VMEM_LIMIT = 60 * 1024 * 1024


def _params(*sem):
    return pltpu.CompilerParams(dimension_semantics=sem, vmem_limit_bytes=VMEM_LIMIT)


def _rms(x, g):
    ms = jnp.mean(x * x, axis=-1, keepdims=True)
    return (x * lax.rsqrt(ms + EPS)) * g


def _rms_rows(x_ref, g_ref, o_ref, chunk=256):
    g = g_ref[...]
    for r in range(0, x_ref.shape[0], chunk):
        rows = slice(r, min(r + chunk, x_ref.shape[0]))
        o_ref[rows, :] = _rms(x_ref[rows, :], g).astype(BF16)


def _gelu(x):
    return x * (0.5 * (1.0 + jnp.tanh(0.7978845608028654 * (x + 0.044715 * (x * x * x)))))


def _sigmoid(x):
    return 1.0 / (1.0 + jnp.exp(-x))


def _softplus(x):
    return jnp.maximum(x, 0.0) + jnp.log1p(jnp.exp(-jnp.abs(x)))


def _row_tile_spec(tm, k):
    mode = {"pipeline_mode": pl.Buffered(1)} if tm >= ROW_TILE_SINGLE_BUFFER else {}
    return pl.BlockSpec((tm, k), lambda i, j: (i, 0), **mode)


def _weight_block(w_ref, wb_ref):
    w = w_ref[...]
    if wb_ref is not None:
        w = w.astype(BF16)
        wb_ref[...] = w
    return w


def _norm_matmul_body(x_ref, g_ref, w_ref, o_ref, *rest, emit):
    wb_ref, hn_ref = rest if emit else (None,) + rest

    @pl.when(pl.program_id(1) == 0)
    def _():
        _rms_rows(x_ref, g_ref, hn_ref)

    o_ref[...] = jnp.dot(hn_ref[...], _weight_block(w_ref, wb_ref), preferred_element_type=F32)


def _norm_matmul(x, g, w, tm, tn, name, emit=False):
    m, k = x.shape
    n = w.shape[1]
    assert not emit or m == tm
    wspec = pl.BlockSpec((k, tn), lambda i, j: (0, j))
    ospec = pl.BlockSpec((tm, tn), lambda i, j: (i, j))
    oshape = jax.ShapeDtypeStruct((m, n), F32)
    return pl.pallas_call(
        functools.partial(_norm_matmul_body, emit=emit),
        grid=(m // tm, n // tn),
        in_specs=[
            _row_tile_spec(tm, k),
            pl.BlockSpec((1, k), lambda i, j: (0, 0)),
            wspec,
        ],
        out_specs=[ospec, wspec] if emit else ospec,
        out_shape=[oshape, jax.ShapeDtypeStruct((k, n), BF16)] if emit else oshape,
        scratch_shapes=[pltpu.VMEM((tm, k), BF16)],
        compiler_params=_params("parallel", "arbitrary"),
        name=name,
    )(x, g, w)


def _outproj_body(a_ref, y_ref, gsb_ref, w1_ref, w2_ref, res_ref, o_ref, *rest, emit):
    w1b_ref, w2b_ref, yn_ref = rest if emit else (None, None) + rest

    @pl.when(pl.program_id(1) == 0)
    def _():
        _rms_rows(y_ref, gsb_ref, yn_ref)

    acc = jnp.dot(a_ref[...], _weight_block(w1_ref, w1b_ref), preferred_element_type=F32)
    acc = acc + jnp.dot(yn_ref[...], _weight_block(w2_ref, w2b_ref), preferred_element_type=F32)
    o_ref[...] = res_ref[...] + acc


def _outproj(a, y_sb, g_sb, w1, w2, res, tm, tn, emit=False):
    m = a.shape[0]
    n = w1.shape[1]
    assert not emit or m == tm
    w2_row_block = (w2.shape[0] - D_SB) // D_SB
    ospec = pl.BlockSpec((tm, tn), lambda i, j: (i, j))
    oshape = jax.ShapeDtypeStruct((m, n), F32)
    wbspecs = [pl.BlockSpec((D_LRU, tn), lambda i, j: (0, j)), pl.BlockSpec((D_SB, tn), lambda i, j: (0, j))]
    wbshapes = [jax.ShapeDtypeStruct((D_LRU, n), BF16), jax.ShapeDtypeStruct((D_SB, n), BF16)]
    return pl.pallas_call(
        functools.partial(_outproj_body, emit=emit),
        grid=(m // tm, n // tn),
        in_specs=[
            _row_tile_spec(tm, D_LRU),
            _row_tile_spec(tm, D_SB),
            pl.BlockSpec((1, D_SB), lambda i, j: (0, 0)),
            pl.BlockSpec((D_LRU, tn), lambda i, j: (0, j)),
            pl.BlockSpec((D_SB, tn), lambda i, j: (w2_row_block, j)),
            ospec,
        ],
        out_specs=[ospec] + wbspecs if emit else ospec,
        out_shape=[oshape] + wbshapes if emit else oshape,
        scratch_shapes=[pltpu.VMEM((tm, D_SB), BF16)],
        compiler_params=_params("parallel", "arbitrary"),
        name="out_proj",
    )(a, y_sb, g_sb, w1, w2, res)


def _down_body(g_ref, w_ref, res_ref, o_ref, *rest, emit):
    wb_ref, acc_ref = rest if emit else (None,) + rest
    k = pl.program_id(2)

    @pl.when(k == 0)
    def _():
        acc_ref[...] = jnp.zeros_like(acc_ref)

    acc_ref[...] += jnp.dot(g_ref[...], _weight_block(w_ref, wb_ref), preferred_element_type=F32)

    @pl.when(k == pl.num_programs(2) - 1)
    def _():
        o_ref[...] = res_ref[...] + acc_ref[...]


def _down(g, w, res, tm, tn, tk, emit=False):
    m, kdim = g.shape
    n = w.shape[1]
    assert not emit or m == tm
    wspec = pl.BlockSpec((tk, tn), lambda i, j, k: (k, j))
    ospec = pl.BlockSpec((tm, tn), lambda i, j, k: (i, j))
    oshape = jax.ShapeDtypeStruct((m, n), F32)
    return pl.pallas_call(
        functools.partial(_down_body, emit=emit),
        grid=(m // tm, n // tn, kdim // tk),
        in_specs=[pl.BlockSpec((tm, tk), lambda i, j, k: (i, k)), wspec, ospec],
        out_specs=[ospec, wspec] if emit else ospec,
        out_shape=[oshape, jax.ShapeDtypeStruct((kdim, n), BF16)] if emit else oshape,
        scratch_shapes=[pltpu.VMEM((tm, tn), F32)],
        compiler_params=_params("parallel", "parallel", "arbitrary"),
        name="ffn_down",
    )(g, w, res)


def _lru_body(u_ref, gate_ref, h0_ref, cprev_ref, cw_ref, cb_ref, wa_ref, ba_ref, wx_ref, bx_ref,
              lam_ref, og_ref, y_ref, hlast_ref, cstate_ref,
              ext_ref, uc_ref, a_ref, b_ref, hs_ref, hc_ref, *, tt):
    @pl.when(pl.program_id(1) == 0)
    def _():
        hc_ref[...] = h0_ref[0]
        ext_ref[0:HALO, :] = cprev_ref[0]

    u = u_ref[...]
    ext_ref[HALO:HALO + tt, :] = u
    cw = cw_ref[...]
    uc = cb_ref[...]
    for k in range(LRU_CONV - 1):
        lo = HALO - (LRU_CONV - 1) + k
        uc = uc + ext_ref[lo:lo + tt, :] * cw[k:k + 1, :]
    uc_ref[...] = uc + u * cw[LRU_CONV - 1:LRU_CONV, :]

    last = ext_ref[tt:tt + HALO, :]
    cstate_ref[0] = last
    ext_ref[0:HALO, :] = last

    for p in range(N_PAIRS):
        sl = slice(p * LRU_PAIR, (p + 1) * LRU_PAIR)
        ucp = uc_ref[:, sl]
        xb = ucp.astype(BF16)
        r = _sigmoid(jnp.dot(xb, wa_ref[p], preferred_element_type=F32) + ba_ref[:, sl])
        i = _sigmoid(jnp.dot(xb, wx_ref[p], preferred_element_type=F32) + bx_ref[:, sl])
        log_a = (LRU_C * r) * (-_softplus(-lam_ref[:, sl]))
        a = jnp.exp(log_a)
        a_ref[:, sl] = a
        b_ref[:, sl] = jnp.sqrt(-jnp.tanh(log_a) * (a * a + 1.0)) * (i * ucp)

    def step(t, h):
        h = a_ref[pl.ds(t, 1), :] * h + b_ref[pl.ds(t, 1), :]
        hs_ref[pl.ds(t, 1), :] = h
        return h

    h = lax.fori_loop(0, tt, step, hc_ref[...], unroll=8)
    hc_ref[...] = h
    hlast_ref[0] = h

    y = hs_ref[...] * _gelu(gate_ref[...])
    y_ref[...] = _rms(y, og_ref[...]).astype(BF16)


def _lru(proj, h0, cprev, wts, nseq, seqlen, tt):
    m = nseq * seqlen
    nt = seqlen // tt
    row = lambda b, t: (b * nt + t, 0)
    vec = lambda b, t: (0, 0)
    return pl.pallas_call(
        functools.partial(_lru_body, tt=tt),
        grid=(nseq, nt),
        in_specs=[
            pl.BlockSpec((tt, D_LRU), row),
            pl.BlockSpec((tt, D_LRU), lambda b, t: (b * nt + t, 1)),
            pl.BlockSpec((1, 1, D_LRU), lambda b, t: (b, 0, 0)),
            pl.BlockSpec((1, HALO, D_LRU), lambda b, t: (b, 0, 0)),
            pl.BlockSpec((HALO, D_LRU), vec),
            pl.BlockSpec((1, D_LRU), vec),
            pl.BlockSpec((N_PAIRS, LRU_PAIR, LRU_PAIR), lambda b, t: (0, 0, 0)),
            pl.BlockSpec((1, D_LRU), vec),
            pl.BlockSpec((N_PAIRS, LRU_PAIR, LRU_PAIR), lambda b, t: (0, 0, 0)),
            pl.BlockSpec((1, D_LRU), vec),
            pl.BlockSpec((1, D_LRU), vec),
            pl.BlockSpec((1, D_LRU), vec),
        ],
        out_specs=[
            pl.BlockSpec((tt, D_LRU), row),
            pl.BlockSpec((1, 1, D_LRU), lambda b, t: (b, 0, 0)),
            pl.BlockSpec((1, HALO, D_LRU), lambda b, t: (b, 0, 0)),
        ],
        out_shape=[
            jax.ShapeDtypeStruct((m, D_LRU), BF16),
            jax.ShapeDtypeStruct((nseq, 1, D_LRU), F32),
            jax.ShapeDtypeStruct((nseq, HALO, D_LRU), F32),
        ],
        scratch_shapes=[
            pltpu.VMEM((tt + HALO, D_LRU), F32),
            pltpu.VMEM((tt, D_LRU), F32),
            pltpu.VMEM((tt, D_LRU), F32),
            pltpu.VMEM((tt, D_LRU), F32),
            pltpu.VMEM((tt, D_LRU), F32),
            pltpu.VMEM((1, D_LRU), F32),
        ],
        compiler_params=_params("arbitrary", "arbitrary"),
        name="rg_lru",
    )(proj, proj, h0, cprev, wts["lru_conv_w"], wts["lru_conv_b"], wts["lru_w_a"], wts["lru_b_a"],
      wts["lru_w_x"], wts["lru_b_x"], wts["lru_lambda"], wts["lru_out_g"])


def _qkv_body(q_ref, k_ref, v_ref, gq_ref, gk_ref, qb_ref, kf_ref, kb_ref, vf_ref, vb_ref):
    for h in range(N_HEADS):
        sl = slice(h * HEAD_DIM, (h + 1) * HEAD_DIM)
        qb_ref[:, sl] = (_rms(q_ref[:, sl], gq_ref[...]) * Q_SCALE).astype(BF16)
        kn = _rms(k_ref[:, sl], gk_ref[...])
        kf_ref[:, sl] = kn
        kb_ref[:, sl] = kn.astype(BF16)
    v = v_ref[...]
    vf_ref[...] = v
    vb_ref[...] = v.astype(BF16)


def _qkv(proj, gq, gk, tm):
    m = proj.shape[0]
    col0 = 2 * D_LRU // D_SB
    blk = lambda c: pl.BlockSpec((tm, D_SB), lambda i: (i, c))
    vec = pl.BlockSpec((1, HEAD_DIM), lambda i: (0, 0))
    out = pl.BlockSpec((tm, D_SB), lambda i: (i, 0))
    return pl.pallas_call(
        _qkv_body,
        grid=(m // tm,),
        in_specs=[blk(col0), blk(col0 + 1), blk(col0 + 2), vec, vec],
        out_specs=[out] * 5,
        out_shape=[jax.ShapeDtypeStruct((m, D_SB), d) for d in (BF16, F32, BF16, F32, BF16)],
        compiler_params=_params("parallel"),
        name="qkv_norm",
    )(proj, proj, proj, gq, gk)


def _neg_abs(x):
    return lax.bitcast_convert_type(lax.bitcast_convert_type(x, jnp.int32) | jnp.int32(-2 ** 31), F32)


def _sb_step(q, k, v, tri, c_ref, crows, acc_ref, arows, acols, masked):
    z = lax.dot_general(q, k, (((1,), (1,)), ((), ())), preferred_element_type=F32)
    sp = jnp.maximum(z, 0.0) + jnp.log(1.0 + jnp.exp2(_neg_abs(z))) * LOG2E
    if masked:
        mask = (lax.broadcasted_iota(jnp.int32, z.shape, 1) < lax.broadcasted_iota(jnp.int32, z.shape, 0))
        sp = jnp.where(mask, sp, 0.0)
    inner = jnp.dot(sp.astype(BF16), tri, preferred_element_type=F32)
    c = c_ref[crows, :]
    w = jnp.exp2((z - sp) - (inner + c))
    if masked:
        w = jnp.where(mask, w, 0.0)
    acc_ref[arows, acols] += jnp.dot(w.astype(BF16), v, preferred_element_type=F32)
    c_ref[crows, :] = c + jnp.sum(sp, axis=-1, keepdims=True)


NEG_BIG = -1e30
SKIP_LOG2 = 160.0


def _sb_logits(q, k, z_ref, slot):
    z_ref[slot] = lax.dot_general(q, k, (((1,), (1,)), ((), ())), preferred_element_type=F32)


def _sb_scores(z_ref, c_ref, crows, t_ref, spb_ref, slot, mode):
    if mode == "none":
        t_ref[slot] = jnp.full(t_ref.shape[1:], NEG_BIG, F32)
        spb_ref[slot] = jnp.zeros(spb_ref.shape[1:], BF16)
        return
    z = z_ref[slot]
    sp = jnp.maximum(z, 0.0) + jnp.log(1.0 + jnp.exp2(_neg_abs(z))) * LOG2E
    c = c_ref[crows, :]
    t = (z - sp) - c
    if mode == "diag":
        mask = (lax.broadcasted_iota(jnp.int32, z.shape, 1) < lax.broadcasted_iota(jnp.int32, z.shape, 0))
        sp = jnp.where(mask, sp, 0.0)
        t = jnp.where(mask, t, NEG_BIG)
    t_ref[slot] = t
    spb_ref[slot] = sp.astype(BF16)
    c_ref[crows, :] = c + jnp.sum(sp, axis=-1, keepdims=True)


def _sb_weights(tri, t_ref, spb_ref, wb_ref, slot):
    inner = jnp.dot(spb_ref[slot], tri, preferred_element_type=F32)
    wb_ref[slot] = jnp.exp2(t_ref[slot] - inner).astype(BF16)


def _sb_values(v, wb_ref, slot, acc_ref, arows):
    acc_ref[arows, :] += jnp.dot(wb_ref[slot], v, preferred_element_type=F32)


def _attn_prompt_body(q_ref, k_ref, v_ref, tri_ref, o_ref, c_ref, z_ref, t_ref, spb_ref, wb_ref, *, nsub):
    ts = SB_TILE
    base = pl.program_id(1) * nsub
    tri = tri_ref[...]
    c_ref[...] = jnp.zeros_like(c_ref)
    o_ref[...] = jnp.zeros_like(o_ref)
    rows = [pl.ds(s * ts, ts) for s in range(nsub)]
    qs = [q_ref[r, :] for r in rows]
    cols = slice(None)

    @pl.when(base == 0)
    def _():
        for t in reversed(range(nsub)):
            k = k_ref[t * ts:(t + 1) * ts, :]
            v = v_ref[t * ts:(t + 1) * ts, :]
            for s in range(t, nsub):
                _sb_step(qs[s], k, v, tri, c_ref, rows[s], o_ref, rows[s], cols, masked=(s == t))

    @pl.when(base > 0)
    def _():
        n = base + nsub

        def tile(i):
            return pl.ds(pl.multiple_of((n - 1 - i) * ts, ts), ts)

        def modes(i):
            kt = nsub - 1 - i
            return tuple("full" if (kt < 0 or s > kt) else ("diag" if s == kt else "none")
                         for s in range(nsub))

        def iteration(i, p, first=1, last=4, static_i=None):
            if last >= 4 and first <= 4:
                v = v_ref[tile(i - 3), :]
                for s in range(nsub):
                    _sb_values(v, wb_ref.at[1 - p], s, o_ref, rows[s])
            if last >= 3 and first <= 3:
                for s in range(nsub):
                    _sb_weights(tri, t_ref.at[1 - p], spb_ref.at[1 - p], wb_ref.at[p], s)
            if last >= 2 and first <= 2:
                md = modes(static_i - 1) if static_i is not None else ("full",) * nsub
                for s in range(nsub):
                    _sb_scores(z_ref.at[1 - p], c_ref, rows[s], t_ref.at[p], spb_ref.at[p], s, md[s])
            if last >= 1 and first <= 1:
                k = k_ref[tile(i), :]
                for s in range(nsub):
                    _sb_logits(qs[s], k, z_ref.at[p], s)

        nfill = 4
        assert nsub == 2
        for i in range(nfill):
            iteration(i, i & 1, last=min(i + 1, 4), static_i=i)

        def cond(carry):
            j, cmin = carry
            return jnp.logical_and(j < (n - nfill) // 2, cmin < SKIP_LOG2)

        def body(carry):
            j, _ = carry
            i = nfill + 2 * j
            iteration(i, 0)
            iteration(i + 1, 1)
            return j + 1, jnp.min(c_ref[...])

        trips, _ = lax.while_loop(cond, body, (jnp.int32(0), jnp.min(c_ref[...])))
        issued = nfill + 2 * trips
        for d in range(3):
            iteration(issued + d, d & 1, first=d + 2)


def _attn_prompt(qb, kb, vb, tri, nsub):
    t = qb.shape[0]
    tq = nsub * SB_TILE
    return pl.pallas_call(
        functools.partial(_attn_prompt_body, nsub=nsub),
        grid=(N_HEADS, t // tq),
        in_specs=[
            pl.BlockSpec((tq, HEAD_DIM), lambda h, i: (i, h)),
            pl.BlockSpec((t, HEAD_DIM), lambda h, i: (0, h)),
            pl.BlockSpec((t, HEAD_DIM), lambda h, i: (0, h)),
            pl.BlockSpec((SB_TILE, SB_TILE), lambda h, i: (0, 0)),
        ],
        out_specs=pl.BlockSpec((tq, HEAD_DIM), lambda h, i: (i, h)),
        out_shape=jax.ShapeDtypeStruct((t, D_SB), F32),
        scratch_shapes=[pltpu.VMEM((tq, 1), F32),
                        pltpu.VMEM((2, nsub, SB_TILE, SB_TILE), F32),
                        pltpu.VMEM((2, nsub, SB_TILE, SB_TILE), F32),
                        pltpu.VMEM((2, nsub, SB_TILE, SB_TILE), BF16),
                        pltpu.VMEM((2, nsub, SB_TILE, SB_TILE), BF16)],
        compiler_params=_params("parallel", "arbitrary"),
        name="sb_attn_prompt",
    )(qb, kb, vb, tri)


def _attn_sample_body(q_ref, kn_ref, vn_ref, kc_ref, vc_ref, trin_ref, tri_ref, o_ref, c_ref, *, tq, chunk):
    ts = SB_TILE
    tri = tri_ref[...]
    rows = [pl.ds(h * tq, tq) for h in range(N_HEADS)]
    cols = [slice(h * HEAD_DIM, (h + 1) * HEAD_DIM) for h in range(N_HEADS)]
    qs = [q_ref[:, cl] for cl in cols]

    @pl.when(pl.program_id(1) == 0)
    def _():
        c_ref[...] = jnp.zeros_like(c_ref)
        o_ref[...] = jnp.zeros_like(o_ref)
        trin = trin_ref[...]
        for h in range(N_HEADS):
            _sb_step(qs[h], kn_ref[:, cols[h]], vn_ref[:, cols[h]], trin, c_ref, rows[h], o_ref, slice(None), cols[h], True)

    @pl.when(jnp.min(c_ref[...]) < SKIP_LOG2)
    def _():
        for t in reversed(range(chunk // ts)):
            for h in range(N_HEADS):
                sel = pl.ds(t * ts * N_HEADS + h, ts, stride=N_HEADS)
                k = kc_ref[sel, :].astype(BF16)
                v = vc_ref[sel, :].astype(BF16)
                _sb_step(qs[h], k, v, tri, c_ref, rows[h], o_ref, slice(None), cols[h], False)


def _attn_sample(qb, kb, vb, kc, vc, tri_new, tri, nseq, tq, chunk):
    nchunk = kc.shape[1] // (chunk * N_HEADS)
    new = pl.BlockSpec((tq, D_SB), lambda b, c: (b, 0))
    cache = pl.BlockSpec((None, chunk * N_HEADS, HEAD_DIM), lambda b, c: (b, nchunk - 1 - c, 0))
    return pl.pallas_call(
        functools.partial(_attn_sample_body, tq=tq, chunk=chunk),
        grid=(nseq, nchunk),
        in_specs=[new, new, new, cache, cache,
                  pl.BlockSpec((tq, tq), lambda b, c: (0, 0)),
                  pl.BlockSpec((SB_TILE, SB_TILE), lambda b, c: (0, 0))],
        out_specs=new,
        out_shape=jax.ShapeDtypeStruct((nseq * tq, D_SB), F32),
        scratch_shapes=[pltpu.VMEM((N_HEADS * tq, 1), F32)],
        compiler_params=_params("parallel", "arbitrary"),
        name="sb_attn_sample",
    )(qb, kb, vb, kc, vc, tri_new, tri)


def _ffn_conv_act(pre, val, ext_ref, cw_ref, cb_ref, tt, cols=slice(None)):
    ext_ref[HALO:HALO + tt, cols] = pre
    cw = cw_ref[:, cols]
    pc = cb_ref[:, cols]
    for k in range(FFN_CONV - 1):
        lo = HALO - (FFN_CONV - 1) + k
        pc = pc + ext_ref[lo:lo + tt, cols] * cw[k:k + 1, :]
    pc = pc + pre * cw[FFN_CONV - 1:FFN_CONV, :]
    return (_gelu(pc) * val).astype(BF16), ext_ref[tt:tt + HALO, cols]


def _ffn_act_body(val_ref, pre_ref, sprev_ref, cw_ref, cb_ref, g_ref, sout_ref, ext_ref, *, tt):
    @pl.when(pl.program_id(2) == 0)
    def _():
        ext_ref[0:HALO, :] = sprev_ref[0]

    g, last = _ffn_conv_act(pre_ref[...], val_ref[...], ext_ref, cw_ref, cb_ref, tt)
    g_ref[...] = g
    sout_ref[0] = last
    ext_ref[0:HALO, :] = last


def _ffn_act(up, sprev, cw, cb, nseq, seqlen, tt, tc):
    m = nseq * seqlen
    nt = seqlen // tt
    nc = D_FF // tc
    return pl.pallas_call(
        functools.partial(_ffn_act_body, tt=tt),
        grid=(nseq, nc, nt),
        in_specs=[
            pl.BlockSpec((tt, tc), lambda b, c, t: (b * nt + t, c)),
            pl.BlockSpec((tt, tc), lambda b, c, t: (b * nt + t, nc + c)),
            pl.BlockSpec((1, HALO, tc), lambda b, c, t: (b, 0, c)),
            pl.BlockSpec((HALO, tc), lambda b, c, t: (0, c)),
            pl.BlockSpec((1, tc), lambda b, c, t: (0, c)),
        ],
        out_specs=[
            pl.BlockSpec((tt, tc), lambda b, c, t: (b * nt + t, c)),
            pl.BlockSpec((1, HALO, tc), lambda b, c, t: (b, 0, c)),
        ],
        out_shape=[
            jax.ShapeDtypeStruct((m, D_FF), BF16),
            jax.ShapeDtypeStruct((nseq, HALO, D_FF), F32),
        ],
        scratch_shapes=[pltpu.VMEM((tt + HALO, tc), F32)],
        compiler_params=_params("parallel", "parallel", "arbitrary"),
        name="ffn_act",
    )(up, up, sprev, cw, cb)


def _ffn_up_act_body(x_ref, g_ref, wv_ref, wp_ref, sprev_ref, cw_ref, cb_ref, o_ref, last_ref,
                     hn_ref, halo_ref, ext_ref, *, tm):
    i = pl.program_id(0)
    j = pl.program_id(1)

    @pl.when(j == 0)
    def _():
        _rms_rows(x_ref, g_ref, hn_ref)

    @pl.when(i == 0)
    def _():
        ext_ref[0:HALO, :] = sprev_ref[...]

    @pl.when(i > 0)
    def _():
        ext_ref[0:HALO, :] = halo_ref[j]

    hn = hn_ref[...]
    tn = o_ref.shape[1]
    halves = [slice(h * MXU_COLS, (h + 1) * MXU_COLS) for h in range(tn // MXU_COLS)]
    pres = [jnp.dot(hn, wp_ref[:, cs], preferred_element_type=F32) for cs in halves]
    vals = [jnp.dot(hn, wv_ref[:, cs], preferred_element_type=F32) for cs in halves]
    for cs, pre, val in zip(halves, pres, vals):
        g, last = _ffn_conv_act(pre, val, ext_ref, cw_ref, cb_ref, tm, cs)
        o_ref[:, cs] = g
        halo_ref[j, :, cs] = last
        last_ref[0, :, cs] = last


def _ffn_up_act(x, g, w_up, sprev, cw, cb, tm, tn):
    m, k = x.shape
    nj = D_FF // tn
    return pl.pallas_call(
        functools.partial(_ffn_up_act_body, tm=tm),
        grid=(m // tm, nj),
        in_specs=[
            _row_tile_spec(tm, k),
            pl.BlockSpec((1, k), lambda i, j: (0, 0)),
            pl.BlockSpec((k, tn), lambda i, j: (0, j)),
            pl.BlockSpec((k, tn), lambda i, j: (0, nj + j)),
            pl.BlockSpec((HALO, tn), lambda i, j: (0, j)),
            pl.BlockSpec((HALO, tn), lambda i, j: (0, j)),
            pl.BlockSpec((1, tn), lambda i, j: (0, j)),
        ],
        out_specs=[
            pl.BlockSpec((tm, tn), lambda i, j: (i, j)),
            pl.BlockSpec((1, HALO, tn), lambda i, j: (i, 0, j)),
        ],
        out_shape=[
            jax.ShapeDtypeStruct((m, D_FF), BF16),
            jax.ShapeDtypeStruct((m // tm, HALO, D_FF), F32),
        ],
        scratch_shapes=[
            pltpu.VMEM((tm, k), BF16),
            pltpu.VMEM((nj, HALO, tn), F32),
            pltpu.VMEM((tm + HALO, tn), F32),
        ],
        compiler_params=_params("arbitrary", "arbitrary"),
        name="ffn_up_act",
    )(x, g, w_up, w_up, sprev, cw, cb)


def _front_pad(x, rows):
    return jnp.pad(x, ((0, 0), (rows - x.shape[1], 0), (0, 0)))


def _layer(x3, kpast, vpast, h0, cprev, fprev, wts, tri, tm, tt_lru, tt_ffn, attn_nsub=2, fuse_ffn=None,
           emit=False):
    nseq, seqlen, _ = x3.shape
    m = nseq * seqlen
    x = x3.reshape(m, D_MODEL)
    if fuse_ffn is None:
        fuse_ffn = nseq == 1
    assert not (emit and fuse_ffn)
    tn = 512 if emit else 1024
    tm_wide = ROW_TILE_SINGLE_BUFFER if m % ROW_TILE_SINGLE_BUFFER == 0 else tm
    wb = {}

    proj = _norm_matmul(x, wts["norm_mix_g"], wts["w_in"], tm_wide, 512, "in_proj", emit)
    if emit:
        proj, wb["w_in"] = proj
    y_lru, h_last, cstate = _lru(proj, h0[:, None, :], _front_pad(cprev, HALO), wts, nseq, seqlen, tt_lru)
    qb, kf, kb, vf, vb = _qkv(proj, wts["q_norm_g"], wts["k_norm_g"], tm)
    if kpast is None:
        y_sb = _attn_prompt(qb, kb, vb, tri, attn_nsub)
    else:
        past = kpast.shape[1]
        y_sb = _attn_sample(qb, kb, vb, kpast.reshape(nseq, past * N_HEADS, HEAD_DIM),
                            vpast.reshape(nseq, past * N_HEADS, HEAD_DIM),
                            tri[:seqlen, :seqlen], tri, nseq, seqlen, min(past, 1024))
    if emit:
        x1, wb["w_out_lru"], wb["w_out_sb"] = _outproj(y_lru, y_sb, wts["sb_out_g"], wts["w_out"], wts["w_out"],
                                                        x, tm, tn, emit)
    else:
        x1 = _outproj(y_lru, y_sb, wts["sb_out_g"], wts["w_out_lru"], wts["w_out_sb"], x, tm_wide, 512)

    fprev8 = _front_pad(fprev, HALO)
    if fuse_ffn:
        g, lasts = _ffn_up_act(x1, wts["norm_ffn_g"], wts["w_ffn_up"], fprev8[0], wts["ffn_conv_w"],
                               wts["ffn_conv_b"], tm_wide, 512)
        fstate = lasts[-1:]
    else:
        up = _norm_matmul(x1, wts["norm_ffn_g"], wts["w_ffn_up"], tm, tn, "ffn_up", emit)
        if emit:
            up, wb["w_ffn_up"] = up
        g, fstate = _ffn_act(up, fprev8, wts["ffn_conv_w"], wts["ffn_conv_b"], nseq, seqlen, tt_ffn, 2048)
    out = _down(g, wts["w_ffn_down"], x1, min(m, 1024), 1024, 2048 if emit else 3072, emit)
    if emit:
        out, wb["w_ffn_down"] = out

    return (out.reshape(nseq, seqlen, D_MODEL),
            kf.reshape(nseq, seqlen, N_HEADS, HEAD_DIM),
            vf.reshape(nseq, seqlen, N_HEADS, HEAD_DIM),
            h_last[:, 0, :],
            cstate[:, HALO - (LRU_CONV - 1):, :],
            fstate[:, HALO - (FFN_CONV - 1):, :],
            wb)


def _pair_blocks(w):
    z = jnp.zeros((N_PAIRS, LRU_BLOCK_DIM, LRU_BLOCK_DIM), w.dtype)
    top = jnp.concatenate([w[0::2], z], axis=2)
    bot = jnp.concatenate([z, w[1::2]], axis=2)
    return jnp.concatenate([top, bot], axis=1).astype(BF16)


def _prep_weights(norm_mix_g, w_in, lru_conv_w, lru_conv_b, lru_w_a, lru_b_a, lru_w_x, lru_b_x,
                  lru_lambda, q_norm_g, k_norm_g, lru_out_g, sb_out_g, w_out, norm_ffn_g,
                  w_ffn_up, ffn_conv_w, ffn_conv_b, w_ffn_down):
    row = lambda v: v.reshape(1, -1)
    pad_rows = lambda w: jnp.pad(w, ((0, HALO - w.shape[0]), (0, 0)))
    return {
        "norm_mix_g": row(norm_mix_g), "w_in": w_in,
        "lru_conv_w": pad_rows(lru_conv_w), "lru_conv_b": row(lru_conv_b),
        "lru_w_a": _pair_blocks(lru_w_a), "lru_b_a": row(lru_b_a),
        "lru_w_x": _pair_blocks(lru_w_x), "lru_b_x": row(lru_b_x),
        "lru_lambda": row(lru_lambda),
        "q_norm_g": row(q_norm_g), "k_norm_g": row(k_norm_g),
        "lru_out_g": row(lru_out_g), "sb_out_g": row(sb_out_g), "w_out": w_out,
        "norm_ffn_g": row(norm_ffn_g), "w_ffn_up": w_ffn_up,
        "ffn_conv_w": pad_rows(ffn_conv_w), "ffn_conv_b": row(ffn_conv_b),
        "w_ffn_down": w_ffn_down,
    }


def kernel(x_prompt, x_sample, cache_sb_k, cache_sb_v, state_lru_h, state_lru_conv, state_ffn_conv, norm_mix_g, w_in, lru_conv_w, lru_conv_b, lru_w_a, lru_b_a, lru_w_x, lru_b_x, lru_lambda, q_norm_g, k_norm_g, lru_out_g, sb_out_g, w_out, norm_ffn_g, w_ffn_up, ffn_conv_w, ffn_conv_b, w_ffn_down):
    depth = w_in.shape[0]
    bp = x_prompt.shape[0]
    tri = jnp.tri(SB_TILE, k=-1, dtype=BF16)
    xp, xs = x_prompt, x_sample
    st_p, st_s = [], []
    for l in range(depth):
        wts = _prep_weights(norm_mix_g[l], w_in[l], lru_conv_w[l], lru_conv_b[l], lru_w_a[l], lru_b_a[l],
                            lru_w_x[l], lru_b_x[l], lru_lambda[l], q_norm_g[l], k_norm_g[l], lru_out_g[l],
                            sb_out_g[l], w_out[l], norm_ffn_g[l], w_ffn_up[l], ffn_conv_w[l],
                            ffn_conv_b[l], w_ffn_down[l])
        xs, *ss, wb = _layer(xs, cache_sb_k[l], cache_sb_v[l], state_lru_h[l], state_lru_conv[l],
                             state_ffn_conv[l], wts, tri, tm=xs.shape[0] * xs.shape[1],
                             tt_lru=xs.shape[1], tt_ffn=xs.shape[1], emit=True)
        st_s.append(ss)
        xp, *sp, _ = _layer(xp, None, None,
                            jnp.zeros((bp, D_LRU), F32),
                            jnp.zeros((bp, LRU_CONV - 1, D_LRU), F32),
                            jnp.zeros((bp, FFN_CONV - 1, D_FF), F32),
                            dict(wts, **wb), tri, tm=512, tt_lru=128, tt_ffn=256)
        st_p.append(sp)
    stack = lambda sts, i: jnp.stack([s[i] for s in sts])
    return (xp, xs) + tuple(stack(st_p, i) for i in range(5)) + tuple(stack(st_s, i) for i in range(5))
```

```python
import functools
import math

import jax
import jax.numpy as jnp
from jax import lax
from jax.experimental import pallas as pl
from jax.experimental.pallas import tpu as pltpu

F32 = jnp.float32
BF16 = jnp.bfloat16

D_MODEL = 4096
HEAD_DIM = 128
N_HEADS = 8
D_SB = N_HEADS * HEAD_DIM
D_LRU = D_MODEL - D_SB
LRU_BLOCKS = 16
LRU_BLOCK_DIM = D_LRU // LRU_BLOCKS
LRU_PAIR = 2 * LRU_BLOCK_DIM
N_PAIRS = LRU_BLOCKS // 2
LRU_CONV = 4
LRU_C = 8.0
D_FF = 3 * D_MODEL
FFN_CONV = 3
EPS = 1e-6
LOG2E = math.log2(math.e)
Q_SCALE = HEAD_DIM ** -0.5 * LOG2E
HALO = 8
SB_TILE = 256
MXU_COLS = 256
ROW_TILE_SINGLE_BUFFER = 1024

VMEM_LIMIT = 60 * 1024 * 1024


def _params(*sem):
    return pltpu.CompilerParams(dimension_semantics=sem, vmem_limit_bytes=VMEM_LIMIT)


def _rms(x, g):
    ms = jnp.mean(x * x, axis=-1, keepdims=True)
    return (x * lax.rsqrt(ms + EPS)) * g


def _rms_rows(x_ref, g_ref, o_ref, chunk=256):
    g = g_ref[...]
    for r in range(0, x_ref.shape[0], chunk):
        rows = slice(r, min(r + chunk, x_ref.shape[0]))
        o_ref[rows, :] = _rms(x_ref[rows, :], g).astype(BF16)


def _gelu(x):
    return x * (0.5 * (1.0 + jnp.tanh(0.7978845608028654 * (x + 0.044715 * (x * x * x)))))


def _sigmoid(x):
    return 1.0 / (1.0 + jnp.exp(-x))


def _softplus(x):
    return jnp.maximum(x, 0.0) + jnp.log1p(jnp.exp(-jnp.abs(x)))


def _row_tile_spec(tm, k):
    mode = {"pipeline_mode": pl.Buffered(1)} if tm >= ROW_TILE_SINGLE_BUFFER else {}
    return pl.BlockSpec((tm, k), lambda i, j: (i, 0), **mode)


def _weight_block(w_ref, wb_ref):
    w = w_ref[...]
    if wb_ref is not None:
        w = w.astype(BF16)
        wb_ref[...] = w
    return w


def _norm_matmul_body(x_ref, g_ref, w_ref, o_ref, *rest, emit):
    wb_ref, hn_ref = rest if emit else (None,) + rest

    @pl.when(pl.program_id(1) == 0)
    def _():
        _rms_rows(x_ref, g_ref, hn_ref)

    o_ref[...] = jnp.dot(hn_ref[...], _weight_block(w_ref, wb_ref), preferred_element_type=F32)


def _norm_matmul(x, g, w, tm, tn, name, emit=False):
    m, k = x.shape
    n = w.shape[1]
    assert not emit or m == tm
    wspec = pl.BlockSpec((k, tn), lambda i, j: (0, j))
    ospec = pl.BlockSpec((tm, tn), lambda i, j: (i, j))
    oshape = jax.ShapeDtypeStruct((m, n), F32)
    return pl.pallas_call(
        functools.partial(_norm_matmul_body, emit=emit),
        grid=(m // tm, n // tn),
        in_specs=[
            _row_tile_spec(tm, k),
            pl.BlockSpec((1, k), lambda i, j: (0, 0)),
            wspec,
        ],
        out_specs=[ospec, wspec] if emit else ospec,
        out_shape=[oshape, jax.ShapeDtypeStruct((k, n), BF16)] if emit else oshape,
        scratch_shapes=[pltpu.VMEM((tm, k), BF16)],
        compiler_params=_params("parallel", "arbitrary"),
        name=name,
    )(x, g, w)


def _outproj_body(a_ref, y_ref, gsb_ref, w1_ref, w2_ref, res_ref, o_ref, *rest, emit):
    w1b_ref, w2b_ref, yn_ref = rest if emit else (None, None) + rest

    @pl.when(pl.program_id(1) == 0)
    def _():
        _rms_rows(y_ref, gsb_ref, yn_ref)

    acc = jnp.dot(a_ref[...], _weight_block(w1_ref, w1b_ref), preferred_element_type=F32)
    acc = acc + jnp.dot(yn_ref[...], _weight_block(w2_ref, w2b_ref), preferred_element_type=F32)
    o_ref[...] = res_ref[...] + acc


def _outproj(a, y_sb, g_sb, w1, w2, res, tm, tn, emit=False):
    m = a.shape[0]
    n = w1.shape[1]
    assert not emit or m == tm
    w2_row_block = (w2.shape[0] - D_SB) // D_SB
    ospec = pl.BlockSpec((tm, tn), lambda i, j: (i, j))
    oshape = jax.ShapeDtypeStruct((m, n), F32)
    wbspecs = [pl.BlockSpec((D_LRU, tn), lambda i, j: (0, j)), pl.BlockSpec((D_SB, tn), lambda i, j: (0, j))]
    wbshapes = [jax.ShapeDtypeStruct((D_LRU, n), BF16), jax.ShapeDtypeStruct((D_SB, n), BF16)]
    return pl.pallas_call(
        functools.partial(_outproj_body, emit=emit),
        grid=(m // tm, n // tn),
        in_specs=[
            _row_tile_spec(tm, D_LRU),
            _row_tile_spec(tm, D_SB),
            pl.BlockSpec((1, D_SB), lambda i, j: (0, 0)),
            pl.BlockSpec((D_LRU, tn), lambda i, j: (0, j)),
            pl.BlockSpec((D_SB, tn), lambda i, j: (w2_row_block, j)),
            ospec,
        ],
        out_specs=[ospec] + wbspecs if emit else ospec,
        out_shape=[oshape] + wbshapes if emit else oshape,
        scratch_shapes=[pltpu.VMEM((tm, D_SB), BF16)],
        compiler_params=_params("parallel", "arbitrary"),
        name="out_proj",
    )(a, y_sb, g_sb, w1, w2, res)


def _down_body(g_ref, w_ref, res_ref, o_ref, *rest, emit):
    wb_ref, acc_ref = rest if emit else (None,) + rest
    k = pl.program_id(2)

    @pl.when(k == 0)
    def _():
        acc_ref[...] = jnp.zeros_like(acc_ref)

    acc_ref[...] += jnp.dot(g_ref[...], _weight_block(w_ref, wb_ref), preferred_element_type=F32)

    @pl.when(k == pl.num_programs(2) - 1)
    def _():
        o_ref[...] = res_ref[...] + acc_ref[...]


def _down(g, w, res, tm, tn, tk, emit=False):
    m, kdim = g.shape
    n = w.shape[1]
    assert not emit or m == tm
    wspec = pl.BlockSpec((tk, tn), lambda i, j, k: (k, j))
    ospec = pl.BlockSpec((tm, tn), lambda i, j, k: (i, j))
    oshape = jax.ShapeDtypeStruct((m, n), F32)
    return pl.pallas_call(
        functools.partial(_down_body, emit=emit),
        grid=(m // tm, n // tn, kdim // tk),
        in_specs=[pl.BlockSpec((tm, tk), lambda i, j, k: (i, k)), wspec, ospec],
        out_specs=[ospec, wspec] if emit else ospec,
        out_shape=[oshape, jax.ShapeDtypeStruct((kdim, n), BF16)] if emit else oshape,
        scratch_shapes=[pltpu.VMEM((tm, tn), F32)],
        compiler_params=_params("parallel", "parallel", "arbitrary"),
        name="ffn_down",
    )(g, w, res)


def _lru_body(u_ref, gate_ref, h0_ref, cprev_ref, cw_ref, cb_ref, wa_ref, ba_ref, wx_ref, bx_ref,
              lam_ref, og_ref, y_ref, hlast_ref, cstate_ref,
              ext_ref, uc_ref, a_ref, b_ref, hs_ref, hc_ref, *, tt):
    @pl.when(pl.program_id(1) == 0)
    def _():
        hc_ref[...] = h0_ref[0]
        ext_ref[0:HALO, :] = cprev_ref[0]

    u = u_ref[...]
    ext_ref[HALO:HALO + tt, :] = u
    cw = cw_ref[...]
    uc = cb_ref[...]
    for k in range(LRU_CONV - 1):
        lo = HALO - (LRU_CONV - 1) + k
        uc = uc + ext_ref[lo:lo + tt, :] * cw[k:k + 1, :]
    uc_ref[...] = uc + u * cw[LRU_CONV - 1:LRU_CONV, :]

    last = ext_ref[tt:tt + HALO, :]
    cstate_ref[0] = last
    ext_ref[0:HALO, :] = last

    for p in range(N_PAIRS):
        sl = slice(p * LRU_PAIR, (p + 1) * LRU_PAIR)
        ucp = uc_ref[:, sl]
        xb = ucp.astype(BF16)
        r = _sigmoid(jnp.dot(xb, wa_ref[p], preferred_element_type=F32) + ba_ref[:, sl])
        i = _sigmoid(jnp.dot(xb, wx_ref[p], preferred_element_type=F32) + bx_ref[:, sl])
        log_a = (LRU_C * r) * (-_softplus(-lam_ref[:, sl]))
        a = jnp.exp(log_a)
        a_ref[:, sl] = a
        b_ref[:, sl] = jnp.sqrt(-jnp.tanh(log_a) * (a * a + 1.0)) * (i * ucp)

    def step(t, h):
        h = a_ref[pl.ds(t, 1), :] * h + b_ref[pl.ds(t, 1), :]
        hs_ref[pl.ds(t, 1), :] = h
        return h

    h = lax.fori_loop(0, tt, step, hc_ref[...], unroll=8)
    hc_ref[...] = h
    hlast_ref[0] = h

    y = hs_ref[...] * _gelu(gate_ref[...])
    y_ref[...] = _rms(y, og_ref[...]).astype(BF16)


def _lru(proj, h0, cprev, wts, nseq, seqlen, tt):
    m = nseq * seqlen
    nt = seqlen // tt
    row = lambda b, t: (b * nt + t, 0)
    vec = lambda b, t: (0, 0)
    return pl.pallas_call(
        functools.partial(_lru_body, tt=tt),
        grid=(nseq, nt),
        in_specs=[
            pl.BlockSpec((tt, D_LRU), row),
            pl.BlockSpec((tt, D_LRU), lambda b, t: (b * nt + t, 1)),
            pl.BlockSpec((1, 1, D_LRU), lambda b, t: (b, 0, 0)),
            pl.BlockSpec((1, HALO, D_LRU), lambda b, t: (b, 0, 0)),
            pl.BlockSpec((HALO, D_LRU), vec),
            pl.BlockSpec((1, D_LRU), vec),
            pl.BlockSpec((N_PAIRS, LRU_PAIR, LRU_PAIR), lambda b, t: (0, 0, 0)),
            pl.BlockSpec((1, D_LRU), vec),
            pl.BlockSpec((N_PAIRS, LRU_PAIR, LRU_PAIR), lambda b, t: (0, 0, 0)),
            pl.BlockSpec((1, D_LRU), vec),
            pl.BlockSpec((1, D_LRU), vec),
            pl.BlockSpec((1, D_LRU), vec),
        ],
        out_specs=[
            pl.BlockSpec((tt, D_LRU), row),
            pl.BlockSpec((1, 1, D_LRU), lambda b, t: (b, 0, 0)),
            pl.BlockSpec((1, HALO, D_LRU), lambda b, t: (b, 0, 0)),
        ],
        out_shape=[
            jax.ShapeDtypeStruct((m, D_LRU), BF16),
            jax.ShapeDtypeStruct((nseq, 1, D_LRU), F32),
            jax.ShapeDtypeStruct((nseq, HALO, D_LRU), F32),
        ],
        scratch_shapes=[
            pltpu.VMEM((tt + HALO, D_LRU), F32),
            pltpu.VMEM((tt, D_LRU), F32),
            pltpu.VMEM((tt, D_LRU), F32),
            pltpu.VMEM((tt, D_LRU), F32),
            pltpu.VMEM((tt, D_LRU), F32),
            pltpu.VMEM((1, D_LRU), F32),
        ],
        compiler_params=_params("arbitrary", "arbitrary"),
        name="rg_lru",
    )(proj, proj, h0, cprev, wts["lru_conv_w"], wts["lru_conv_b"], wts["lru_w_a"], wts["lru_b_a"],
      wts["lru_w_x"], wts["lru_b_x"], wts["lru_lambda"], wts["lru_out_g"])


def _qkv_body(q_ref, k_ref, v_ref, gq_ref, gk_ref, qb_ref, kf_ref, kb_ref, vf_ref, vb_ref):
    for h in range(N_HEADS):
        sl = slice(h * HEAD_DIM, (h + 1) * HEAD_DIM)
        qb_ref[:, sl] = (_rms(q_ref[:, sl], gq_ref[...]) * Q_SCALE).astype(BF16)
        kn = _rms(k_ref[:, sl], gk_ref[...])
        kf_ref[:, sl] = kn
        kb_ref[:, sl] = kn.astype(BF16)
    v = v_ref[...]
    vf_ref[...] = v
    vb_ref[...] = v.astype(BF16)


def _qkv(proj, gq, gk, tm):
    m = proj.shape[0]
    col0 = 2 * D_LRU // D_SB
    blk = lambda c: pl.BlockSpec((tm, D_SB), lambda i: (i, c))
    vec = pl.BlockSpec((1, HEAD_DIM), lambda i: (0, 0))
    out = pl.BlockSpec((tm, D_SB), lambda i: (i, 0))
    return pl.pallas_call(
        _qkv_body,
        grid=(m // tm,),
        in_specs=[blk(col0), blk(col0 + 1), blk(col0 + 2), vec, vec],
        out_specs=[out] * 5,
        out_shape=[jax.ShapeDtypeStruct((m, D_SB), d) for d in (BF16, F32, BF16, F32, BF16)],
        compiler_params=_params("parallel"),
        name="qkv_norm",
    )(proj, proj, proj, gq, gk)


def _neg_abs(x):
    return lax.bitcast_convert_type(lax.bitcast_convert_type(x, jnp.int32) | jnp.int32(-2 ** 31), F32)


def _sb_step(q, k, v, tri, c_ref, crows, acc_ref, arows, acols, masked):
    z = lax.dot_general(q, k, (((1,), (1,)), ((), ())), preferred_element_type=F32)
    sp = jnp.maximum(z, 0.0) + jnp.log(1.0 + jnp.exp2(_neg_abs(z))) * LOG2E
    if masked:
        mask = (lax.broadcasted_iota(jnp.int32, z.shape, 1) < lax.broadcasted_iota(jnp.int32, z.shape, 0))
        sp = jnp.where(mask, sp, 0.0)
    inner = jnp.dot(sp.astype(BF16), tri, preferred_element_type=F32)
    c = c_ref[crows, :]
    w = jnp.exp2((z - sp) - (inner + c))
    if masked:
        w = jnp.where(mask, w, 0.0)
    acc_ref[arows, acols] += jnp.dot(w.astype(BF16), v, preferred_element_type=F32)
    c_ref[crows, :] = c + jnp.sum(sp, axis=-1, keepdims=True)


NEG_BIG = -1e30
SKIP_LOG2 = 160.0


def _sb_logits(q, k, z_ref, slot):
    z_ref[slot] = lax.dot_general(q, k, (((1,), (1,)), ((), ())), preferred_element_type=F32)


def _sb_scores(z_ref, c_ref, crows, t_ref, spb_ref, slot, mode):
    if mode == "none":
        t_ref[slot] = jnp.full(t_ref.shape[1:], NEG_BIG, F32)
        spb_ref[slot] = jnp.zeros(spb_ref.shape[1:], BF16)
        return
    z = z_ref[slot]
    sp = jnp.maximum(z, 0.0) + jnp.log(1.0 + jnp.exp2(_neg_abs(z))) * LOG2E
    c = c_ref[crows, :]
    t = (z - sp) - c
    if mode == "diag":
        mask = (lax.broadcasted_iota(jnp.int32, z.shape, 1) < lax.broadcasted_iota(jnp.int32, z.shape, 0))
        sp = jnp.where(mask, sp, 0.0)
        t = jnp.where(mask, t, NEG_BIG)
    t_ref[slot] = t
    spb_ref[slot] = sp.astype(BF16)
    c_ref[crows, :] = c + jnp.sum(sp, axis=-1, keepdims=True)


def _sb_weights(tri, t_ref, spb_ref, wb_ref, slot):
    inner = jnp.dot(spb_ref[slot], tri, preferred_element_type=F32)
    wb_ref[slot] = jnp.exp2(t_ref[slot] - inner).astype(BF16)


def _sb_values(v, wb_ref, slot, acc_ref, arows):
    acc_ref[arows, :] += jnp.dot(wb_ref[slot], v, preferred_element_type=F32)


def _attn_prompt_body(q_ref, k_ref, v_ref, tri_ref, o_ref, c_ref, z_ref, t_ref, spb_ref, wb_ref, *, nsub):
    ts = SB_TILE
    base = pl.program_id(1) * nsub
    tri = tri_ref[...]
    c_ref[...] = jnp.zeros_like(c_ref)
    o_ref[...] = jnp.zeros_like(o_ref)
    rows = [pl.ds(s * ts, ts) for s in range(nsub)]
    qs = [q_ref[r, :] for r in rows]
    cols = slice(None)

    @pl.when(base == 0)
    def _():
        for t in reversed(range(nsub)):
            k = k_ref[t * ts:(t + 1) * ts, :]
            v = v_ref[t * ts:(t + 1) * ts, :]
            for s in range(t, nsub):
                _sb_step(qs[s], k, v, tri, c_ref, rows[s], o_ref, rows[s], cols, masked=(s == t))

    @pl.when(base > 0)
    def _():
        n = base + nsub

        def tile(i):
            return pl.ds(pl.multiple_of((n - 1 - i) * ts, ts), ts)

        def modes(i):
            kt = nsub - 1 - i
            return tuple("full" if (kt < 0 or s > kt) else ("diag" if s == kt else "none")
                         for s in range(nsub))

        def iteration(i, p, first=1, last=4, static_i=None):
            if last >= 4 and first <= 4:
                v = v_ref[tile(i - 3), :]
                for s in range(nsub):
                    _sb_values(v, wb_ref.at[1 - p], s, o_ref, rows[s])
            if last >= 3 and first <= 3:
                for s in range(nsub):
                    _sb_weights(tri, t_ref.at[1 - p], spb_ref.at[1 - p], wb_ref.at[p], s)
            if last >= 2 and first <= 2:
                md = modes(static_i - 1) if static_i is not None else ("full",) * nsub
                for s in range(nsub):
                    _sb_scores(z_ref.at[1 - p], c_ref, rows[s], t_ref.at[p], spb_ref.at[p], s, md[s])
            if last >= 1 and first <= 1:
                k = k_ref[tile(i), :]
                for s in range(nsub):
                    _sb_logits(qs[s], k, z_ref.at[p], s)

        nfill = 4
        assert nsub == 2
        for i in range(nfill):
            iteration(i, i & 1, last=min(i + 1, 4), static_i=i)

        def cond(carry):
            j, cmin = carry
            return jnp.logical_and(j < (n - nfill) // 2, cmin < SKIP_LOG2)

        def body(carry):
            j, _ = carry
            i = nfill + 2 * j
            iteration(i, 0)
            iteration(i + 1, 1)
            return j + 1, jnp.min(c_ref[...])

        trips, _ = lax.while_loop(cond, body, (jnp.int32(0), jnp.min(c_ref[...])))
        issued = nfill + 2 * trips
        for d in range(3):
            iteration(issued + d, d & 1, first=d + 2)


def _attn_prompt(qb, kb, vb, tri, nsub):
    t = qb.shape[0]
    tq = nsub * SB_TILE
    return pl.pallas_call(
        functools.partial(_attn_prompt_body, nsub=nsub),
        grid=(N_HEADS, t // tq),
        in_specs=[
            pl.BlockSpec((tq, HEAD_DIM), lambda h, i: (i, h)),
            pl.BlockSpec((t, HEAD_DIM), lambda h, i: (0, h)),
            pl.BlockSpec((t, HEAD_DIM), lambda h, i: (0, h)),
            pl.BlockSpec((SB_TILE, SB_TILE), lambda h, i: (0, 0)),
        ],
        out_specs=pl.BlockSpec((tq, HEAD_DIM), lambda h, i: (i, h)),
        out_shape=jax.ShapeDtypeStruct((t, D_SB), F32),
        scratch_shapes=[pltpu.VMEM((tq, 1), F32),
                        pltpu.VMEM((2, nsub, SB_TILE, SB_TILE), F32),
                        pltpu.VMEM((2, nsub, SB_TILE, SB_TILE), F32),
                        pltpu.VMEM((2, nsub, SB_TILE, SB_TILE), BF16),
                        pltpu.VMEM((2, nsub, SB_TILE, SB_TILE), BF16)],
        compiler_params=_params("parallel", "arbitrary"),
        name="sb_attn_prompt",
    )(qb, kb, vb, tri)


def _attn_sample_body(q_ref, kn_ref, vn_ref, kc_ref, vc_ref, trin_ref, tri_ref, o_ref, c_ref, *, tq, chunk):
    ts = SB_TILE
    tri = tri_ref[...]
    rows = [pl.ds(h * tq, tq) for h in range(N_HEADS)]
    cols = [slice(h * HEAD_DIM, (h + 1) * HEAD_DIM) for h in range(N_HEADS)]
    qs = [q_ref[:, cl] for cl in cols]

    @pl.when(pl.program_id(1) == 0)
    def _():
        c_ref[...] = jnp.zeros_like(c_ref)
        o_ref[...] = jnp.zeros_like(o_ref)
        trin = trin_ref[...]
        for h in range(N_HEADS):
            _sb_step(qs[h], kn_ref[:, cols[h]], vn_ref[:, cols[h]], trin, c_ref, rows[h], o_ref, slice(None), cols[h], True)

    @pl.when(jnp.min(c_ref[...]) < SKIP_LOG2)
    def _():
        for t in reversed(range(chunk // ts)):
            for h in range(N_HEADS):
                sel = pl.ds(t * ts * N_HEADS + h, ts, stride=N_HEADS)
                k = kc_ref[sel, :].astype(BF16)
                v = vc_ref[sel, :].astype(BF16)
                _sb_step(qs[h], k, v, tri, c_ref, rows[h], o_ref, slice(None), cols[h], False)


def _attn_sample(qb, kb, vb, kc, vc, tri_new, tri, nseq, tq, chunk):
    nchunk = kc.shape[1] // (chunk * N_HEADS)
    new = pl.BlockSpec((tq, D_SB), lambda b, c: (b, 0))
    cache = pl.BlockSpec((None, chunk * N_HEADS, HEAD_DIM), lambda b, c: (b, nchunk - 1 - c, 0))
    return pl.pallas_call(
        functools.partial(_attn_sample_body, tq=tq, chunk=chunk),
        grid=(nseq, nchunk),
        in_specs=[new, new, new, cache, cache,
                  pl.BlockSpec((tq, tq), lambda b, c: (0, 0)),
                  pl.BlockSpec((SB_TILE, SB_TILE), lambda b, c: (0, 0))],
        out_specs=new,
        out_shape=jax.ShapeDtypeStruct((nseq * tq, D_SB), F32),
        scratch_shapes=[pltpu.VMEM((N_HEADS * tq, 1), F32)],
        compiler_params=_params("parallel", "arbitrary"),
        name="sb_attn_sample",
    )(qb, kb, vb, kc, vc, tri_new, tri)


def _ffn_conv_act(pre, val, ext_ref, cw_ref, cb_ref, tt, cols=slice(None)):
    ext_ref[HALO:HALO + tt, cols] = pre
    cw = cw_ref[:, cols]
    pc = cb_ref[:, cols]
    for k in range(FFN_CONV - 1):
        lo = HALO - (FFN_CONV - 1) + k
        pc = pc + ext_ref[lo:lo + tt, cols] * cw[k:k + 1, :]
    pc = pc + pre * cw[FFN_CONV - 1:FFN_CONV, :]
    return (_gelu(pc) * val).astype(BF16), ext_ref[tt:tt + HALO, cols]


def _ffn_act_body(val_ref, pre_ref, sprev_ref, cw_ref, cb_ref, g_ref, sout_ref, ext_ref, *, tt):
    @pl.when(pl.program_id(2) == 0)
    def _():
        ext_ref[0:HALO, :] = sprev_ref[0]

    g, last = _ffn_conv_act(pre_ref[...], val_ref[...], ext_ref, cw_ref, cb_ref, tt)
    g_ref[...] = g
    sout_ref[0] = last
    ext_ref[0:HALO, :] = last


def _ffn_act(up, sprev, cw, cb, nseq, seqlen, tt, tc):
    m = nseq * seqlen
    nt = seqlen // tt
    nc = D_FF // tc
    return pl.pallas_call(
        functools.partial(_ffn_act_body, tt=tt),
        grid=(nseq, nc, nt),
        in_specs=[
            pl.BlockSpec((tt, tc), lambda b, c, t: (b * nt + t, c)),
            pl.BlockSpec((tt, tc), lambda b, c, t: (b * nt + t, nc + c)),
            pl.BlockSpec((1, HALO, tc), lambda b, c, t: (b, 0, c)),
            pl.BlockSpec((HALO, tc), lambda b, c, t: (0, c)),
            pl.BlockSpec((1, tc), lambda b, c, t: (0, c)),
        ],
        out_specs=[
            pl.BlockSpec((tt, tc), lambda b, c, t: (b * nt + t, c)),
            pl.BlockSpec((1, HALO, tc), lambda b, c, t: (b, 0, c)),
        ],
        out_shape=[
            jax.ShapeDtypeStruct((m, D_FF), BF16),
            jax.ShapeDtypeStruct((nseq, HALO, D_FF), F32),
        ],
        scratch_shapes=[pltpu.VMEM((tt + HALO, tc), F32)],
        compiler_params=_params("parallel", "parallel", "arbitrary"),
        name="ffn_act",
    )(up, up, sprev, cw, cb)


def _ffn_up_act_body(x_ref, g_ref, wv_ref, wp_ref, sprev_ref, cw_ref, cb_ref, o_ref, last_ref,
                     hn_ref, halo_ref, ext_ref, *, tm):
    i = pl.program_id(0)
    j = pl.program_id(1)

    @pl.when(j == 0)
    def _():
        _rms_rows(x_ref, g_ref, hn_ref)

    @pl.when(i == 0)
    def _():
        ext_ref[0:HALO, :] = sprev_ref[...]

    @pl.when(i > 0)
    def _():
        ext_ref[0:HALO, :] = halo_ref[j]

    hn = hn_ref[...]
    tn = o_ref.shape[1]
    halves = [slice(h * MXU_COLS, (h + 1) * MXU_COLS) for h in range(tn // MXU_COLS)]
    pres = [jnp.dot(hn, wp_ref[:, cs], preferred_element_type=F32) for cs in halves]
    vals = [jnp.dot(hn, wv_ref[:, cs], preferred_element_type=F32) for cs in halves]
    for cs, pre, val in zip(halves, pres, vals):
        g, last = _ffn_conv_act(pre, val, ext_ref, cw_ref, cb_ref, tm, cs)
        o_ref[:, cs] = g
        halo_ref[j, :, cs] = last
        last_ref[0, :, cs] = last


def _ffn_up_act(x, g, w_up, sprev, cw, cb, tm, tn):
    m, k = x.shape
    nj = D_FF // tn
    return pl.pallas_call(
        functools.partial(_ffn_up_act_body, tm=tm),
        grid=(m // tm, nj),
        in_specs=[
            _row_tile_spec(tm, k),
            pl.BlockSpec((1, k), lambda i, j: (0, 0)),
            pl.BlockSpec((k, tn), lambda i, j: (0, j)),
            pl.BlockSpec((k, tn), lambda i, j: (0, nj + j)),
            pl.BlockSpec((HALO, tn), lambda i, j: (0, j)),
            pl.BlockSpec((HALO, tn), lambda i, j: (0, j)),
            pl.BlockSpec((1, tn), lambda i, j: (0, j)),
        ],
        out_specs=[
            pl.BlockSpec((tm, tn), lambda i, j: (i, j)),
            pl.BlockSpec((1, HALO, tn), lambda i, j: (i, 0, j)),
        ],
        out_shape=[
            jax.ShapeDtypeStruct((m, D_FF), BF16),
            jax.ShapeDtypeStruct((m // tm, HALO, D_FF), F32),
        ],
        scratch_shapes=[
            pltpu.VMEM((tm, k), BF16),
            pltpu.VMEM((nj, HALO, tn), F32),
            pltpu.VMEM((tm + HALO, tn), F32),
        ],
        compiler_params=_params("arbitrary", "arbitrary"),
        name="ffn_up_act",
    )(x, g, w_up, w_up, sprev, cw, cb)


def _front_pad(x, rows):
    return jnp.pad(x, ((0, 0), (rows - x.shape[1], 0), (0, 0)))


def _layer(x3, kpast, vpast, h0, cprev, fprev, wts, tri, tm, tt_lru, tt_ffn, attn_nsub=2, fuse_ffn=None,
           emit=False):
    nseq, seqlen, _ = x3.shape
    m = nseq * seqlen
    x = x3.reshape(m, D_MODEL)
    if fuse_ffn is None:
        fuse_ffn = nseq == 1
    assert not (emit and fuse_ffn)
    tn = 512 if emit else 1024
    tm_wide = ROW_TILE_SINGLE_BUFFER if m % ROW_TILE_SINGLE_BUFFER == 0 else tm
    wb = {}

    proj = _norm_matmul(x, wts["norm_mix_g"], wts["w_in"], tm_wide, 512, "in_proj", emit)
    if emit:
        proj, wb["w_in"] = proj
    y_lru, h_last, cstate = _lru(proj, h0[:, None, :], _front_pad(cprev, HALO), wts, nseq, seqlen, tt_lru)
    qb, kf, kb, vf, vb = _qkv(proj, wts["q_norm_g"], wts["k_norm_g"], tm)
    if kpast is None:
        y_sb = _attn_prompt(qb, kb, vb, tri, attn_nsub)
    else:
        past = kpast.shape[1]
        y_sb = _attn_sample(qb, kb, vb, kpast.reshape(nseq, past * N_HEADS, HEAD_DIM),
                            vpast.reshape(nseq, past * N_HEADS, HEAD_DIM),
                            tri[:seqlen, :seqlen], tri, nseq, seqlen, min(past, 1024))
    if emit:
        x1, wb["w_out_lru"], wb["w_out_sb"] = _outproj(y_lru, y_sb, wts["sb_out_g"], wts["w_out"], wts["w_out"],
                                                        x, tm, tn, emit)
    else:
        x1 = _outproj(y_lru, y_sb, wts["sb_out_g"], wts["w_out_lru"], wts["w_out_sb"], x, tm, tn)

    fprev8 = _front_pad(fprev, HALO)
    if fuse_ffn:
        g, lasts = _ffn_up_act(x1, wts["norm_ffn_g"], wts["w_ffn_up"], fprev8[0], wts["ffn_conv_w"],
                               wts["ffn_conv_b"], tm_wide, 512)
        fstate = lasts[-1:]
    else:
        up = _norm_matmul(x1, wts["norm_ffn_g"], wts["w_ffn_up"], tm, tn, "ffn_up", emit)
        if emit:
            up, wb["w_ffn_up"] = up
        g, fstate = _ffn_act(up, fprev8, wts["ffn_conv_w"], wts["ffn_conv_b"], nseq, seqlen, tt_ffn, 2048)
    out = _down(g, wts["w_ffn_down"], x1, min(m, 1024), 1024, 2048 if emit else 3072, emit)
    if emit:
        out, wb["w_ffn_down"] = out

    return (out.reshape(nseq, seqlen, D_MODEL),
            kf.reshape(nseq, seqlen, N_HEADS, HEAD_DIM),
            vf.reshape(nseq, seqlen, N_HEADS, HEAD_DIM),
            h_last[:, 0, :],
            cstate[:, HALO - (LRU_CONV - 1):, :],
            fstate[:, HALO - (FFN_CONV - 1):, :],
            wb)


def _pair_blocks(w):
    z = jnp.zeros((N_PAIRS, LRU_BLOCK_DIM, LRU_BLOCK_DIM), w.dtype)
    top = jnp.concatenate([w[0::2], z], axis=2)
    bot = jnp.concatenate([z, w[1::2]], axis=2)
    return jnp.concatenate([top, bot], axis=1).astype(BF16)


def _prep_weights(norm_mix_g, w_in, lru_conv_w, lru_conv_b, lru_w_a, lru_b_a, lru_w_x, lru_b_x,
                  lru_lambda, q_norm_g, k_norm_g, lru_out_g, sb_out_g, w_out, norm_ffn_g,
                  w_ffn_up, ffn_conv_w, ffn_conv_b, w_ffn_down):
    row = lambda v: v.reshape(1, -1)
    pad_rows = lambda w: jnp.pad(w, ((0, HALO - w.shape[0]), (0, 0)))
    return {
        "norm_mix_g": row(norm_mix_g), "w_in": w_in,
        "lru_conv_w": pad_rows(lru_conv_w), "lru_conv_b": row(lru_conv_b),
        "lru_w_a": _pair_blocks(lru_w_a), "lru_b_a": row(lru_b_a),
        "lru_w_x": _pair_blocks(lru_w_x), "lru_b_x": row(lru_b_x),
        "lru_lambda": row(lru_lambda),
        "q_norm_g": row(q_norm_g), "k_norm_g": row(k_norm_g),
        "lru_out_g": row(lru_out_g), "sb_out_g": row(sb_out_g), "w_out": w_out,
        "norm_ffn_g": row(norm_ffn_g), "w_ffn_up": w_ffn_up,
        "ffn_conv_w": pad_rows(ffn_conv_w), "ffn_conv_b": row(ffn_conv_b),
        "w_ffn_down": w_ffn_down,
    }


def kernel(x_prompt, x_sample, cache_sb_k, cache_sb_v, state_lru_h, state_lru_conv, state_ffn_conv, norm_mix_g, w_in, lru_conv_w, lru_conv_b, lru_w_a, lru_b_a, lru_w_x, lru_b_x, lru_lambda, q_norm_g, k_norm_g, lru_out_g, sb_out_g, w_out, norm_ffn_g, w_ffn_up, ffn_conv_w, ffn_conv_b, w_ffn_down):
    depth = w_in.shape[0]
    bp = x_prompt.shape[0]
    tri = jnp.tri(SB_TILE, k=-1, dtype=BF16)
    xp, xs = x_prompt, x_sample
    st_p, st_s = [], []
    for l in range(depth):
        wts = _prep_weights(norm_mix_g[l], w_in[l], lru_conv_w[l], lru_conv_b[l], lru_w_a[l], lru_b_a[l],
                            lru_w_x[l], lru_b_x[l], lru_lambda[l], q_norm_g[l], k_norm_g[l], lru_out_g[l],
                            sb_out_g[l], w_out[l], norm_ffn_g[l], w_ffn_up[l], ffn_conv_w[l],
                            ffn_conv_b[l], w_ffn_down[l])
        xs, *ss, wb = _layer(xs, cache_sb_k[l], cache_sb_v[l], state_lru_h[l], state_lru_conv[l],
                             state_ffn_conv[l], wts, tri, tm=xs.shape[0] * xs.shape[1],
                             tt_lru=xs.shape[1], tt_ffn=xs.shape[1], emit=True)
        st_s.append(ss)
        xp, *sp, _ = _layer(xp, None, None,
                            jnp.zeros((bp, D_LRU), F32),
                            jnp.zeros((bp, LRU_CONV - 1, D_LRU), F32),
                            jnp.zeros((bp, FFN_CONV - 1, D_FF), F32),
                            dict(wts, **wb), tri, tm=512, tt_lru=128, tt_ffn=256)
        st_p.append(sp)
    stack = lambda sts, i: jnp.stack([s[i] for s in sts])
    return (xp, xs) + tuple(stack(st_p, i) for i in range(5)) + tuple(stack(st_s, i) for i in range(5))
```

```python
import functools
import math

import jax
import jax.numpy as jnp
from jax import lax
from jax.experimental import pallas as pl
from jax.experimental.pallas import tpu as pltpu

F32 = jnp.float32
BF16 = jnp.bfloat16

D_MODEL = 4096
HEAD_DIM = 128
N_HEADS = 8
D_SB = N_HEADS * HEAD_DIM
D_LRU = D_MODEL - D_SB
LRU_BLOCKS = 16
LRU_BLOCK_DIM = D_LRU // LRU_BLOCKS
LRU_PAIR = 2 * LRU_BLOCK_DIM
N_PAIRS = LRU_BLOCKS // 2
LRU_CONV = 4
LRU_C = 8.0
D_FF = 3 * D_MODEL
FFN_CONV = 3
EPS = 1e-6
LOG2E = math.log2(math.e)
Q_SCALE = HEAD_DIM ** -0.5 * LOG2E
HALO = 8
SB_TILE = 256
SB_HEADS_PER_STEP = 2
MXU_COLS = 256
ROW_TILE_SINGLE_BUFFER = 1024

VMEM_LIMIT = 60 * 1024 * 1024


def _params(*sem):
    return pltpu.CompilerParams(dimension_semantics=sem, vmem_limit_bytes=VMEM_LIMIT)


def _rms(x, g):
    ms = jnp.mean(x * x, axis=-1, keepdims=True)
    return (x * lax.rsqrt(ms + EPS)) * g


def _rms_rows(x_ref, g_ref, o_ref, chunk=256):
    g = g_ref[...]
    for r in range(0, x_ref.shape[0], chunk):
        rows = slice(r, min(r + chunk, x_ref.shape[0]))
        o_ref[rows, :] = _rms(x_ref[rows, :], g).astype(BF16)


def _gelu(x):
    return x * (0.5 * (1.0 + jnp.tanh(0.7978845608028654 * (x + 0.044715 * (x * x * x)))))


def _sigmoid(x):
    return 1.0 / (1.0 + jnp.exp(-x))


def _softplus(x):
    return jnp.maximum(x, 0.0) + jnp.log1p(jnp.exp(-jnp.abs(x)))


def _row_tile_spec(tm, k):
    mode = {"pipeline_mode": pl.Buffered(1)} if tm >= ROW_TILE_SINGLE_BUFFER else {}
    return pl.BlockSpec((tm, k), lambda i, j: (i, 0), **mode)


def _weight_block(w_ref, wb_ref):
    w = w_ref[...]
    if wb_ref is not None:
        w = w.astype(BF16)
        wb_ref[...] = w
    return w


def _norm_matmul_body(x_ref, g_ref, w_ref, o_ref, *rest, emit):
    wb_ref, hn_ref = rest if emit else (None,) + rest

    @pl.when(pl.program_id(1) == 0)
    def _():
        _rms_rows(x_ref, g_ref, hn_ref)

    o_ref[...] = jnp.dot(hn_ref[...], _weight_block(w_ref, wb_ref), preferred_element_type=F32)


def _norm_matmul(x, g, w, tm, tn, name, emit=False):
    m, k = x.shape
    n = w.shape[1]
    assert not emit or m == tm
    wspec = pl.BlockSpec((k, tn), lambda i, j: (0, j))
    ospec = pl.BlockSpec((tm, tn), lambda i, j: (i, j))
    oshape = jax.ShapeDtypeStruct((m, n), F32)
    return pl.pallas_call(
        functools.partial(_norm_matmul_body, emit=emit),
        grid=(m // tm, n // tn),
        in_specs=[
            _row_tile_spec(tm, k),
            pl.BlockSpec((1, k), lambda i, j: (0, 0)),
            wspec,
        ],
        out_specs=[ospec, wspec] if emit else ospec,
        out_shape=[oshape, jax.ShapeDtypeStruct((k, n), BF16)] if emit else oshape,
        scratch_shapes=[pltpu.VMEM((tm, k), BF16)],
        compiler_params=_params("parallel", "arbitrary"),
        name=name,
    )(x, g, w)


def _outproj_body(a_ref, y_ref, gsb_ref, w1_ref, w2_ref, res_ref, o_ref, *rest, emit):
    w1b_ref, w2b_ref, yn_ref = rest if emit else (None, None) + rest

    @pl.when(pl.program_id(1) == 0)
    def _():
        _rms_rows(y_ref, gsb_ref, yn_ref)

    acc = jnp.dot(a_ref[...], _weight_block(w1_ref, w1b_ref), preferred_element_type=F32)
    acc = acc + jnp.dot(yn_ref[...], _weight_block(w2_ref, w2b_ref), preferred_element_type=F32)
    o_ref[...] = res_ref[...] + acc


def _outproj(a, y_sb, g_sb, w1, w2, res, tm, tn, emit=False):
    m = a.shape[0]
    n = w1.shape[1]
    assert not emit or m == tm
    w2_row_block = (w2.shape[0] - D_SB) // D_SB
    ospec = pl.BlockSpec((tm, tn), lambda i, j: (i, j))
    oshape = jax.ShapeDtypeStruct((m, n), F32)
    wbspecs = [pl.BlockSpec((D_LRU, tn), lambda i, j: (0, j)), pl.BlockSpec((D_SB, tn), lambda i, j: (0, j))]
    wbshapes = [jax.ShapeDtypeStruct((D_LRU, n), BF16), jax.ShapeDtypeStruct((D_SB, n), BF16)]
    return pl.pallas_call(
        functools.partial(_outproj_body, emit=emit),
        grid=(m // tm, n // tn),
        in_specs=[
            _row_tile_spec(tm, D_LRU),
            _row_tile_spec(tm, D_SB),
            pl.BlockSpec((1, D_SB), lambda i, j: (0, 0)),
            pl.BlockSpec((D_LRU, tn), lambda i, j: (0, j)),
            pl.BlockSpec((D_SB, tn), lambda i, j: (w2_row_block, j)),
            ospec,
        ],
        out_specs=[ospec] + wbspecs if emit else ospec,
        out_shape=[oshape] + wbshapes if emit else oshape,
        scratch_shapes=[pltpu.VMEM((tm, D_SB), BF16)],
        compiler_params=_params("parallel", "arbitrary"),
        name="out_proj",
    )(a, y_sb, g_sb, w1, w2, res)


def _down_body(g_ref, w_ref, res_ref, o_ref, *rest, emit):
    wb_ref, acc_ref = rest if emit else (None,) + rest
    k = pl.program_id(2)

    @pl.when(k == 0)
    def _():
        acc_ref[...] = jnp.zeros_like(acc_ref)

    acc_ref[...] += jnp.dot(g_ref[...], _weight_block(w_ref, wb_ref), preferred_element_type=F32)

    @pl.when(k == pl.num_programs(2) - 1)
    def _():
        o_ref[...] = res_ref[...] + acc_ref[...]


def _down(g, w, res, tm, tn, tk, emit=False):
    m, kdim = g.shape
    n = w.shape[1]
    assert not emit or m == tm
    wspec = pl.BlockSpec((tk, tn), lambda i, j, k: (k, j))
    ospec = pl.BlockSpec((tm, tn), lambda i, j, k: (i, j))
    oshape = jax.ShapeDtypeStruct((m, n), F32)
    return pl.pallas_call(
        functools.partial(_down_body, emit=emit),
        grid=(m // tm, n // tn, kdim // tk),
        in_specs=[pl.BlockSpec((tm, tk), lambda i, j, k: (i, k)), wspec, ospec],
        out_specs=[ospec, wspec] if emit else ospec,
        out_shape=[oshape, jax.ShapeDtypeStruct((kdim, n), BF16)] if emit else oshape,
        scratch_shapes=[pltpu.VMEM((tm, tn), F32)],
        compiler_params=_params("parallel", "parallel", "arbitrary"),
        name="ffn_down",
    )(g, w, res)


def _lru_body(u_ref, gate_ref, h0_ref, cprev_ref, cw_ref, cb_ref, wa_ref, ba_ref, wx_ref, bx_ref,
              lam_ref, og_ref, y_ref, hlast_ref, cstate_ref,
              ext_ref, uc_ref, a_ref, b_ref, hs_ref, hc_ref, *, tt):
    @pl.when(pl.program_id(1) == 0)
    def _():
        hc_ref[...] = h0_ref[0]
        ext_ref[0:HALO, :] = cprev_ref[0]

    u = u_ref[...]
    ext_ref[HALO:HALO + tt, :] = u
    cw = cw_ref[...]
    uc = cb_ref[...]
    for k in range(LRU_CONV - 1):
        lo = HALO - (LRU_CONV - 1) + k
        uc = uc + ext_ref[lo:lo + tt, :] * cw[k:k + 1, :]
    uc_ref[...] = uc + u * cw[LRU_CONV - 1:LRU_CONV, :]

    last = ext_ref[tt:tt + HALO, :]
    cstate_ref[0] = last
    ext_ref[0:HALO, :] = last

    for p in range(N_PAIRS):
        sl = slice(p * LRU_PAIR, (p + 1) * LRU_PAIR)
        ucp = uc_ref[:, sl]
        xb = ucp.astype(BF16)
        r = _sigmoid(jnp.dot(xb, wa_ref[p], preferred_element_type=F32) + ba_ref[:, sl])
        i = _sigmoid(jnp.dot(xb, wx_ref[p], preferred_element_type=F32) + bx_ref[:, sl])
        log_a = (LRU_C * r) * (-_softplus(-lam_ref[:, sl]))
        a = jnp.exp(log_a)
        a_ref[:, sl] = a
        b_ref[:, sl] = jnp.sqrt(-jnp.tanh(log_a) * (a * a + 1.0)) * (i * ucp)

    def step(t, h):
        h = a_ref[pl.ds(t, 1), :] * h + b_ref[pl.ds(t, 1), :]
        hs_ref[pl.ds(t, 1), :] = h
        return h

    h = lax.fori_loop(0, tt, step, hc_ref[...], unroll=8)
    hc_ref[...] = h
    hlast_ref[0] = h

    y = hs_ref[...] * _gelu(gate_ref[...])
    y_ref[...] = _rms(y, og_ref[...]).astype(BF16)


def _lru(proj, h0, cprev, wts, nseq, seqlen, tt):
    m = nseq * seqlen
    nt = seqlen // tt
    row = lambda b, t: (b * nt + t, 0)
    vec = lambda b, t: (0, 0)
    return pl.pallas_call(
        functools.partial(_lru_body, tt=tt),
        grid=(nseq, nt),
        in_specs=[
            pl.BlockSpec((tt, D_LRU), row),
            pl.BlockSpec((tt, D_LRU), lambda b, t: (b * nt + t, 1)),
            pl.BlockSpec((1, 1, D_LRU), lambda b, t: (b, 0, 0)),
            pl.BlockSpec((1, HALO, D_LRU), lambda b, t: (b, 0, 0)),
            pl.BlockSpec((HALO, D_LRU), vec),
            pl.BlockSpec((1, D_LRU), vec),
            pl.BlockSpec((N_PAIRS, LRU_PAIR, LRU_PAIR), lambda b, t: (0, 0, 0)),
            pl.BlockSpec((1, D_LRU), vec),
            pl.BlockSpec((N_PAIRS, LRU_PAIR, LRU_PAIR), lambda b, t: (0, 0, 0)),
            pl.BlockSpec((1, D_LRU), vec),
            pl.BlockSpec((1, D_LRU), vec),
            pl.BlockSpec((1, D_LRU), vec),
        ],
        out_specs=[
            pl.BlockSpec((tt, D_LRU), row),
            pl.BlockSpec((1, 1, D_LRU), lambda b, t: (b, 0, 0)),
            pl.BlockSpec((1, HALO, D_LRU), lambda b, t: (b, 0, 0)),
        ],
        out_shape=[
            jax.ShapeDtypeStruct((m, D_LRU), BF16),
            jax.ShapeDtypeStruct((nseq, 1, D_LRU), F32),
            jax.ShapeDtypeStruct((nseq, HALO, D_LRU), F32),
        ],
        scratch_shapes=[
            pltpu.VMEM((tt + HALO, D_LRU), F32),
            pltpu.VMEM((tt, D_LRU), F32),
            pltpu.VMEM((tt, D_LRU), F32),
            pltpu.VMEM((tt, D_LRU), F32),
            pltpu.VMEM((tt, D_LRU), F32),
            pltpu.VMEM((1, D_LRU), F32),
        ],
        compiler_params=_params("arbitrary", "arbitrary"),
        name="rg_lru",
    )(proj, proj, h0, cprev, wts["lru_conv_w"], wts["lru_conv_b"], wts["lru_w_a"], wts["lru_b_a"],
      wts["lru_w_x"], wts["lru_b_x"], wts["lru_lambda"], wts["lru_out_g"])


def _qkv_body(q_ref, k_ref, v_ref, gq_ref, gk_ref, qb_ref, kf_ref, kb_ref, vf_ref, vb_ref):
    for h in range(N_HEADS):
        sl = slice(h * HEAD_DIM, (h + 1) * HEAD_DIM)
        qb_ref[:, sl] = (_rms(q_ref[:, sl], gq_ref[...]) * Q_SCALE).astype(BF16)
        kn = _rms(k_ref[:, sl], gk_ref[...])
        kf_ref[:, sl] = kn
        kb_ref[:, sl] = kn.astype(BF16)
    v = v_ref[...]
    vf_ref[...] = v
    vb_ref[...] = v.astype(BF16)


def _qkv(proj, gq, gk, tm):
    m = proj.shape[0]
    col0 = 2 * D_LRU // D_SB
    blk = lambda c: pl.BlockSpec((tm, D_SB), lambda i: (i, c))
    vec = pl.BlockSpec((1, HEAD_DIM), lambda i: (0, 0))
    out = pl.BlockSpec((tm, D_SB), lambda i: (i, 0))
    return pl.pallas_call(
        _qkv_body,
        grid=(m // tm,),
        in_specs=[blk(col0), blk(col0 + 1), blk(col0 + 2), vec, vec],
        out_specs=[out] * 5,
        out_shape=[jax.ShapeDtypeStruct((m, D_SB), d) for d in (BF16, F32, BF16, F32, BF16)],
        compiler_params=_params("parallel"),
        name="qkv_norm",
    )(proj, proj, proj, gq, gk)


def _neg_abs(x):
    return lax.bitcast_convert_type(lax.bitcast_convert_type(x, jnp.int32) | jnp.int32(-2 ** 31), F32)


def _sb_step(q, k, v, tri, c_ref, crows, acc_ref, arows, acols, masked):
    z = lax.dot_general(q, k, (((1,), (1,)), ((), ())), preferred_element_type=F32)
    sp = jnp.maximum(z, 0.0) + jnp.log(1.0 + jnp.exp2(_neg_abs(z))) * LOG2E
    if masked:
        mask = (lax.broadcasted_iota(jnp.int32, z.shape, 1) < lax.broadcasted_iota(jnp.int32, z.shape, 0))
        sp = jnp.where(mask, sp, 0.0)
    inner = jnp.dot(sp.astype(BF16), tri, preferred_element_type=F32)
    c = c_ref[crows, :]
    w = jnp.exp2((z - sp) - (inner + c))
    if masked:
        w = jnp.where(mask, w, 0.0)
    acc_ref[arows, acols] += jnp.dot(w.astype(BF16), v, preferred_element_type=F32)
    c_ref[crows, :] = c + jnp.sum(sp, axis=-1, keepdims=True)


NEG_BIG = -1e30
SKIP_LOG2 = 160.0


def _sb_logits(q, k, z_ref, slot):
    z_ref[slot] = lax.dot_general(q, k, (((1,), (1,)), ((), ())), preferred_element_type=F32)


def _sb_scores(z_ref, c_ref, crows, t_ref, spb_ref, slot, mode):
    if mode == "none":
        t_ref[slot] = jnp.full(t_ref.shape[1:], NEG_BIG, F32)
        spb_ref[slot] = jnp.zeros(spb_ref.shape[1:], BF16)
        return
    z = z_ref[slot]
    sp = jnp.maximum(z, 0.0) + jnp.log(1.0 + jnp.exp2(_neg_abs(z))) * LOG2E
    c = c_ref[crows, :]
    t = (z - sp) - c
    if mode == "diag":
        mask = (lax.broadcasted_iota(jnp.int32, z.shape, 1) < lax.broadcasted_iota(jnp.int32, z.shape, 0))
        sp = jnp.where(mask, sp, 0.0)
        t = jnp.where(mask, t, NEG_BIG)
    t_ref[slot] = t
    spb_ref[slot] = sp.astype(BF16)
    c_ref[crows, :] = c + jnp.sum(sp, axis=-1, keepdims=True)


def _sb_weights(tri, t_ref, spb_ref, wb_ref, slot):
    inner = jnp.dot(spb_ref[slot], tri, preferred_element_type=F32)
    wb_ref[slot] = jnp.exp2(t_ref[slot] - inner).astype(BF16)


def _sb_values(v, wb_ref, slot, acc_ref, arows, acols):
    acc_ref[arows, acols] += jnp.dot(wb_ref[slot], v, preferred_element_type=F32)


def _attn_prompt_body(q_ref, k_ref, v_ref, tri_ref, o_ref, c_ref, z_ref, t_ref, spb_ref, wb_ref, *, nsub, nhead):
    ts = SB_TILE
    base = pl.program_id(1) * nsub
    tri = tri_ref[...]
    c_ref[...] = jnp.zeros_like(c_ref)
    o_ref[...] = jnp.zeros_like(o_ref)
    rows = [pl.ds(s * ts, ts) for s in range(nsub)]
    hcols = [slice(h * HEAD_DIM, (h + 1) * HEAD_DIM) for h in range(nhead)]
    chains = [(h, s) for h in range(nhead) for s in range(nsub)]
    crows = [pl.ds(u * ts, ts) for u in range(len(chains))]
    qs = [q_ref[rows[s], hcols[h]] for h, s in chains]

    @pl.when(base == 0)
    def _():
        for t in reversed(range(nsub)):
            for u, (h, s) in enumerate(chains):
                if s >= t:
                    k = k_ref[t * ts:(t + 1) * ts, hcols[h]]
                    v = v_ref[t * ts:(t + 1) * ts, hcols[h]]
                    _sb_step(qs[u], k, v, tri, c_ref, crows[u], o_ref, rows[s], hcols[h], masked=(s == t))

    @pl.when(base > 0)
    def _():
        n = base + nsub

        def tile(i):
            return pl.ds(pl.multiple_of((n - 1 - i) * ts, ts), ts)

        def modes(i):
            kt = nsub - 1 - i
            return tuple("full" if (kt < 0 or s > kt) else ("diag" if s == kt else "none")
                         for s in range(nsub))

        def iteration(i, p, first=1, last=4, static_i=None):
            if last >= 4 and first <= 4:
                vs = [v_ref[tile(i - 3), hc] for hc in hcols]
                for u, (h, s) in enumerate(chains):
                    _sb_values(vs[h], wb_ref.at[1 - p], u, o_ref, rows[s], hcols[h])
            if last >= 3 and first <= 3:
                for u in range(len(chains)):
                    _sb_weights(tri, t_ref.at[1 - p], spb_ref.at[1 - p], wb_ref.at[p], u)
            if last >= 2 and first <= 2:
                md = modes(static_i - 1) if static_i is not None else ("full",) * nsub
                for u, (h, s) in enumerate(chains):
                    _sb_scores(z_ref.at[1 - p], c_ref, crows[u], t_ref.at[p], spb_ref.at[p], u, md[s])
            if last >= 1 and first <= 1:
                ks = [k_ref[tile(i), hc] for hc in hcols]
                for u, (h, s) in enumerate(chains):
                    _sb_logits(qs[u], ks[h], z_ref.at[p], u)

        nfill = 4
        assert nsub == 2
        for i in range(nfill):
            iteration(i, i & 1, last=min(i + 1, 4), static_i=i)

        def cond(carry):
            j, cmin = carry
            return jnp.logical_and(j < (n - nfill) // 2, cmin < SKIP_LOG2)

        def body(carry):
            j, _ = carry
            i = nfill + 2 * j
            iteration(i, 0)
            iteration(i + 1, 1)
            return j + 1, jnp.min(c_ref[...])

        trips, _ = lax.while_loop(cond, body, (jnp.int32(0), jnp.min(c_ref[...])))
        issued = nfill + 2 * trips
        for d in range(3):
            iteration(issued + d, d & 1, first=d + 2)


def _attn_prompt(qb, kb, vb, tri, nsub):
    t = qb.shape[0]
    tq = nsub * SB_TILE
    nhead = SB_HEADS_PER_STEP
    hw = nhead * HEAD_DIM
    nchain = nhead * nsub
    return pl.pallas_call(
        functools.partial(_attn_prompt_body, nsub=nsub, nhead=nhead),
        grid=(N_HEADS // nhead, t // tq),
        in_specs=[
            pl.BlockSpec((tq, hw), lambda h, i: (i, h)),
            pl.BlockSpec((t, hw), lambda h, i: (0, h)),
            pl.BlockSpec((t, hw), lambda h, i: (0, h)),
            pl.BlockSpec((SB_TILE, SB_TILE), lambda h, i: (0, 0)),
        ],
        out_specs=pl.BlockSpec((tq, hw), lambda h, i: (i, h)),
        out_shape=jax.ShapeDtypeStruct((t, D_SB), F32),
        scratch_shapes=[pltpu.VMEM((nchain * SB_TILE, 1), F32),
                        pltpu.VMEM((2, nchain, SB_TILE, SB_TILE), F32),
                        pltpu.VMEM((2, nchain, SB_TILE, SB_TILE), F32),
                        pltpu.VMEM((2, nchain, SB_TILE, SB_TILE), BF16),
                        pltpu.VMEM((2, nchain, SB_TILE, SB_TILE), BF16)],
        compiler_params=_params("parallel", "arbitrary"),
        name="sb_attn_prompt",
    )(qb, kb, vb, tri)


def _attn_sample_body(q_ref, kn_ref, vn_ref, kc_ref, vc_ref, trin_ref, tri_ref, o_ref, c_ref, *, tq, chunk):
    ts = SB_TILE
    tri = tri_ref[...]
    rows = [pl.ds(h * tq, tq) for h in range(N_HEADS)]
    cols = [slice(h * HEAD_DIM, (h + 1) * HEAD_DIM) for h in range(N_HEADS)]
    qs = [q_ref[:, cl] for cl in cols]

    @pl.when(pl.program_id(1) == 0)
    def _():
        c_ref[...] = jnp.zeros_like(c_ref)
        o_ref[...] = jnp.zeros_like(o_ref)
        trin = trin_ref[...]
        for h in range(N_HEADS):
            _sb_step(qs[h], kn_ref[:, cols[h]], vn_ref[:, cols[h]], trin, c_ref, rows[h], o_ref, slice(None), cols[h], True)

    @pl.when(jnp.min(c_ref[...]) < SKIP_LOG2)
    def _():
        for t in reversed(range(chunk // ts)):
            for h in range(N_HEADS):
                sel = pl.ds(t * ts * N_HEADS + h, ts, stride=N_HEADS)
                k = kc_ref[sel, :].astype(BF16)
                v = vc_ref[sel, :].astype(BF16)
                _sb_step(qs[h], k, v, tri, c_ref, rows[h], o_ref, slice(None), cols[h], False)


def _attn_sample(qb, kb, vb, kc, vc, tri_new, tri, nseq, tq, chunk):
    nchunk = kc.shape[1] // (chunk * N_HEADS)
    new = pl.BlockSpec((tq, D_SB), lambda b, c: (b, 0))
    cache = pl.BlockSpec((None, chunk * N_HEADS, HEAD_DIM), lambda b, c: (b, nchunk - 1 - c, 0))
    return pl.pallas_call(
        functools.partial(_attn_sample_body, tq=tq, chunk=chunk),
        grid=(nseq, nchunk),
        in_specs=[new, new, new, cache, cache,
                  pl.BlockSpec((tq, tq), lambda b, c: (0, 0)),
                  pl.BlockSpec((SB_TILE, SB_TILE), lambda b, c: (0, 0))],
        out_specs=new,
        out_shape=jax.ShapeDtypeStruct((nseq * tq, D_SB), F32),
        scratch_shapes=[pltpu.VMEM((N_HEADS * tq, 1), F32)],
        compiler_params=_params("parallel", "arbitrary"),
        name="sb_attn_sample",
    )(qb, kb, vb, kc, vc, tri_new, tri)


def _ffn_conv_act(pre, val, ext_ref, cw_ref, cb_ref, tt, cols=slice(None)):
    ext_ref[HALO:HALO + tt, cols] = pre
    cw = cw_ref[:, cols]
    pc = cb_ref[:, cols]
    for k in range(FFN_CONV - 1):
        lo = HALO - (FFN_CONV - 1) + k
        pc = pc + ext_ref[lo:lo + tt, cols] * cw[k:k + 1, :]
    pc = pc + pre * cw[FFN_CONV - 1:FFN_CONV, :]
    return (_gelu(pc) * val).astype(BF16), ext_ref[tt:tt + HALO, cols]


def _ffn_act_body(val_ref, pre_ref, sprev_ref, cw_ref, cb_ref, g_ref, sout_ref, ext_ref, *, tt):
    @pl.when(pl.program_id(2) == 0)
    def _():
        ext_ref[0:HALO, :] = sprev_ref[0]

    g, last = _ffn_conv_act(pre_ref[...], val_ref[...], ext_ref, cw_ref, cb_ref, tt)
    g_ref[...] = g
    sout_ref[0] = last
    ext_ref[0:HALO, :] = last


def _ffn_act(up, sprev, cw, cb, nseq, seqlen, tt, tc):
    m = nseq * seqlen
    nt = seqlen // tt
    nc = D_FF // tc
    return pl.pallas_call(
        functools.partial(_ffn_act_body, tt=tt),
        grid=(nseq, nc, nt),
        in_specs=[
            pl.BlockSpec((tt, tc), lambda b, c, t: (b * nt + t, c)),
            pl.BlockSpec((tt, tc), lambda b, c, t: (b * nt + t, nc + c)),
            pl.BlockSpec((1, HALO, tc), lambda b, c, t: (b, 0, c)),
            pl.BlockSpec((HALO, tc), lambda b, c, t: (0, c)),
            pl.BlockSpec((1, tc), lambda b, c, t: (0, c)),
        ],
        out_specs=[
            pl.BlockSpec((tt, tc), lambda b, c, t: (b * nt + t, c)),
            pl.BlockSpec((1, HALO, tc), lambda b, c, t: (b, 0, c)),
        ],
        out_shape=[
            jax.ShapeDtypeStruct((m, D_FF), BF16),
            jax.ShapeDtypeStruct((nseq, HALO, D_FF), F32),
        ],
        scratch_shapes=[pltpu.VMEM((tt + HALO, tc), F32)],
        compiler_params=_params("parallel", "parallel", "arbitrary"),
        name="ffn_act",
    )(up, up, sprev, cw, cb)


def _ffn_up_act_body(x_ref, g_ref, wv_ref, wp_ref, sprev_ref, cw_ref, cb_ref, o_ref, last_ref,
                     hn_ref, halo_ref, ext_ref, *, tm):
    i = pl.program_id(0)
    j = pl.program_id(1)

    @pl.when(j == 0)
    def _():
        _rms_rows(x_ref, g_ref, hn_ref)

    @pl.when(i == 0)
    def _():
        ext_ref[0:HALO, :] = sprev_ref[...]

    @pl.when(i > 0)
    def _():
        ext_ref[0:HALO, :] = halo_ref[j]

    hn = hn_ref[...]
    tn = o_ref.shape[1]
    halves = [slice(h * MXU_COLS, (h + 1) * MXU_COLS) for h in range(tn // MXU_COLS)]
    pres = [jnp.dot(hn, wp_ref[:, cs], preferred_element_type=F32) for cs in halves]
    vals = [jnp.dot(hn, wv_ref[:, cs], preferred_element_type=F32) for cs in halves]
    for cs, pre, val in zip(halves, pres, vals):
        g, last = _ffn_conv_act(pre, val, ext_ref, cw_ref, cb_ref, tm, cs)
        o_ref[:, cs] = g
        halo_ref[j, :, cs] = last
        last_ref[0, :, cs] = last


def _ffn_up_act(x, g, w_up, sprev, cw, cb, tm, tn):
    m, k = x.shape
    nj = D_FF // tn
    return pl.pallas_call(
        functools.partial(_ffn_up_act_body, tm=tm),
        grid=(m // tm, nj),
        in_specs=[
            _row_tile_spec(tm, k),
            pl.BlockSpec((1, k), lambda i, j: (0, 0)),
            pl.BlockSpec((k, tn), lambda i, j: (0, j)),
            pl.BlockSpec((k, tn), lambda i, j: (0, nj + j)),
            pl.BlockSpec((HALO, tn), lambda i, j: (0, j)),
            pl.BlockSpec((HALO, tn), lambda i, j: (0, j)),
            pl.BlockSpec((1, tn), lambda i, j: (0, j)),
        ],
        out_specs=[
            pl.BlockSpec((tm, tn), lambda i, j: (i, j)),
            pl.BlockSpec((1, HALO, tn), lambda i, j: (i, 0, j)),
        ],
        out_shape=[
            jax.ShapeDtypeStruct((m, D_FF), BF16),
            jax.ShapeDtypeStruct((m // tm, HALO, D_FF), F32),
        ],
        scratch_shapes=[
            pltpu.VMEM((tm, k), BF16),
            pltpu.VMEM((nj, HALO, tn), F32),
            pltpu.VMEM((tm + HALO, tn), F32),
        ],
        compiler_params=_params("arbitrary", "arbitrary"),
        name="ffn_up_act",
    )(x, g, w_up, w_up, sprev, cw, cb)


def _front_pad(x, rows):
    return jnp.pad(x, ((0, 0), (rows - x.shape[1], 0), (0, 0)))


def _layer(x3, kpast, vpast, h0, cprev, fprev, wts, tri, tm, tt_lru, tt_ffn, attn_nsub=2, fuse_ffn=None,
           emit=False):
    nseq, seqlen, _ = x3.shape
    m = nseq * seqlen
    x = x3.reshape(m, D_MODEL)
    if fuse_ffn is None:
        fuse_ffn = nseq == 1
    assert not (emit and fuse_ffn)
    tn = 512 if emit else 1024
    tm_wide = ROW_TILE_SINGLE_BUFFER if m % ROW_TILE_SINGLE_BUFFER == 0 else tm
    wb = {}

    proj = _norm_matmul(x, wts["norm_mix_g"], wts["w_in"], tm_wide, 512, "in_proj", emit)
    if emit:
        proj, wb["w_in"] = proj
    y_lru, h_last, cstate = _lru(proj, h0[:, None, :], _front_pad(cprev, HALO), wts, nseq, seqlen, tt_lru)
    qb, kf, kb, vf, vb = _qkv(proj, wts["q_norm_g"], wts["k_norm_g"], tm)
    if kpast is None:
        y_sb = _attn_prompt(qb, kb, vb, tri, attn_nsub)
    else:
        past = kpast.shape[1]
        y_sb = _attn_sample(qb, kb, vb, kpast.reshape(nseq, past * N_HEADS, HEAD_DIM),
                            vpast.reshape(nseq, past * N_HEADS, HEAD_DIM),
                            tri[:seqlen, :seqlen], tri, nseq, seqlen, min(past, 1024))
    if emit:
        x1, wb["w_out_lru"], wb["w_out_sb"] = _outproj(y_lru, y_sb, wts["sb_out_g"], wts["w_out"], wts["w_out"],
                                                        x, tm, tn, emit)
    else:
        x1 = _outproj(y_lru, y_sb, wts["sb_out_g"], wts["w_out_lru"], wts["w_out_sb"], x, tm, tn)

    fprev8 = _front_pad(fprev, HALO)
    if fuse_ffn:
        g, lasts = _ffn_up_act(x1, wts["norm_ffn_g"], wts["w_ffn_up"], fprev8[0], wts["ffn_conv_w"],
                               wts["ffn_conv_b"], tm_wide, 512)
        fstate = lasts[-1:]
    else:
        up = _norm_matmul(x1, wts["norm_ffn_g"], wts["w_ffn_up"], tm, tn, "ffn_up", emit)
        if emit:
            up, wb["w_ffn_up"] = up
        g, fstate = _ffn_act(up, fprev8, wts["ffn_conv_w"], wts["ffn_conv_b"], nseq, seqlen, tt_ffn, 2048)
    out = _down(g, wts["w_ffn_down"], x1, min(m, 1024), 1024, 2048 if emit else 3072, emit)
    if emit:
        out, wb["w_ffn_down"] = out

    return (out.reshape(nseq, seqlen, D_MODEL),
            kf.reshape(nseq, seqlen, N_HEADS, HEAD_DIM),
            vf.reshape(nseq, seqlen, N_HEADS, HEAD_DIM),
            h_last[:, 0, :],
            cstate[:, HALO - (LRU_CONV - 1):, :],
            fstate[:, HALO - (FFN_CONV - 1):, :],
            wb)


def _pair_blocks(w):
    z = jnp.zeros((N_PAIRS, LRU_BLOCK_DIM, LRU_BLOCK_DIM), w.dtype)
    top = jnp.concatenate([w[0::2], z], axis=2)
    bot = jnp.concatenate([z, w[1::2]], axis=2)
    return jnp.concatenate([top, bot], axis=1).astype(BF16)


def _prep_weights(norm_mix_g, w_in, lru_conv_w, lru_conv_b, lru_w_a, lru_b_a, lru_w_x, lru_b_x,
                  lru_lambda, q_norm_g, k_norm_g, lru_out_g, sb_out_g, w_out, norm_ffn_g,
                  w_ffn_up, ffn_conv_w, ffn_conv_b, w_ffn_down):
    row = lambda v: v.reshape(1, -1)
    pad_rows = lambda w: jnp.pad(w, ((0, HALO - w.shape[0]), (0, 0)))
    return {
        "norm_mix_g": row(norm_mix_g), "w_in": w_in,
        "lru_conv_w": pad_rows(lru_conv_w), "lru_conv_b": row(lru_conv_b),
        "lru_w_a": _pair_blocks(lru_w_a), "lru_b_a": row(lru_b_a),
        "lru_w_x": _pair_blocks(lru_w_x), "lru_b_x": row(lru_b_x),
        "lru_lambda": row(lru_lambda),
        "q_norm_g": row(q_norm_g), "k_norm_g": row(k_norm_g),
        "lru_out_g": row(lru_out_g), "sb_out_g": row(sb_out_g), "w_out": w_out,
        "norm_ffn_g": row(norm_ffn_g), "w_ffn_up": w_ffn_up,
        "ffn_conv_w": pad_rows(ffn_conv_w), "ffn_conv_b": row(ffn_conv_b),
        "w_ffn_down": w_ffn_down,
    }


def kernel(x_prompt, x_sample, cache_sb_k, cache_sb_v, state_lru_h, state_lru_conv, state_ffn_conv, norm_mix_g, w_in, lru_conv_w, lru_conv_b, lru_w_a, lru_b_a, lru_w_x, lru_b_x, lru_lambda, q_norm_g, k_norm_g, lru_out_g, sb_out_g, w_out, norm_ffn_g, w_ffn_up, ffn_conv_w, ffn_conv_b, w_ffn_down):
    depth = w_in.shape[0]
    bp = x_prompt.shape[0]
    tri = jnp.tri(SB_TILE, k=-1, dtype=BF16)
    xp, xs = x_prompt, x_sample
    st_p, st_s = [], []
    for l in range(depth):
        wts = _prep_weights(norm_mix_g[l], w_in[l], lru_conv_w[l], lru_conv_b[l], lru_w_a[l], lru_b_a[l],
                            lru_w_x[l], lru_b_x[l], lru_lambda[l], q_norm_g[l], k_norm_g[l], lru_out_g[l],
                            sb_out_g[l], w_out[l], norm_ffn_g[l], w_ffn_up[l], ffn_conv_w[l],
                            ffn_conv_b[l], w_ffn_down[l])
        xs, *ss, wb = _layer(xs, cache_sb_k[l], cache_sb_v[l], state_lru_h[l], state_lru_conv[l],
                             state_ffn_conv[l], wts, tri, tm=xs.shape[0] * xs.shape[1],
                             tt_lru=xs.shape[1], tt_ffn=xs.shape[1], emit=True)
        st_s.append(ss)
        xp, *sp, _ = _layer(xp, None, None,
                            jnp.zeros((bp, D_LRU), F32),
                            jnp.zeros((bp, LRU_CONV - 1, D_LRU), F32),
                            jnp.zeros((bp, FFN_CONV - 1, D_FF), F32),
                            dict(wts, **wb), tri, tm=512, tt_lru=128, tt_ffn=256)
        st_p.append(sp)
    stack = lambda sts, i: jnp.stack([s[i] for s in sts])
    return (xp, xs) + tuple(stack(st_p, i) for i in range(5)) + tuple(stack(st_s, i) for i in range(5))
```

```python
import functools
import math

import jax
import jax.numpy as jnp
from jax import lax
from jax.experimental import pallas as pl
from jax.experimental.pallas import tpu as pltpu

F32 = jnp.float32
BF16 = jnp.bfloat16

D_MODEL = 4096
HEAD_DIM = 128
N_HEADS = 8
D_SB = N_HEADS * HEAD_DIM
D_LRU = D_MODEL - D_SB
LRU_BLOCKS = 16
LRU_BLOCK_DIM = D_LRU // LRU_BLOCKS
LRU_PAIR = 2 * LRU_BLOCK_DIM
N_PAIRS = LRU_BLOCKS // 2
LRU_CONV = 4
LRU_C = 8.0
D_FF = 3 * D_MODEL
FFN_CONV = 3
EPS = 1e-6
LOG2E = math.log2(math.e)
Q_SCALE = HEAD_DIM ** -0.5 * LOG2E
HALO = 8
SB_TILE = 256
SB_HEADS_PER_STEP = 2
MXU_COLS = 256
ROW_TILE_PREFETCH = 1024

VMEM_LIMIT = 60 * 1024 * 1024


def _params(*sem):
    return pltpu.CompilerParams(dimension_semantics=sem, vmem_limit_bytes=VMEM_LIMIT)


def _rms(x, g):
    ms = jnp.mean(x * x, axis=-1, keepdims=True)
    return (x * lax.rsqrt(ms + EPS)) * g


def _rms_rows(x_ref, g_ref, o_ref, chunk=256):
    g = g_ref[...]
    for r in range(0, x_ref.shape[0], chunk):
        rows = slice(r, min(r + chunk, x_ref.shape[0]))
        o_ref[rows, :] = _rms(x_ref[rows, :], g).astype(BF16)


def _gelu(x):
    return x * (0.5 * (1.0 + jnp.tanh(0.7978845608028654 * (x + 0.044715 * (x * x * x)))))


def _sigmoid(x):
    return 1.0 / (1.0 + jnp.exp(-x))


def _softplus(x):
    return jnp.maximum(x, 0.0) + jnp.log1p(jnp.exp(-jnp.abs(x)))


def _weight_block(w_ref, wb_ref):
    w = w_ref[...]
    if wb_ref is not None:
        w = w.astype(BF16)
        wb_ref[...] = w
    return w


def _rows_copy(x_hbm, xbuf, sem, i):
    tm = xbuf.shape[0]
    return pltpu.make_async_copy(x_hbm.at[pl.ds(pl.multiple_of(i * tm, tm), tm), :], xbuf, sem)


def _norm_row_tile(x_ref, g_ref, hn_ref, prefetch):
    i, j = pl.program_id(0), pl.program_id(1)
    if prefetch is None:
        @pl.when(j == 0)
        def _():
            _rms_rows(x_ref, g_ref, hn_ref)
        return
    xbuf, sem = prefetch

    @pl.when(j == 0)
    def _():
        @pl.when(i == 0)
        def _():
            _rows_copy(x_ref, xbuf, sem, 0).start()

        _rows_copy(x_ref, xbuf, sem, i).wait()
        _rms_rows(xbuf, g_ref, hn_ref)

    @pl.when(jnp.logical_and(j == 1, i + 1 < pl.num_programs(0)))
    def _():
        _rows_copy(x_ref, xbuf, sem, i + 1).start()


def _rows_operand(tm, k, ncols):
    if tm >= ROW_TILE_PREFETCH and ncols >= 2:
        return pl.BlockSpec(memory_space=pl.ANY), [pltpu.VMEM((tm, k), F32), pltpu.SemaphoreType.DMA(())]
    return pl.BlockSpec((tm, k), lambda i, j: (i, 0)), []


def _norm_matmul_body(x_ref, g_ref, w_ref, o_ref, *rest, emit, prefetch):
    wb_ref, hn_ref, *pre = rest if emit else (None,) + rest
    _norm_row_tile(x_ref, g_ref, hn_ref, tuple(pre) if prefetch else None)
    o_ref[...] = jnp.dot(hn_ref[...], _weight_block(w_ref, wb_ref), preferred_element_type=F32)


def _norm_matmul(x, g, w, tm, tn, name, emit=False):
    m, k = x.shape
    n = w.shape[1]
    assert not emit or m == tm
    xspec, xscratch = _rows_operand(tm, k, n // tn)
    wspec = pl.BlockSpec((k, tn), lambda i, j: (0, j))
    ospec = pl.BlockSpec((tm, tn), lambda i, j: (i, j))
    oshape = jax.ShapeDtypeStruct((m, n), F32)
    return pl.pallas_call(
        functools.partial(_norm_matmul_body, emit=emit, prefetch=bool(xscratch)),
        grid=(m // tm, n // tn),
        in_specs=[
            xspec,
            pl.BlockSpec((1, k), lambda i, j: (0, 0)),
            wspec,
        ],
        out_specs=[ospec, wspec] if emit else ospec,
        out_shape=[oshape, jax.ShapeDtypeStruct((k, n), BF16)] if emit else oshape,
        scratch_shapes=[pltpu.VMEM((tm, k), BF16)] + xscratch,
        compiler_params=_params("arbitrary", "arbitrary"),
        name=name,
    )(x, g, w)


def _outproj_body(a_ref, y_ref, gsb_ref, w1_ref, w2_ref, res_ref, o_ref, *rest, emit):
    w1b_ref, w2b_ref, yn_ref = rest if emit else (None, None) + rest

    @pl.when(pl.program_id(1) == 0)
    def _():
        _rms_rows(y_ref, gsb_ref, yn_ref)

    acc = jnp.dot(a_ref[...], _weight_block(w1_ref, w1b_ref), preferred_element_type=F32)
    acc = acc + jnp.dot(yn_ref[...], _weight_block(w2_ref, w2b_ref), preferred_element_type=F32)
    o_ref[...] = res_ref[...] + acc


def _outproj(a, y_sb, g_sb, w1, w2, res, tm, tn, emit=False):
    m = a.shape[0]
    n = w1.shape[1]
    assert not emit or m == tm
    w2_row_block = (w2.shape[0] - D_SB) // D_SB
    ospec = pl.BlockSpec((tm, tn), lambda i, j: (i, j))
    oshape = jax.ShapeDtypeStruct((m, n), F32)
    wbspecs = [pl.BlockSpec((D_LRU, tn), lambda i, j: (0, j)), pl.BlockSpec((D_SB, tn), lambda i, j: (0, j))]
    wbshapes = [jax.ShapeDtypeStruct((D_LRU, n), BF16), jax.ShapeDtypeStruct((D_SB, n), BF16)]
    return pl.pallas_call(
        functools.partial(_outproj_body, emit=emit),
        grid=(m // tm, n // tn),
        in_specs=[
            pl.BlockSpec((tm, D_LRU), lambda i, j: (i, 0)),
            pl.BlockSpec((tm, D_SB), lambda i, j: (i, 0)),
            pl.BlockSpec((1, D_SB), lambda i, j: (0, 0)),
            pl.BlockSpec((D_LRU, tn), lambda i, j: (0, j)),
            pl.BlockSpec((D_SB, tn), lambda i, j: (w2_row_block, j)),
            ospec,
        ],
        out_specs=[ospec] + wbspecs if emit else ospec,
        out_shape=[oshape] + wbshapes if emit else oshape,
        scratch_shapes=[pltpu.VMEM((tm, D_SB), BF16)],
        compiler_params=_params("parallel", "arbitrary"),
        name="out_proj",
    )(a, y_sb, g_sb, w1, w2, res)


def _down_body(g_ref, w_ref, res_ref, o_ref, *rest, emit):
    wb_ref, acc_ref = rest if emit else (None,) + rest
    k = pl.program_id(2)

    @pl.when(k == 0)
    def _():
        acc_ref[...] = jnp.zeros_like(acc_ref)

    acc_ref[...] += jnp.dot(g_ref[...], _weight_block(w_ref, wb_ref), preferred_element_type=F32)

    @pl.when(k == pl.num_programs(2) - 1)
    def _():
        o_ref[...] = res_ref[...] + acc_ref[...]


def _down(g, w, res, tm, tn, tk, emit=False):
    m, kdim = g.shape
    n = w.shape[1]
    assert not emit or m == tm
    wspec = pl.BlockSpec((tk, tn), lambda i, j, k: (k, j))
    ospec = pl.BlockSpec((tm, tn), lambda i, j, k: (i, j))
    oshape = jax.ShapeDtypeStruct((m, n), F32)
    return pl.pallas_call(
        functools.partial(_down_body, emit=emit),
        grid=(m // tm, n // tn, kdim // tk),
        in_specs=[pl.BlockSpec((tm, tk), lambda i, j, k: (i, k)), wspec, ospec],
        out_specs=[ospec, wspec] if emit else ospec,
        out_shape=[oshape, jax.ShapeDtypeStruct((kdim, n), BF16)] if emit else oshape,
        scratch_shapes=[pltpu.VMEM((tm, tn), F32)],
        compiler_params=_params("parallel", "parallel", "arbitrary"),
        name="ffn_down",
    )(g, w, res)


def _lru_body(u_ref, gate_ref, h0_ref, cprev_ref, cw_ref, cb_ref, wa_ref, ba_ref, wx_ref, bx_ref,
              lam_ref, og_ref, y_ref, hlast_ref, cstate_ref,
              ext_ref, uc_ref, a_ref, b_ref, hs_ref, hc_ref, *, tt):
    @pl.when(pl.program_id(1) == 0)
    def _():
        hc_ref[...] = h0_ref[0]
        ext_ref[0:HALO, :] = cprev_ref[0]

    u = u_ref[...]
    ext_ref[HALO:HALO + tt, :] = u
    cw = cw_ref[...]
    uc = cb_ref[...]
    for k in range(LRU_CONV - 1):
        lo = HALO - (LRU_CONV - 1) + k
        uc = uc + ext_ref[lo:lo + tt, :] * cw[k:k + 1, :]
    uc_ref[...] = uc + u * cw[LRU_CONV - 1:LRU_CONV, :]

    last = ext_ref[tt:tt + HALO, :]
    cstate_ref[0] = last
    ext_ref[0:HALO, :] = last

    for p in range(N_PAIRS):
        sl = slice(p * LRU_PAIR, (p + 1) * LRU_PAIR)
        ucp = uc_ref[:, sl]
        xb = ucp.astype(BF16)
        r = _sigmoid(jnp.dot(xb, wa_ref[p], preferred_element_type=F32) + ba_ref[:, sl])
        i = _sigmoid(jnp.dot(xb, wx_ref[p], preferred_element_type=F32) + bx_ref[:, sl])
        log_a = (LRU_C * r) * (-_softplus(-lam_ref[:, sl]))
        a = jnp.exp(log_a)
        a_ref[:, sl] = a
        b_ref[:, sl] = jnp.sqrt(-jnp.tanh(log_a) * (a * a + 1.0)) * (i * ucp)

    def step(t, h):
        h = a_ref[pl.ds(t, 1), :] * h + b_ref[pl.ds(t, 1), :]
        hs_ref[pl.ds(t, 1), :] = h
        return h

    h = lax.fori_loop(0, tt, step, hc_ref[...], unroll=8)
    hc_ref[...] = h
    hlast_ref[0] = h

    y = hs_ref[...] * _gelu(gate_ref[...])
    y_ref[...] = _rms(y, og_ref[...]).astype(BF16)


def _lru(proj, h0, cprev, wts, nseq, seqlen, tt):
    m = nseq * seqlen
    nt = seqlen // tt
    row = lambda b, t: (b * nt + t, 0)
    vec = lambda b, t: (0, 0)
    return pl.pallas_call(
        functools.partial(_lru_body, tt=tt),
        grid=(nseq, nt),
        in_specs=[
            pl.BlockSpec((tt, D_LRU), row),
            pl.BlockSpec((tt, D_LRU), lambda b, t: (b * nt + t, 1)),
            pl.BlockSpec((1, 1, D_LRU), lambda b, t: (b, 0, 0)),
            pl.BlockSpec((1, HALO, D_LRU), lambda b, t: (b, 0, 0)),
            pl.BlockSpec((HALO, D_LRU), vec),
            pl.BlockSpec((1, D_LRU), vec),
            pl.BlockSpec((N_PAIRS, LRU_PAIR, LRU_PAIR), lambda b, t: (0, 0, 0)),
            pl.BlockSpec((1, D_LRU), vec),
            pl.BlockSpec((N_PAIRS, LRU_PAIR, LRU_PAIR), lambda b, t: (0, 0, 0)),
            pl.BlockSpec((1, D_LRU), vec),
            pl.BlockSpec((1, D_LRU), vec),
            pl.BlockSpec((1, D_LRU), vec),
        ],
        out_specs=[
            pl.BlockSpec((tt, D_LRU), row),
            pl.BlockSpec((1, 1, D_LRU), lambda b, t: (b, 0, 0)),
            pl.BlockSpec((1, HALO, D_LRU), lambda b, t: (b, 0, 0)),
        ],
        out_shape=[
            jax.ShapeDtypeStruct((m, D_LRU), BF16),
            jax.ShapeDtypeStruct((nseq, 1, D_LRU), F32),
            jax.ShapeDtypeStruct((nseq, HALO, D_LRU), F32),
        ],
        scratch_shapes=[
            pltpu.VMEM((tt + HALO, D_LRU), F32),
            pltpu.VMEM((tt, D_LRU), F32),
            pltpu.VMEM((tt, D_LRU), F32),
            pltpu.VMEM((tt, D_LRU), F32),
            pltpu.VMEM((tt, D_LRU), F32),
            pltpu.VMEM((1, D_LRU), F32),
        ],
        compiler_params=_params("arbitrary", "arbitrary"),
        name="rg_lru",
    )(proj, proj, h0, cprev, wts["lru_conv_w"], wts["lru_conv_b"], wts["lru_w_a"], wts["lru_b_a"],
      wts["lru_w_x"], wts["lru_b_x"], wts["lru_lambda"], wts["lru_out_g"])


def _qkv_body(q_ref, k_ref, v_ref, gq_ref, gk_ref, qb_ref, kf_ref, kb_ref, vf_ref, vb_ref):
    for h in range(N_HEADS):
        sl = slice(h * HEAD_DIM, (h + 1) * HEAD_DIM)
        qb_ref[:, sl] = (_rms(q_ref[:, sl], gq_ref[...]) * Q_SCALE).astype(BF16)
        kn = _rms(k_ref[:, sl], gk_ref[...])
        kf_ref[:, sl] = kn
        kb_ref[:, sl] = kn.astype(BF16)
    v = v_ref[...]
    vf_ref[...] = v
    vb_ref[...] = v.astype(BF16)


def _qkv(proj, gq, gk, tm):
    m = proj.shape[0]
    col0 = 2 * D_LRU // D_SB
    blk = lambda c: pl.BlockSpec((tm, D_SB), lambda i: (i, c))
    vec = pl.BlockSpec((1, HEAD_DIM), lambda i: (0, 0))
    out = pl.BlockSpec((tm, D_SB), lambda i: (i, 0))
    return pl.pallas_call(
        _qkv_body,
        grid=(m // tm,),
        in_specs=[blk(col0), blk(col0 + 1), blk(col0 + 2), vec, vec],
        out_specs=[out] * 5,
        out_shape=[jax.ShapeDtypeStruct((m, D_SB), d) for d in (BF16, F32, BF16, F32, BF16)],
        compiler_params=_params("parallel"),
        name="qkv_norm",
    )(proj, proj, proj, gq, gk)


def _neg_abs(x):
    return lax.bitcast_convert_type(lax.bitcast_convert_type(x, jnp.int32) | jnp.int32(-2 ** 31), F32)


def _sb_step(q, k, v, tri, c_ref, crows, acc_ref, arows, acols, masked):
    z = lax.dot_general(q, k, (((1,), (1,)), ((), ())), preferred_element_type=F32)
    sp = jnp.maximum(z, 0.0) + jnp.log(1.0 + jnp.exp2(_neg_abs(z))) * LOG2E
    if masked:
        mask = (lax.broadcasted_iota(jnp.int32, z.shape, 1) < lax.broadcasted_iota(jnp.int32, z.shape, 0))
        sp = jnp.where(mask, sp, 0.0)
    inner = jnp.dot(sp.astype(BF16), tri, preferred_element_type=F32)
    c = c_ref[crows, :]
    w = jnp.exp2((z - sp) - (inner + c))
    if masked:
        w = jnp.where(mask, w, 0.0)
    acc_ref[arows, acols] += jnp.dot(w.astype(BF16), v, preferred_element_type=F32)
    c_ref[crows, :] = c + jnp.sum(sp, axis=-1, keepdims=True)


NEG_BIG = -1e30
SKIP_LOG2 = 160.0


def _sb_logits(q, k, z_ref, slot):
    z_ref[slot] = lax.dot_general(q, k, (((1,), (1,)), ((), ())), preferred_element_type=F32)


def _sb_scores(z_ref, c_ref, crows, t_ref, spb_ref, slot, mode):
    if mode == "none":
        t_ref[slot] = jnp.full(t_ref.shape[1:], NEG_BIG, F32)
        spb_ref[slot] = jnp.zeros(spb_ref.shape[1:], BF16)
        return
    z = z_ref[slot]
    sp = jnp.maximum(z, 0.0) + jnp.log(1.0 + jnp.exp2(_neg_abs(z))) * LOG2E
    c = c_ref[crows, :]
    t = (z - sp) - c
    if mode == "diag":
        mask = (lax.broadcasted_iota(jnp.int32, z.shape, 1) < lax.broadcasted_iota(jnp.int32, z.shape, 0))
        sp = jnp.where(mask, sp, 0.0)
        t = jnp.where(mask, t, NEG_BIG)
    t_ref[slot] = t
    spb_ref[slot] = sp.astype(BF16)
    c_ref[crows, :] = c + jnp.sum(sp, axis=-1, keepdims=True)


def _sb_weights(tri, t_ref, spb_ref, wb_ref, slot):
    inner = jnp.dot(spb_ref[slot], tri, preferred_element_type=F32)
    wb_ref[slot] = jnp.exp2(t_ref[slot] - inner).astype(BF16)


def _sb_values(v, wb_ref, slot, acc_ref, arows, acols):
    acc_ref[arows, acols] += jnp.dot(wb_ref[slot], v, preferred_element_type=F32)


def _attn_prompt_body(q_ref, k_ref, v_ref, tri_ref, o_ref, c_ref, z_ref, t_ref, spb_ref, wb_ref, *, nsub, nhead):
    ts = SB_TILE
    base = pl.program_id(1) * nsub
    tri = tri_ref[...]
    c_ref[...] = jnp.zeros_like(c_ref)
    o_ref[...] = jnp.zeros_like(o_ref)
    rows = [pl.ds(s * ts, ts) for s in range(nsub)]
    hcols = [slice(h * HEAD_DIM, (h + 1) * HEAD_DIM) for h in range(nhead)]
    chains = [(h, s) for h in range(nhead) for s in range(nsub)]
    crows = [pl.ds(u * ts, ts) for u in range(len(chains))]
    qs = [q_ref[rows[s], hcols[h]] for h, s in chains]

    @pl.when(base == 0)
    def _():
        for t in reversed(range(nsub)):
            for u, (h, s) in enumerate(chains):
                if s >= t:
                    k = k_ref[t * ts:(t + 1) * ts, hcols[h]]
                    v = v_ref[t * ts:(t + 1) * ts, hcols[h]]
                    _sb_step(qs[u], k, v, tri, c_ref, crows[u], o_ref, rows[s], hcols[h], masked=(s == t))

    @pl.when(base > 0)
    def _():
        n = base + nsub

        def tile(i):
            return pl.ds(pl.multiple_of((n - 1 - i) * ts, ts), ts)

        def modes(i):
            kt = nsub - 1 - i
            return tuple("full" if (kt < 0 or s > kt) else ("diag" if s == kt else "none")
                         for s in range(nsub))

        def iteration(i, p, first=1, last=4, static_i=None):
            if last >= 4 and first <= 4:
                vs = [v_ref[tile(i - 3), hc] for hc in hcols]
                for u, (h, s) in enumerate(chains):
                    _sb_values(vs[h], wb_ref.at[1 - p], u, o_ref, rows[s], hcols[h])
            if last >= 3 and first <= 3:
                for u in range(len(chains)):
                    _sb_weights(tri, t_ref.at[1 - p], spb_ref.at[1 - p], wb_ref.at[p], u)
            if last >= 2 and first <= 2:
                md = modes(static_i - 1) if static_i is not None else ("full",) * nsub
                for u, (h, s) in enumerate(chains):
                    _sb_scores(z_ref.at[1 - p], c_ref, crows[u], t_ref.at[p], spb_ref.at[p], u, md[s])
            if last >= 1 and first <= 1:
                ks = [k_ref[tile(i), hc] for hc in hcols]
                for u, (h, s) in enumerate(chains):
                    _sb_logits(qs[u], ks[h], z_ref.at[p], u)

        nfill = 4
        assert nsub == 2
        for i in range(nfill):
            iteration(i, i & 1, last=min(i + 1, 4), static_i=i)

        def cond(carry):
            j, cmin = carry
            return jnp.logical_and(j < (n - nfill) // 2, cmin < SKIP_LOG2)

        def body(carry):
            j, _ = carry
            i = nfill + 2 * j
            iteration(i, 0)
            iteration(i + 1, 1)
            return j + 1, jnp.min(c_ref[...])

        trips, _ = lax.while_loop(cond, body, (jnp.int32(0), jnp.min(c_ref[...])))
        issued = nfill + 2 * trips
        for d in range(3):
            iteration(issued + d, d & 1, first=d + 2)


def _attn_prompt(qb, kb, vb, tri, nsub):
    t = qb.shape[0]
    tq = nsub * SB_TILE
    nhead = SB_HEADS_PER_STEP
    hw = nhead * HEAD_DIM
    nchain = nhead * nsub
    return pl.pallas_call(
        functools.partial(_attn_prompt_body, nsub=nsub, nhead=nhead),
        grid=(N_HEADS // nhead, t // tq),
        in_specs=[
            pl.BlockSpec((tq, hw), lambda h, i: (i, h)),
            pl.BlockSpec((t, hw), lambda h, i: (0, h)),
            pl.BlockSpec((t, hw), lambda h, i: (0, h)),
            pl.BlockSpec((SB_TILE, SB_TILE), lambda h, i: (0, 0)),
        ],
        out_specs=pl.BlockSpec((tq, hw), lambda h, i: (i, h)),
        out_shape=jax.ShapeDtypeStruct((t, D_SB), F32),
        scratch_shapes=[pltpu.VMEM((nchain * SB_TILE, 1), F32),
                        pltpu.VMEM((2, nchain, SB_TILE, SB_TILE), F32),
                        pltpu.VMEM((2, nchain, SB_TILE, SB_TILE), F32),
                        pltpu.VMEM((2, nchain, SB_TILE, SB_TILE), BF16),
                        pltpu.VMEM((2, nchain, SB_TILE, SB_TILE), BF16)],
        compiler_params=_params("parallel", "arbitrary"),
        name="sb_attn_prompt",
    )(qb, kb, vb, tri)


def _attn_sample_body(q_ref, kn_ref, vn_ref, kc_ref, vc_ref, trin_ref, tri_ref, o_ref, c_ref, *, tq, chunk):
    ts = SB_TILE
    tri = tri_ref[...]
    rows = [pl.ds(h * tq, tq) for h in range(N_HEADS)]
    cols = [slice(h * HEAD_DIM, (h + 1) * HEAD_DIM) for h in range(N_HEADS)]
    qs = [q_ref[:, cl] for cl in cols]

    @pl.when(pl.program_id(1) == 0)
    def _():
        c_ref[...] = jnp.zeros_like(c_ref)
        o_ref[...] = jnp.zeros_like(o_ref)
        trin = trin_ref[...]
        for h in range(N_HEADS):
            _sb_step(qs[h], kn_ref[:, cols[h]], vn_ref[:, cols[h]], trin, c_ref, rows[h], o_ref, slice(None), cols[h], True)

    @pl.when(jnp.min(c_ref[...]) < SKIP_LOG2)
    def _():
        for t in reversed(range(chunk // ts)):
            for h in range(N_HEADS):
                sel = pl.ds(t * ts * N_HEADS + h, ts, stride=N_HEADS)
                k = kc_ref[sel, :].astype(BF16)
                v = vc_ref[sel, :].astype(BF16)
                _sb_step(qs[h], k, v, tri, c_ref, rows[h], o_ref, slice(None), cols[h], False)


def _attn_sample(qb, kb, vb, kc, vc, tri_new, tri, nseq, tq, chunk):
    nchunk = kc.shape[1] // (chunk * N_HEADS)
    new = pl.BlockSpec((tq, D_SB), lambda b, c: (b, 0))
    cache = pl.BlockSpec((None, chunk * N_HEADS, HEAD_DIM), lambda b, c: (b, nchunk - 1 - c, 0))
    return pl.pallas_call(
        functools.partial(_attn_sample_body, tq=tq, chunk=chunk),
        grid=(nseq, nchunk),
        in_specs=[new, new, new, cache, cache,
                  pl.BlockSpec((tq, tq), lambda b, c: (0, 0)),
                  pl.BlockSpec((SB_TILE, SB_TILE), lambda b, c: (0, 0))],
        out_specs=new,
        out_shape=jax.ShapeDtypeStruct((nseq * tq, D_SB), F32),
        scratch_shapes=[pltpu.VMEM((N_HEADS * tq, 1), F32)],
        compiler_params=_params("parallel", "arbitrary"),
        name="sb_attn_sample",
    )(qb, kb, vb, kc, vc, tri_new, tri)


def _ffn_conv_act(pre, val, ext_ref, cw_ref, cb_ref, tt, cols=slice(None)):
    ext_ref[HALO:HALO + tt, cols] = pre
    cw = cw_ref[:, cols]
    pc = cb_ref[:, cols]
    for k in range(FFN_CONV - 1):
        lo = HALO - (FFN_CONV - 1) + k
        pc = pc + ext_ref[lo:lo + tt, cols] * cw[k:k + 1, :]
    pc = pc + pre * cw[FFN_CONV - 1:FFN_CONV, :]
    return (_gelu(pc) * val).astype(BF16), ext_ref[tt:tt + HALO, cols]


def _ffn_act_body(val_ref, pre_ref, sprev_ref, cw_ref, cb_ref, g_ref, sout_ref, ext_ref, *, tt):
    @pl.when(pl.program_id(2) == 0)
    def _():
        ext_ref[0:HALO, :] = sprev_ref[0]

    g, last = _ffn_conv_act(pre_ref[...], val_ref[...], ext_ref, cw_ref, cb_ref, tt)
    g_ref[...] = g
    sout_ref[0] = last
    ext_ref[0:HALO, :] = last


def _ffn_act(up, sprev, cw, cb, nseq, seqlen, tt, tc):
    m = nseq * seqlen
    nt = seqlen // tt
    nc = D_FF // tc
    return pl.pallas_call(
        functools.partial(_ffn_act_body, tt=tt),
        grid=(nseq, nc, nt),
        in_specs=[
            pl.BlockSpec((tt, tc), lambda b, c, t: (b * nt + t, c)),
            pl.BlockSpec((tt, tc), lambda b, c, t: (b * nt + t, nc + c)),
            pl.BlockSpec((1, HALO, tc), lambda b, c, t: (b, 0, c)),
            pl.BlockSpec((HALO, tc), lambda b, c, t: (0, c)),
            pl.BlockSpec((1, tc), lambda b, c, t: (0, c)),
        ],
        out_specs=[
            pl.BlockSpec((tt, tc), lambda b, c, t: (b * nt + t, c)),
            pl.BlockSpec((1, HALO, tc), lambda b, c, t: (b, 0, c)),
        ],
        out_shape=[
            jax.ShapeDtypeStruct((m, D_FF), BF16),
            jax.ShapeDtypeStruct((nseq, HALO, D_FF), F32),
        ],
        scratch_shapes=[pltpu.VMEM((tt + HALO, tc), F32)],
        compiler_params=_params("parallel", "parallel", "arbitrary"),
        name="ffn_act",
    )(up, up, sprev, cw, cb)


def _ffn_up_act_body(x_ref, g_ref, wv_ref, wp_ref, sprev_ref, cw_ref, cb_ref, o_ref, last_ref,
                     hn_ref, halo_ref, ext_ref, *pre, tm):
    i = pl.program_id(0)
    j = pl.program_id(1)
    _norm_row_tile(x_ref, g_ref, hn_ref, tuple(pre) if pre else None)

    @pl.when(i == 0)
    def _():
        ext_ref[0:HALO, :] = sprev_ref[...]

    @pl.when(i > 0)
    def _():
        ext_ref[0:HALO, :] = halo_ref[j]

    hn = hn_ref[...]
    tn = o_ref.shape[1]
    halves = [slice(h * MXU_COLS, (h + 1) * MXU_COLS) for h in range(tn // MXU_COLS)]
    pres = [jnp.dot(hn, wp_ref[:, cs], preferred_element_type=F32) for cs in halves]
    vals = [jnp.dot(hn, wv_ref[:, cs], preferred_element_type=F32) for cs in halves]
    for cs, pre, val in zip(halves, pres, vals):
        g, last = _ffn_conv_act(pre, val, ext_ref, cw_ref, cb_ref, tm, cs)
        o_ref[:, cs] = g
        halo_ref[j, :, cs] = last
        last_ref[0, :, cs] = last


def _ffn_up_act(x, g, w_up, sprev, cw, cb, tm, tn):
    m, k = x.shape
    nj = D_FF // tn
    xspec, xscratch = _rows_operand(tm, k, nj)
    return pl.pallas_call(
        functools.partial(_ffn_up_act_body, tm=tm),
        grid=(m // tm, nj),
        in_specs=[
            xspec,
            pl.BlockSpec((1, k), lambda i, j: (0, 0)),
            pl.BlockSpec((k, tn), lambda i, j: (0, j)),
            pl.BlockSpec((k, tn), lambda i, j: (0, nj + j)),
            pl.BlockSpec((HALO, tn), lambda i, j: (0, j)),
            pl.BlockSpec((HALO, tn), lambda i, j: (0, j)),
            pl.BlockSpec((1, tn), lambda i, j: (0, j)),
        ],
        out_specs=[
            pl.BlockSpec((tm, tn), lambda i, j: (i, j)),
            pl.BlockSpec((1, HALO, tn), lambda i, j: (i, 0, j)),
        ],
        out_shape=[
            jax.ShapeDtypeStruct((m, D_FF), BF16),
            jax.ShapeDtypeStruct((m // tm, HALO, D_FF), F32),
        ],
        scratch_shapes=[
            pltpu.VMEM((tm, k), BF16),
            pltpu.VMEM((nj, HALO, tn), F32),
            pltpu.VMEM((tm + HALO, tn), F32),
        ] + xscratch,
        compiler_params=_params("arbitrary", "arbitrary"),
        name="ffn_up_act",
    )(x, g, w_up, w_up, sprev, cw, cb)


def _front_pad(x, rows):
    return jnp.pad(x, ((0, 0), (rows - x.shape[1], 0), (0, 0)))


def _layer(x3, kpast, vpast, h0, cprev, fprev, wts, tri, tm, tt_lru, tt_ffn, attn_nsub=2, fuse_ffn=None,
           emit=False):
    nseq, seqlen, _ = x3.shape
    m = nseq * seqlen
    x = x3.reshape(m, D_MODEL)
    if fuse_ffn is None:
        fuse_ffn = nseq == 1
    assert not (emit and fuse_ffn)
    tn = 512 if emit else 1024
    tm_wide = ROW_TILE_PREFETCH if m % ROW_TILE_PREFETCH == 0 else tm
    wb = {}

    proj = _norm_matmul(x, wts["norm_mix_g"], wts["w_in"], tm_wide, 512, "in_proj", emit)
    if emit:
        proj, wb["w_in"] = proj
    y_lru, h_last, cstate = _lru(proj, h0[:, None, :], _front_pad(cprev, HALO), wts, nseq, seqlen, tt_lru)
    qb, kf, kb, vf, vb = _qkv(proj, wts["q_norm_g"], wts["k_norm_g"], tm)
    if kpast is None:
        y_sb = _attn_prompt(qb, kb, vb, tri, attn_nsub)
    else:
        past = kpast.shape[1]
        y_sb = _attn_sample(qb, kb, vb, kpast.reshape(nseq, past * N_HEADS, HEAD_DIM),
                            vpast.reshape(nseq, past * N_HEADS, HEAD_DIM),
                            tri[:seqlen, :seqlen], tri, nseq, seqlen, min(past, 1024))
    if emit:
        x1, wb["w_out_lru"], wb["w_out_sb"] = _outproj(y_lru, y_sb, wts["sb_out_g"], wts["w_out"], wts["w_out"],
                                                        x, tm, tn, emit)
    else:
        x1 = _outproj(y_lru, y_sb, wts["sb_out_g"], wts["w_out_lru"], wts["w_out_sb"], x, tm, tn)

    fprev8 = _front_pad(fprev, HALO)
    if fuse_ffn:
        g, lasts = _ffn_up_act(x1, wts["norm_ffn_g"], wts["w_ffn_up"], fprev8[0], wts["ffn_conv_w"],
                               wts["ffn_conv_b"], tm_wide, 512)
        fstate = lasts[-1:]
    else:
        up = _norm_matmul(x1, wts["norm_ffn_g"], wts["w_ffn_up"], tm, tn, "ffn_up", emit)
        if emit:
            up, wb["w_ffn_up"] = up
        g, fstate = _ffn_act(up, fprev8, wts["ffn_conv_w"], wts["ffn_conv_b"], nseq, seqlen, tt_ffn, 2048)
    out = _down(g, wts["w_ffn_down"], x1, min(m, 1024), 1024, 2048 if emit else 3072, emit)
    if emit:
        out, wb["w_ffn_down"] = out

    return (out.reshape(nseq, seqlen, D_MODEL),
            kf.reshape(nseq, seqlen, N_HEADS, HEAD_DIM),
            vf.reshape(nseq, seqlen, N_HEADS, HEAD_DIM),
            h_last[:, 0, :],
            cstate[:, HALO - (LRU_CONV - 1):, :],
            fstate[:, HALO - (FFN_CONV - 1):, :],
            wb)


def _pair_blocks(w):
    z = jnp.zeros((N_PAIRS, LRU_BLOCK_DIM, LRU_BLOCK_DIM), w.dtype)
    top = jnp.concatenate([w[0::2], z], axis=2)
    bot = jnp.concatenate([z, w[1::2]], axis=2)
    return jnp.concatenate([top, bot], axis=1).astype(BF16)


def _prep_weights(norm_mix_g, w_in, lru_conv_w, lru_conv_b, lru_w_a, lru_b_a, lru_w_x, lru_b_x,
                  lru_lambda, q_norm_g, k_norm_g, lru_out_g, sb_out_g, w_out, norm_ffn_g,
                  w_ffn_up, ffn_conv_w, ffn_conv_b, w_ffn_down):
    row = lambda v: v.reshape(1, -1)
    pad_rows = lambda w: jnp.pad(w, ((0, HALO - w.shape[0]), (0, 0)))
    return {
        "norm_mix_g": row(norm_mix_g), "w_in": w_in,
        "lru_conv_w": pad_rows(lru_conv_w), "lru_conv_b": row(lru_conv_b),
        "lru_w_a": _pair_blocks(lru_w_a), "lru_b_a": row(lru_b_a),
        "lru_w_x": _pair_blocks(lru_w_x), "lru_b_x": row(lru_b_x),
        "lru_lambda": row(lru_lambda),
        "q_norm_g": row(q_norm_g), "k_norm_g": row(k_norm_g),
        "lru_out_g": row(lru_out_g), "sb_out_g": row(sb_out_g), "w_out": w_out,
        "norm_ffn_g": row(norm_ffn_g), "w_ffn_up": w_ffn_up,
        "ffn_conv_w": pad_rows(ffn_conv_w), "ffn_conv_b": row(ffn_conv_b),
        "w_ffn_down": w_ffn_down,
    }


def kernel(x_prompt, x_sample, cache_sb_k, cache_sb_v, state_lru_h, state_lru_conv, state_ffn_conv, norm_mix_g, w_in, lru_conv_w, lru_conv_b, lru_w_a, lru_b_a, lru_w_x, lru_b_x, lru_lambda, q_norm_g, k_norm_g, lru_out_g, sb_out_g, w_out, norm_ffn_g, w_ffn_up, ffn_conv_w, ffn_conv_b, w_ffn_down):
    depth = w_in.shape[0]
    bp = x_prompt.shape[0]
    tri = jnp.tri(SB_TILE, k=-1, dtype=BF16)
    xp, xs = x_prompt, x_sample
    st_p, st_s = [], []
    for l in range(depth):
        wts = _prep_weights(norm_mix_g[l], w_in[l], lru_conv_w[l], lru_conv_b[l], lru_w_a[l], lru_b_a[l],
                            lru_w_x[l], lru_b_x[l], lru_lambda[l], q_norm_g[l], k_norm_g[l], lru_out_g[l],
                            sb_out_g[l], w_out[l], norm_ffn_g[l], w_ffn_up[l], ffn_conv_w[l],
                            ffn_conv_b[l], w_ffn_down[l])
        xs, *ss, wb = _layer(xs, cache_sb_k[l], cache_sb_v[l], state_lru_h[l], state_lru_conv[l],
                             state_ffn_conv[l], wts, tri, tm=xs.shape[0] * xs.shape[1],
                             tt_lru=xs.shape[1], tt_ffn=xs.shape[1], emit=True)
        st_s.append(ss)
        xp, *sp, _ = _layer(xp, None, None,
                            jnp.zeros((bp, D_LRU), F32),
                            jnp.zeros((bp, LRU_CONV - 1, D_LRU), F32),
                            jnp.zeros((bp, FFN_CONV - 1, D_FF), F32),
                            dict(wts, **wb), tri, tm=512, tt_lru=128, tt_ffn=256)
        st_p.append(sp)
    stack = lambda sts, i: jnp.stack([s[i] for s in sts])
    return (xp, xs) + tuple(stack(st_p, i) for i in range(5)) + tuple(stack(st_s, i) for i in range(5))
```

```python
import functools
import math

import jax
import jax.numpy as jnp
from jax import lax
from jax.experimental import pallas as pl
from jax.experimental.pallas import tpu as pltpu

F32 = jnp.float32
BF16 = jnp.bfloat16

D_MODEL = 4096
HEAD_DIM = 128
N_HEADS = 8
D_SB = N_HEADS * HEAD_DIM
D_LRU = D_MODEL - D_SB
LRU_BLOCKS = 16
LRU_BLOCK_DIM = D_LRU // LRU_BLOCKS
LRU_PAIR = 2 * LRU_BLOCK_DIM
N_PAIRS = LRU_BLOCKS // 2
LRU_CONV = 4
LRU_C = 8.0
D_FF = 3 * D_MODEL
FFN_CONV = 3
EPS = 1e-6
LOG2E = math.log2(math.e)
Q_SCALE = HEAD_DIM ** -0.5 * LOG2E
HALO = 8
SB_TILE = 256
SB_HEADS_PER_STEP = 2
MXU_COLS = 256
ROW_TILE_PREFETCH = 1024

VMEM_LIMIT = 60 * 1024 * 1024


def _params(*sem):
    return pltpu.CompilerParams(dimension_semantics=sem, vmem_limit_bytes=VMEM_LIMIT)


def _rms(x, g):
    ms = jnp.mean(x * x, axis=-1, keepdims=True)
    return (x * lax.rsqrt(ms + EPS)) * g


def _rms_rows(x_ref, g_ref, o_ref, chunk=256):
    g = g_ref[...]
    for r in range(0, x_ref.shape[0], chunk):
        rows = slice(r, min(r + chunk, x_ref.shape[0]))
        o_ref[rows, :] = _rms(x_ref[rows, :], g).astype(BF16)


def _gelu(x):
    return x * (0.5 * (1.0 + jnp.tanh(0.7978845608028654 * (x + 0.044715 * (x * x * x)))))


def _sigmoid(x):
    return 1.0 / (1.0 + jnp.exp(-x))


def _softplus(x):
    return jnp.maximum(x, 0.0) + jnp.log1p(jnp.exp(-jnp.abs(x)))


def _weight_block(w_ref, wb_ref):
    w = w_ref[...]
    if wb_ref is not None:
        w = w.astype(BF16)
        wb_ref[...] = w
    return w


def _rows_copy(x_hbm, xbuf, sem, i):
    tm = xbuf.shape[0]
    return pltpu.make_async_copy(x_hbm.at[pl.ds(pl.multiple_of(i * tm, tm), tm), :], xbuf, sem)


def _norm_row_tile(x_ref, g_ref, hn_ref, prefetch):
    i, j = pl.program_id(0), pl.program_id(1)
    if prefetch is None:
        @pl.when(j == 0)
        def _():
            _rms_rows(x_ref, g_ref, hn_ref)
        return
    xbuf, sem = prefetch

    @pl.when(j == 0)
    def _():
        @pl.when(i == 0)
        def _():
            _rows_copy(x_ref, xbuf, sem, 0).start()

        _rows_copy(x_ref, xbuf, sem, i).wait()
        _rms_rows(xbuf, g_ref, hn_ref)

    @pl.when(jnp.logical_and(j == 1, i + 1 < pl.num_programs(0)))
    def _():
        _rows_copy(x_ref, xbuf, sem, i + 1).start()


def _rows_operand(tm, k, ncols):
    if tm >= ROW_TILE_PREFETCH and ncols >= 2:
        return pl.BlockSpec(memory_space=pl.ANY), [pltpu.VMEM((tm, k), F32), pltpu.SemaphoreType.DMA(())]
    return pl.BlockSpec((tm, k), lambda i, j: (i, 0)), []


def _norm_matmul_body(x_ref, g_ref, w_ref, o_ref, *rest, emit, prefetch):
    wb_ref, hn_ref, *pre = rest if emit else (None,) + rest
    _norm_row_tile(x_ref, g_ref, hn_ref, tuple(pre) if prefetch else None)
    o_ref[...] = jnp.dot(hn_ref[...], _weight_block(w_ref, wb_ref), preferred_element_type=F32)


def _norm_matmul(x, g, w, tm, tn, name, emit=False):
    m, k = x.shape
    n = w.shape[1]
    assert not emit or m == tm
    xspec, xscratch = _rows_operand(tm, k, n // tn)
    wspec = pl.BlockSpec((k, tn), lambda i, j: (0, j))
    ospec = pl.BlockSpec((tm, tn), lambda i, j: (i, j))
    oshape = jax.ShapeDtypeStruct((m, n), F32)
    return pl.pallas_call(
        functools.partial(_norm_matmul_body, emit=emit, prefetch=bool(xscratch)),
        grid=(m // tm, n // tn),
        in_specs=[
            xspec,
            pl.BlockSpec((1, k), lambda i, j: (0, 0)),
            wspec,
        ],
        out_specs=[ospec, wspec] if emit else ospec,
        out_shape=[oshape, jax.ShapeDtypeStruct((k, n), BF16)] if emit else oshape,
        scratch_shapes=[pltpu.VMEM((tm, k), BF16)] + xscratch,
        compiler_params=_params("arbitrary", "arbitrary"),
        name=name,
    )(x, g, w)


def _outproj_body(a_ref, y_ref, gsb_ref, w1_ref, w2_ref, res_ref, o_ref, *rest, emit):
    w1b_ref, w2b_ref, yn_ref = rest if emit else (None, None) + rest

    @pl.when(pl.program_id(1) == 0)
    def _():
        _rms_rows(y_ref, gsb_ref, yn_ref)

    acc = jnp.dot(a_ref[...], _weight_block(w1_ref, w1b_ref), preferred_element_type=F32)
    acc = acc + jnp.dot(yn_ref[...], _weight_block(w2_ref, w2b_ref), preferred_element_type=F32)
    o_ref[...] = res_ref[...] + acc


def _outproj(a, y_sb, g_sb, w1, w2, res, tm, tn, emit=False):
    m = a.shape[0]
    n = w1.shape[1]
    assert not emit or m == tm
    w2_row_block = (w2.shape[0] - D_SB) // D_SB
    ospec = pl.BlockSpec((tm, tn), lambda i, j: (i, j))
    oshape = jax.ShapeDtypeStruct((m, n), F32)
    wbspecs = [pl.BlockSpec((D_LRU, tn), lambda i, j: (0, j)), pl.BlockSpec((D_SB, tn), lambda i, j: (0, j))]
    wbshapes = [jax.ShapeDtypeStruct((D_LRU, n), BF16), jax.ShapeDtypeStruct((D_SB, n), BF16)]
    return pl.pallas_call(
        functools.partial(_outproj_body, emit=emit),
        grid=(m // tm, n // tn),
        in_specs=[
            pl.BlockSpec((tm, D_LRU), lambda i, j: (i, 0)),
            pl.BlockSpec((tm, D_SB), lambda i, j: (i, 0)),
            pl.BlockSpec((1, D_SB), lambda i, j: (0, 0)),
            pl.BlockSpec((D_LRU, tn), lambda i, j: (0, j)),
            pl.BlockSpec((D_SB, tn), lambda i, j: (w2_row_block, j)),
            ospec,
        ],
        out_specs=[ospec] + wbspecs if emit else ospec,
        out_shape=[oshape] + wbshapes if emit else oshape,
        scratch_shapes=[pltpu.VMEM((tm, D_SB), BF16)],
        compiler_params=_params("parallel", "arbitrary"),
        name="out_proj",
    )(a, y_sb, g_sb, w1, w2, res)


def _down_body(g_ref, w_ref, res_ref, o_ref, *rest, emit):
    wb_ref, acc_ref = rest if emit else (None,) + rest
    k = pl.program_id(2)

    @pl.when(k == 0)
    def _():
        acc_ref[...] = jnp.zeros_like(acc_ref)

    acc_ref[...] += jnp.dot(g_ref[...], _weight_block(w_ref, wb_ref), preferred_element_type=F32)

    @pl.when(k == pl.num_programs(2) - 1)
    def _():
        o_ref[...] = res_ref[...] + acc_ref[...]


def _down(g, w, res, tm, tn, tk, emit=False):
    m, kdim = g.shape
    n = w.shape[1]
    assert not emit or m == tm
    wspec = pl.BlockSpec((tk, tn), lambda i, j, k: (k, j))
    ospec = pl.BlockSpec((tm, tn), lambda i, j, k: (i, j))
    oshape = jax.ShapeDtypeStruct((m, n), F32)
    return pl.pallas_call(
        functools.partial(_down_body, emit=emit),
        grid=(m // tm, n // tn, kdim // tk),
        in_specs=[pl.BlockSpec((tm, tk), lambda i, j, k: (i, k)), wspec, ospec],
        out_specs=[ospec, wspec] if emit else ospec,
        out_shape=[oshape, jax.ShapeDtypeStruct((kdim, n), BF16)] if emit else oshape,
        scratch_shapes=[pltpu.VMEM((tm, tn), F32)],
        compiler_params=_params("parallel", "parallel", "arbitrary"),
        name="ffn_down",
    )(g, w, res)


def _lru_body(u_ref, gate_ref, h0_ref, cprev_ref, cw_ref, cb_ref, wa_ref, ba_ref, wx_ref, bx_ref,
              lam_ref, og_ref, y_ref, hlast_ref, cstate_ref,
              ext_ref, uc_ref, a_ref, b_ref, hs_ref, hc_ref, *, tt):
    @pl.when(pl.program_id(1) == 0)
    def _():
        hc_ref[...] = h0_ref[0]
        ext_ref[0:HALO, :] = cprev_ref[0]

    u = u_ref[...]
    ext_ref[HALO:HALO + tt, :] = u
    cw = cw_ref[...]
    uc = cb_ref[...]
    for k in range(LRU_CONV - 1):
        lo = HALO - (LRU_CONV - 1) + k
        uc = uc + ext_ref[lo:lo + tt, :] * cw[k:k + 1, :]
    uc_ref[...] = uc + u * cw[LRU_CONV - 1:LRU_CONV, :]

    last = ext_ref[tt:tt + HALO, :]
    cstate_ref[0] = last
    ext_ref[0:HALO, :] = last

    for p in range(N_PAIRS):
        sl = slice(p * LRU_PAIR, (p + 1) * LRU_PAIR)
        ucp = uc_ref[:, sl]
        xb = ucp.astype(BF16)
        r = _sigmoid(jnp.dot(xb, wa_ref[p], preferred_element_type=F32) + ba_ref[:, sl])
        i = _sigmoid(jnp.dot(xb, wx_ref[p], preferred_element_type=F32) + bx_ref[:, sl])
        log_a = (LRU_C * r) * (-_softplus(-lam_ref[:, sl]))
        a = jnp.exp(log_a)
        a_ref[:, sl] = a
        b_ref[:, sl] = jnp.sqrt(-jnp.tanh(log_a) * (a * a + 1.0)) * (i * ucp)

    def step(t, h):
        h = a_ref[pl.ds(t, 1), :] * h + b_ref[pl.ds(t, 1), :]
        hs_ref[pl.ds(t, 1), :] = h
        return h

    h = lax.fori_loop(0, tt, step, hc_ref[...], unroll=8)
    hc_ref[...] = h
    hlast_ref[0] = h

    y = hs_ref[...] * _gelu(gate_ref[...])
    y_ref[...] = _rms(y, og_ref[...]).astype(BF16)


def _lru(proj, h0, cprev, wts, nseq, seqlen, tt):
    m = nseq * seqlen
    nt = seqlen // tt
    row = lambda b, t: (b * nt + t, 0)
    vec = lambda b, t: (0, 0)
    return pl.pallas_call(
        functools.partial(_lru_body, tt=tt),
        grid=(nseq, nt),
        in_specs=[
            pl.BlockSpec((tt, D_LRU), row),
            pl.BlockSpec((tt, D_LRU), lambda b, t: (b * nt + t, 1)),
            pl.BlockSpec((1, 1, D_LRU), lambda b, t: (b, 0, 0)),
            pl.BlockSpec((1, HALO, D_LRU), lambda b, t: (b, 0, 0)),
            pl.BlockSpec((HALO, D_LRU), vec),
            pl.BlockSpec((1, D_LRU), vec),
            pl.BlockSpec((N_PAIRS, LRU_PAIR, LRU_PAIR), lambda b, t: (0, 0, 0)),
            pl.BlockSpec((1, D_LRU), vec),
            pl.BlockSpec((N_PAIRS, LRU_PAIR, LRU_PAIR), lambda b, t: (0, 0, 0)),
            pl.BlockSpec((1, D_LRU), vec),
            pl.BlockSpec((1, D_LRU), vec),
            pl.BlockSpec((1, D_LRU), vec),
        ],
        out_specs=[
            pl.BlockSpec((tt, D_LRU), row),
            pl.BlockSpec((1, 1, D_LRU), lambda b, t: (b, 0, 0)),
            pl.BlockSpec((1, HALO, D_LRU), lambda b, t: (b, 0, 0)),
        ],
        out_shape=[
            jax.ShapeDtypeStruct((m, D_LRU), BF16),
            jax.ShapeDtypeStruct((nseq, 1, D_LRU), F32),
            jax.ShapeDtypeStruct((nseq, HALO, D_LRU), F32),
        ],
        scratch_shapes=[
            pltpu.VMEM((tt + HALO, D_LRU), F32),
            pltpu.VMEM((tt, D_LRU), F32),
            pltpu.VMEM((tt, D_LRU), F32),
            pltpu.VMEM((tt, D_LRU), F32),
            pltpu.VMEM((tt, D_LRU), F32),
            pltpu.VMEM((1, D_LRU), F32),
        ],
        compiler_params=_params("arbitrary", "arbitrary"),
        name="rg_lru",
    )(proj, proj, h0, cprev, wts["lru_conv_w"], wts["lru_conv_b"], wts["lru_w_a"], wts["lru_b_a"],
      wts["lru_w_x"], wts["lru_b_x"], wts["lru_lambda"], wts["lru_out_g"])


def _qkv_body(q_ref, k_ref, v_ref, gq_ref, gk_ref, qb_ref, kf_ref, kb_ref, vf_ref, vb_ref):
    for h in range(N_HEADS):
        sl = slice(h * HEAD_DIM, (h + 1) * HEAD_DIM)
        qb_ref[:, sl] = (_rms(q_ref[:, sl], gq_ref[...]) * Q_SCALE).astype(BF16)
        kn = _rms(k_ref[:, sl], gk_ref[...])
        kf_ref[:, sl] = kn
        kb_ref[:, sl] = kn.astype(BF16)
    v = v_ref[...]
    vf_ref[...] = v
    vb_ref[...] = v.astype(BF16)


def _qkv(proj, gq, gk, tm):
    m = proj.shape[0]
    col0 = 2 * D_LRU // D_SB
    blk = lambda c: pl.BlockSpec((tm, D_SB), lambda i: (i, c))
    vec = pl.BlockSpec((1, HEAD_DIM), lambda i: (0, 0))
    out = pl.BlockSpec((tm, D_SB), lambda i: (i, 0))
    return pl.pallas_call(
        _qkv_body,
        grid=(m // tm,),
        in_specs=[blk(col0), blk(col0 + 1), blk(col0 + 2), vec, vec],
        out_specs=[out] * 5,
        out_shape=[jax.ShapeDtypeStruct((m, D_SB), d) for d in (BF16, F32, BF16, F32, BF16)],
        compiler_params=_params("parallel"),
        name="qkv_norm",
    )(proj, proj, proj, gq, gk)


def _neg_abs(x):
    return lax.bitcast_convert_type(lax.bitcast_convert_type(x, jnp.int32) | jnp.int32(-2 ** 31), F32)


def _sb_step(q, k, v, tri, c_ref, crows, acc_ref, arows, acols, masked):
    z = lax.dot_general(q, k, (((1,), (1,)), ((), ())), preferred_element_type=F32)
    sp = jnp.maximum(z, 0.0) + jnp.log(1.0 + jnp.exp2(_neg_abs(z))) * LOG2E
    if masked:
        mask = (lax.broadcasted_iota(jnp.int32, z.shape, 1) < lax.broadcasted_iota(jnp.int32, z.shape, 0))
        sp = jnp.where(mask, sp, 0.0)
    inner = jnp.dot(sp.astype(BF16), tri, preferred_element_type=F32)
    c = c_ref[crows, :]
    w = jnp.exp2((z - sp) - (inner + c))
    if masked:
        w = jnp.where(mask, w, 0.0)
    acc_ref[arows, acols] += jnp.dot(w.astype(BF16), v, preferred_element_type=F32)
    c_ref[crows, :] = c + jnp.sum(sp, axis=-1, keepdims=True)


NEG_BIG = -1e30
SKIP_LOG2 = 160.0


def _sb_logits(q, k, z_ref, slot):
    z_ref[slot] = lax.dot_general(q, k, (((1,), (1,)), ((), ())), preferred_element_type=F32)


def _sb_scores(z_ref, c_ref, crows, t_ref, spb_ref, slot, mode):
    if mode == "none":
        t_ref[slot] = jnp.full(t_ref.shape[1:], NEG_BIG, F32)
        spb_ref[slot] = jnp.zeros(spb_ref.shape[1:], BF16)
        return
    z = z_ref[slot]
    sp = jnp.maximum(z, 0.0) + jnp.log(1.0 + jnp.exp2(_neg_abs(z))) * LOG2E
    c = c_ref[crows, :]
    t = (z - sp) - c
    if mode == "diag":
        mask = (lax.broadcasted_iota(jnp.int32, z.shape, 1) < lax.broadcasted_iota(jnp.int32, z.shape, 0))
        sp = jnp.where(mask, sp, 0.0)
        t = jnp.where(mask, t, NEG_BIG)
    t_ref[slot] = t
    spb_ref[slot] = sp.astype(BF16)
    c_ref[crows, :] = c + jnp.sum(sp, axis=-1, keepdims=True)


def _sb_weights(tri, t_ref, spb_ref, wb_ref, slot):
    inner = jnp.dot(spb_ref[slot], tri, preferred_element_type=F32)
    wb_ref[slot] = jnp.exp2(t_ref[slot] - inner).astype(BF16)


def _sb_values(v, wb_ref, slot, acc_ref, arows, acols):
    acc_ref[arows, acols] += jnp.dot(wb_ref[slot], v, preferred_element_type=F32)


def _attn_prompt_body(q_ref, k_ref, v_ref, tri_ref, o_ref, c_ref, z_ref, t_ref, spb_ref, wb_ref, *, nsub, nhead):
    ts = SB_TILE
    base = pl.program_id(1) * nsub
    tri = tri_ref[...]
    c_ref[...] = jnp.zeros_like(c_ref)
    o_ref[...] = jnp.zeros_like(o_ref)
    rows = [pl.ds(s * ts, ts) for s in range(nsub)]
    hcols = [slice(h * HEAD_DIM, (h + 1) * HEAD_DIM) for h in range(nhead)]
    chains = [(h, s) for h in range(nhead) for s in range(nsub)]
    crows = [pl.ds(u * ts, ts) for u in range(len(chains))]
    qs = [q_ref[rows[s], hcols[h]] for h, s in chains]

    @pl.when(base == 0)
    def _():
        for t in reversed(range(nsub)):
            for u, (h, s) in enumerate(chains):
                if s >= t:
                    k = k_ref[t * ts:(t + 1) * ts, hcols[h]]
                    v = v_ref[t * ts:(t + 1) * ts, hcols[h]]
                    _sb_step(qs[u], k, v, tri, c_ref, crows[u], o_ref, rows[s], hcols[h], masked=(s == t))

    @pl.when(base > 0)
    def _():
        n = base + nsub

        def tile(i):
            return pl.ds(pl.multiple_of((n - 1 - i) * ts, ts), ts)

        def modes(i):
            kt = nsub - 1 - i
            return tuple("full" if (kt < 0 or s > kt) else ("diag" if s == kt else "none")
                         for s in range(nsub))

        def iteration(i, p, first=1, last=4, static_i=None):
            if last >= 4 and first <= 4:
                vs = [v_ref[tile(i - 3), hc] for hc in hcols]
                for u, (h, s) in enumerate(chains):
                    _sb_values(vs[h], wb_ref.at[1 - p], u, o_ref, rows[s], hcols[h])
            if last >= 3 and first <= 3:
                for u in range(len(chains)):
                    _sb_weights(tri, t_ref.at[1 - p], spb_ref.at[1 - p], wb_ref.at[p], u)
            if last >= 2 and first <= 2:
                md = modes(static_i - 1) if static_i is not None else ("full",) * nsub
                for u, (h, s) in enumerate(chains):
                    _sb_scores(z_ref.at[1 - p], c_ref, crows[u], t_ref.at[p], spb_ref.at[p], u, md[s])
            if last >= 1 and first <= 1:
                ks = [k_ref[tile(i), hc] for hc in hcols]
                for u, (h, s) in enumerate(chains):
                    _sb_logits(qs[u], ks[h], z_ref.at[p], u)

        nfill = 4
        assert nsub == 2
        for i in range(nfill):
            iteration(i, i & 1, last=min(i + 1, 4), static_i=i)

        def cond(carry):
            j, cmin = carry
            return jnp.logical_and(j < (n - nfill) // 2, cmin < SKIP_LOG2)

        def body(carry):
            j, _ = carry
            i = nfill + 2 * j
            iteration(i, 0)
            iteration(i + 1, 1)
            return j + 1, jnp.min(c_ref[...])

        trips, _ = lax.while_loop(cond, body, (jnp.int32(0), jnp.min(c_ref[...])))
        issued = nfill + 2 * trips
        for d in range(3):
            iteration(issued + d, d & 1, first=d + 2)


def _attn_prompt(qb, kb, vb, tri, nsub):
    t = qb.shape[0]
    tq = nsub * SB_TILE
    nhead = SB_HEADS_PER_STEP
    hw = nhead * HEAD_DIM
    nchain = nhead * nsub
    return pl.pallas_call(
        functools.partial(_attn_prompt_body, nsub=nsub, nhead=nhead),
        grid=(N_HEADS // nhead, t // tq),
        in_specs=[
            pl.BlockSpec((tq, hw), lambda h, i: (i, h)),
            pl.BlockSpec((t, hw), lambda h, i: (0, h)),
            pl.BlockSpec((t, hw), lambda h, i: (0, h)),
            pl.BlockSpec((SB_TILE, SB_TILE), lambda h, i: (0, 0)),
        ],
        out_specs=pl.BlockSpec((tq, hw), lambda h, i: (i, h)),
        out_shape=jax.ShapeDtypeStruct((t, D_SB), F32),
        scratch_shapes=[pltpu.VMEM((nchain * SB_TILE, 1), F32),
                        pltpu.VMEM((2, nchain, SB_TILE, SB_TILE), F32),
                        pltpu.VMEM((2, nchain, SB_TILE, SB_TILE), F32),
                        pltpu.VMEM((2, nchain, SB_TILE, SB_TILE), BF16),
                        pltpu.VMEM((2, nchain, SB_TILE, SB_TILE), BF16)],
        compiler_params=_params("parallel", "arbitrary"),
        name="sb_attn_prompt",
    )(qb, kb, vb, tri)


def _attn_sample_body(q_ref, kn_ref, vn_ref, kc_ref, vc_ref, trin_ref, tri_ref, o_ref, c_ref, *, tq, chunk):
    ts = SB_TILE
    tri = tri_ref[...]
    rows = [pl.ds(h * tq, tq) for h in range(N_HEADS)]
    cols = [slice(h * HEAD_DIM, (h + 1) * HEAD_DIM) for h in range(N_HEADS)]
    qs = [q_ref[:, cl] for cl in cols]

    @pl.when(pl.program_id(1) == 0)
    def _():
        c_ref[...] = jnp.zeros_like(c_ref)
        o_ref[...] = jnp.zeros_like(o_ref)
        trin = trin_ref[...]
        for h in range(N_HEADS):
            _sb_step(qs[h], kn_ref[:, cols[h]], vn_ref[:, cols[h]], trin, c_ref, rows[h], o_ref, slice(None), cols[h], True)

    @pl.when(jnp.min(c_ref[...]) < SKIP_LOG2)
    def _():
        for t in reversed(range(chunk // ts)):
            for h in range(N_HEADS):
                sel = pl.ds(t * ts * N_HEADS + h, ts, stride=N_HEADS)
                k = kc_ref[sel, :].astype(BF16)
                v = vc_ref[sel, :].astype(BF16)
                _sb_step(qs[h], k, v, tri, c_ref, rows[h], o_ref, slice(None), cols[h], False)


def _attn_sample(qb, kb, vb, kc, vc, tri_new, tri, nseq, tq, chunk):
    nchunk = kc.shape[1] // (chunk * N_HEADS)
    new = pl.BlockSpec((tq, D_SB), lambda b, c: (b, 0))
    cache = pl.BlockSpec((None, chunk * N_HEADS, HEAD_DIM), lambda b, c: (b, nchunk - 1 - c, 0))
    return pl.pallas_call(
        functools.partial(_attn_sample_body, tq=tq, chunk=chunk),
        grid=(nseq, nchunk),
        in_specs=[new, new, new, cache, cache,
                  pl.BlockSpec((tq, tq), lambda b, c: (0, 0)),
                  pl.BlockSpec((SB_TILE, SB_TILE), lambda b, c: (0, 0))],
        out_specs=new,
        out_shape=jax.ShapeDtypeStruct((nseq * tq, D_SB), F32),
        scratch_shapes=[pltpu.VMEM((N_HEADS * tq, 1), F32)],
        compiler_params=_params("parallel", "arbitrary"),
        name="sb_attn_sample",
    )(qb, kb, vb, kc, vc, tri_new, tri)


def _ffn_conv_act(pre, val, ext_ref, cw_ref, cb_ref, tt, cols=slice(None)):
    ext_ref[HALO:HALO + tt, cols] = pre
    cw = cw_ref[:, cols]
    pc = cb_ref[:, cols]
    for k in range(FFN_CONV - 1):
        lo = HALO - (FFN_CONV - 1) + k
        pc = pc + ext_ref[lo:lo + tt, cols] * cw[k:k + 1, :]
    pc = pc + pre * cw[FFN_CONV - 1:FFN_CONV, :]
    return (_gelu(pc) * val).astype(BF16), ext_ref[tt:tt + HALO, cols]


def _ffn_act_body(val_ref, pre_ref, sprev_ref, cw_ref, cb_ref, g_ref, sout_ref, ext_ref, *, tt):
    @pl.when(pl.program_id(2) == 0)
    def _():
        ext_ref[0:HALO, :] = sprev_ref[0]

    g, last = _ffn_conv_act(pre_ref[...], val_ref[...], ext_ref, cw_ref, cb_ref, tt)
    g_ref[...] = g
    sout_ref[0] = last
    ext_ref[0:HALO, :] = last


def _ffn_act(up, sprev, cw, cb, nseq, seqlen, tt, tc):
    m = nseq * seqlen
    nt = seqlen // tt
    nc = D_FF // tc
    return pl.pallas_call(
        functools.partial(_ffn_act_body, tt=tt),
        grid=(nseq, nc, nt),
        in_specs=[
            pl.BlockSpec((tt, tc), lambda b, c, t: (b * nt + t, c)),
            pl.BlockSpec((tt, tc), lambda b, c, t: (b * nt + t, nc + c)),
            pl.BlockSpec((1, HALO, tc), lambda b, c, t: (b, 0, c)),
            pl.BlockSpec((HALO, tc), lambda b, c, t: (0, c)),
            pl.BlockSpec((1, tc), lambda b, c, t: (0, c)),
        ],
        out_specs=[
            pl.BlockSpec((tt, tc), lambda b, c, t: (b * nt + t, c)),
            pl.BlockSpec((1, HALO, tc), lambda b, c, t: (b, 0, c)),
        ],
        out_shape=[
            jax.ShapeDtypeStruct((m, D_FF), BF16),
            jax.ShapeDtypeStruct((nseq, HALO, D_FF), F32),
        ],
        scratch_shapes=[pltpu.VMEM((tt + HALO, tc), F32)],
        compiler_params=_params("parallel", "parallel", "arbitrary"),
        name="ffn_act",
    )(up, up, sprev, cw, cb)


def _ffn_up_act_body(x_ref, g_ref, wv_ref, wp_ref, sprev_ref, cw_ref, cb_ref, o_ref, last_ref,
                     hn_ref, halo_ref, ext_ref, *pre, tm):
    i = pl.program_id(0)
    j = pl.program_id(1)
    _norm_row_tile(x_ref, g_ref, hn_ref, tuple(pre) if pre else None)

    @pl.when(i == 0)
    def _():
        ext_ref[0:HALO, :] = sprev_ref[...]

    @pl.when(i > 0)
    def _():
        ext_ref[0:HALO, :] = halo_ref[j]

    hn = hn_ref[...]
    tn = o_ref.shape[1]
    halves = [slice(h * MXU_COLS, (h + 1) * MXU_COLS) for h in range(tn // MXU_COLS)]
    pres = [jnp.dot(hn, wp_ref[:, cs], preferred_element_type=F32) for cs in halves]
    vals = [jnp.dot(hn, wv_ref[:, cs], preferred_element_type=F32) for cs in halves]
    for cs, pre, val in zip(halves, pres, vals):
        g, last = _ffn_conv_act(pre, val, ext_ref, cw_ref, cb_ref, tm, cs)
        o_ref[:, cs] = g
        halo_ref[j, :, cs] = last
        last_ref[0, :, cs] = last


def _ffn_up_act(x, g, w_up, sprev, cw, cb, tm, tn):
    m, k = x.shape
    nj = D_FF // tn
    xspec, xscratch = _rows_operand(tm, k, nj)
    return pl.pallas_call(
        functools.partial(_ffn_up_act_body, tm=tm),
        grid=(m // tm, nj),
        in_specs=[
            xspec,
            pl.BlockSpec((1, k), lambda i, j: (0, 0)),
            pl.BlockSpec((k, tn), lambda i, j: (0, j)),
            pl.BlockSpec((k, tn), lambda i, j: (0, nj + j)),
            pl.BlockSpec((HALO, tn), lambda i, j: (0, j)),
            pl.BlockSpec((HALO, tn), lambda i, j: (0, j)),
            pl.BlockSpec((1, tn), lambda i, j: (0, j)),
        ],
        out_specs=[
            pl.BlockSpec((tm, tn), lambda i, j: (i, j)),
            pl.BlockSpec((1, HALO, tn), lambda i, j: (i, 0, j)),
        ],
        out_shape=[
            jax.ShapeDtypeStruct((m, D_FF), BF16),
            jax.ShapeDtypeStruct((m // tm, HALO, D_FF), F32),
        ],
        scratch_shapes=[
            pltpu.VMEM((tm, k), BF16),
            pltpu.VMEM((nj, HALO, tn), F32),
            pltpu.VMEM((tm + HALO, tn), F32),
        ] + xscratch,
        compiler_params=_params("arbitrary", "arbitrary"),
        name="ffn_up_act",
    )(x, g, w_up, w_up, sprev, cw, cb)


def _front_pad(x, rows):
    return jnp.pad(x, ((0, 0), (rows - x.shape[1], 0), (0, 0)))


def _layer(x3, kpast, vpast, h0, cprev, fprev, wts, tri, tm, tt_lru, tt_ffn, attn_nsub=2, fuse_ffn=None,
           emit=False):
    nseq, seqlen, _ = x3.shape
    m = nseq * seqlen
    x = x3.reshape(m, D_MODEL)
    if fuse_ffn is None:
        fuse_ffn = nseq == 1
    assert not (emit and fuse_ffn)
    tn = 512 if emit else 1024
    tm_wide = ROW_TILE_PREFETCH if m % ROW_TILE_PREFETCH == 0 else tm
    wb = {}

    proj = _norm_matmul(x, wts["norm_mix_g"], wts["w_in"], tm_wide, 512, "in_proj", emit)
    if emit:
        proj, wb["w_in"] = proj
    y_lru, h_last, cstate = _lru(proj, h0[:, None, :], _front_pad(cprev, HALO), wts, nseq, seqlen, tt_lru)
    qb, kf, kb, vf, vb = _qkv(proj, wts["q_norm_g"], wts["k_norm_g"], tm)
    if kpast is None:
        y_sb = _attn_prompt(qb, kb, vb, tri, attn_nsub)
    else:
        past = kpast.shape[1]
        y_sb = _attn_sample(qb, kb, vb, kpast.reshape(nseq, past * N_HEADS, HEAD_DIM),
                            vpast.reshape(nseq, past * N_HEADS, HEAD_DIM),
                            tri[:seqlen, :seqlen], tri, nseq, seqlen, min(past, 1024))
    if emit:
        x1, wb["w_out_lru"], wb["w_out_sb"] = _outproj(y_lru, y_sb, wts["sb_out_g"], wts["w_out"], wts["w_out"],
                                                        x, tm, tn, emit)
    else:
        x1 = _outproj(y_lru, y_sb, wts["sb_out_g"], wts["w_out_lru"], wts["w_out_sb"], x, tm_wide, 512)

    fprev8 = _front_pad(fprev, HALO)
    if fuse_ffn:
        g, lasts = _ffn_up_act(x1, wts["norm_ffn_g"], wts["w_ffn_up"], fprev8[0], wts["ffn_conv_w"],
                               wts["ffn_conv_b"], tm_wide, 512)
        fstate = lasts[-1:]
    else:
        up = _norm_matmul(x1, wts["norm_ffn_g"], wts["w_ffn_up"], tm, tn, "ffn_up", emit)
        if emit:
            up, wb["w_ffn_up"] = up
        g, fstate = _ffn_act(up, fprev8, wts["ffn_conv_w"], wts["ffn_conv_b"], nseq, seqlen, tt_ffn,
                             D_FF if tt_ffn <= 64 else 2048)
    out = _down(g, wts["w_ffn_down"], x1, min(m, 1024), 1024, 2048 if emit else 3072, emit)
    if emit:
        out, wb["w_ffn_down"] = out

    return (out.reshape(nseq, seqlen, D_MODEL),
            kf.reshape(nseq, seqlen, N_HEADS, HEAD_DIM),
            vf.reshape(nseq, seqlen, N_HEADS, HEAD_DIM),
            h_last[:, 0, :],
            cstate[:, HALO - (LRU_CONV - 1):, :],
            fstate[:, HALO - (FFN_CONV - 1):, :],
            wb)


def _pair_blocks(w):
    z = jnp.zeros((N_PAIRS, LRU_BLOCK_DIM, LRU_BLOCK_DIM), w.dtype)
    top = jnp.concatenate([w[0::2], z], axis=2)
    bot = jnp.concatenate([z, w[1::2]], axis=2)
    return jnp.concatenate([top, bot], axis=1).astype(BF16)


def _prep_weights(norm_mix_g, w_in, lru_conv_w, lru_conv_b, lru_w_a, lru_b_a, lru_w_x, lru_b_x,
                  lru_lambda, q_norm_g, k_norm_g, lru_out_g, sb_out_g, w_out, norm_ffn_g,
                  w_ffn_up, ffn_conv_w, ffn_conv_b, w_ffn_down):
    row = lambda v: v.reshape(1, -1)
    pad_rows = lambda w: jnp.pad(w, ((0, HALO - w.shape[0]), (0, 0)))
    return {
        "norm_mix_g": row(norm_mix_g), "w_in": w_in,
        "lru_conv_w": pad_rows(lru_conv_w), "lru_conv_b": row(lru_conv_b),
        "lru_w_a": _pair_blocks(lru_w_a), "lru_b_a": row(lru_b_a),
        "lru_w_x": _pair_blocks(lru_w_x), "lru_b_x": row(lru_b_x),
        "lru_lambda": row(lru_lambda),
        "q_norm_g": row(q_norm_g), "k_norm_g": row(k_norm_g),
        "lru_out_g": row(lru_out_g), "sb_out_g": row(sb_out_g), "w_out": w_out,
        "norm_ffn_g": row(norm_ffn_g), "w_ffn_up": w_ffn_up,
        "ffn_conv_w": pad_rows(ffn_conv_w), "ffn_conv_b": row(ffn_conv_b),
        "w_ffn_down": w_ffn_down,
    }


def kernel(x_prompt, x_sample, cache_sb_k, cache_sb_v, state_lru_h, state_lru_conv, state_ffn_conv, norm_mix_g, w_in, lru_conv_w, lru_conv_b, lru_w_a, lru_b_a, lru_w_x, lru_b_x, lru_lambda, q_norm_g, k_norm_g, lru_out_g, sb_out_g, w_out, norm_ffn_g, w_ffn_up, ffn_conv_w, ffn_conv_b, w_ffn_down):
    depth = w_in.shape[0]
    bp = x_prompt.shape[0]
    tri = jnp.tri(SB_TILE, k=-1, dtype=BF16)
    xp, xs = x_prompt, x_sample
    st_p, st_s = [], []
    for l in range(depth):
        wts = _prep_weights(norm_mix_g[l], w_in[l], lru_conv_w[l], lru_conv_b[l], lru_w_a[l], lru_b_a[l],
                            lru_w_x[l], lru_b_x[l], lru_lambda[l], q_norm_g[l], k_norm_g[l], lru_out_g[l],
                            sb_out_g[l], w_out[l], norm_ffn_g[l], w_ffn_up[l], ffn_conv_w[l],
                            ffn_conv_b[l], w_ffn_down[l])
        xs, *ss, wb = _layer(xs, cache_sb_k[l], cache_sb_v[l], state_lru_h[l], state_lru_conv[l],
                             state_ffn_conv[l], wts, tri, tm=xs.shape[0] * xs.shape[1],
                             tt_lru=xs.shape[1], tt_ffn=xs.shape[1], emit=True)
        st_s.append(ss)
        xp, *sp, _ = _layer(xp, None, None,
                            jnp.zeros((bp, D_LRU), F32),
                            jnp.zeros((bp, LRU_CONV - 1, D_LRU), F32),
                            jnp.zeros((bp, FFN_CONV - 1, D_FF), F32),
                            dict(wts, **wb), tri, tm=512, tt_lru=128, tt_ffn=256)
        st_p.append(sp)
    stack = lambda sts, i: jnp.stack([s[i] for s in sts])
    return (xp, xs) + tuple(stack(st_p, i) for i in range(5)) + tuple(stack(st_s, i) for i in range(5))
```

```python
import functools
import math

import jax
import jax.numpy as jnp
from jax import lax
from jax.experimental import pallas as pl
from jax.experimental.pallas import tpu as pltpu

F32 = jnp.float32
BF16 = jnp.bfloat16

D_MODEL = 4096
HEAD_DIM = 128
N_HEADS = 8
D_SB = N_HEADS * HEAD_DIM
D_LRU = D_MODEL - D_SB
LRU_BLOCKS = 16
LRU_BLOCK_DIM = D_LRU // LRU_BLOCKS
LRU_PAIR = 2 * LRU_BLOCK_DIM
N_PAIRS = LRU_BLOCKS // 2
LRU_CONV = 4
LRU_C = 8.0
D_FF = 3 * D_MODEL
FFN_CONV = 3
EPS = 1e-6
LOG2E = math.log2(math.e)
Q_SCALE = HEAD_DIM ** -0.5 * LOG2E
HALO = 8
SB_TILE = 256
SB_HEADS_PER_STEP = 2
MXU_COLS = 256
ROW_TILE_PREFETCH = 1024

VMEM_LIMIT = 60 * 1024 * 1024


def _params(*sem):
    return pltpu.CompilerParams(dimension_semantics=sem, vmem_limit_bytes=VMEM_LIMIT)


def _rms(x, g):
    ms = jnp.mean(x * x, axis=-1, keepdims=True)
    return (x * lax.rsqrt(ms + EPS)) * g


def _rms_rows(x_ref, g_ref, o_ref, chunk=256):
    g = g_ref[...]
    for r in range(0, x_ref.shape[0], chunk):
        rows = slice(r, min(r + chunk, x_ref.shape[0]))
        o_ref[rows, :] = _rms(x_ref[rows, :], g).astype(BF16)


def _gelu(x):
    return x * (0.5 * (1.0 + jnp.tanh(0.7978845608028654 * (x + 0.044715 * (x * x * x)))))


def _sigmoid(x):
    return 1.0 / (1.0 + jnp.exp(-x))


def _softplus(x):
    return jnp.maximum(x, 0.0) + jnp.log1p(jnp.exp(-jnp.abs(x)))


def _weight_block(w_ref, wb_ref):
    w = w_ref[...]
    if wb_ref is not None:
        w = w.astype(BF16)
        wb_ref[...] = w
    return w


def _rows_copy(x_hbm, xbuf, sem, i):
    tm = xbuf.shape[0]
    return pltpu.make_async_copy(x_hbm.at[pl.ds(pl.multiple_of(i * tm, tm), tm), :], xbuf, sem)


def _norm_row_tile(x_ref, g_ref, hn_ref, prefetch):
    i, j = pl.program_id(0), pl.program_id(1)
    if prefetch is None:
        @pl.when(j == 0)
        def _():
            _rms_rows(x_ref, g_ref, hn_ref)
        return
    xbuf, sem = prefetch

    @pl.when(j == 0)
    def _():
        @pl.when(i == 0)
        def _():
            _rows_copy(x_ref, xbuf, sem, 0).start()

        _rows_copy(x_ref, xbuf, sem, i).wait()
        _rms_rows(xbuf, g_ref, hn_ref)

    @pl.when(jnp.logical_and(j == 1, i + 1 < pl.num_programs(0)))
    def _():
        _rows_copy(x_ref, xbuf, sem, i + 1).start()


def _rows_operand(tm, k, ncols):
    if tm >= ROW_TILE_PREFETCH and ncols >= 2:
        return pl.BlockSpec(memory_space=pl.ANY), [pltpu.VMEM((tm, k), F32), pltpu.SemaphoreType.DMA(())]
    return pl.BlockSpec((tm, k), lambda i, j: (i, 0)), []


def _norm_matmul_body(x_ref, g_ref, w_ref, o_ref, *rest, emit, prefetch):
    wb_ref, hn_ref, *pre = rest if emit else (None,) + rest
    _norm_row_tile(x_ref, g_ref, hn_ref, tuple(pre) if prefetch else None)
    o_ref[...] = jnp.dot(hn_ref[...], _weight_block(w_ref, wb_ref), preferred_element_type=F32)


def _norm_matmul(x, g, w, tm, tn, name, emit=False):
    m, k = x.shape
    n = w.shape[1]
    assert not emit or m == tm
    xspec, xscratch = _rows_operand(tm, k, n // tn)
    wspec = pl.BlockSpec((k, tn), lambda i, j: (0, j))
    ospec = pl.BlockSpec((tm, tn), lambda i, j: (i, j))
    oshape = jax.ShapeDtypeStruct((m, n), F32)
    return pl.pallas_call(
        functools.partial(_norm_matmul_body, emit=emit, prefetch=bool(xscratch)),
        grid=(m // tm, n // tn),
        in_specs=[
            xspec,
            pl.BlockSpec((1, k), lambda i, j: (0, 0)),
            wspec,
        ],
        out_specs=[ospec, wspec] if emit else ospec,
        out_shape=[oshape, jax.ShapeDtypeStruct((k, n), BF16)] if emit else oshape,
        scratch_shapes=[pltpu.VMEM((tm, k), BF16)] + xscratch,
        compiler_params=_params("arbitrary", "arbitrary"),
        name=name,
    )(x, g, w)


def _outproj_body(a_ref, y_ref, gsb_ref, w1_ref, w2_ref, res_ref, o_ref, *rest, emit):
    w1b_ref, w2b_ref, yn_ref = rest if emit else (None, None) + rest

    @pl.when(pl.program_id(1) == 0)
    def _():
        _rms_rows(y_ref, gsb_ref, yn_ref)

    acc = jnp.dot(a_ref[...], _weight_block(w1_ref, w1b_ref), preferred_element_type=F32)
    acc = acc + jnp.dot(yn_ref[...], _weight_block(w2_ref, w2b_ref), preferred_element_type=F32)
    o_ref[...] = res_ref[...] + acc


def _outproj(a, y_sb, g_sb, w1, w2, res, tm, tn, emit=False):
    m = a.shape[0]
    n = w1.shape[1]
    assert not emit or m == tm
    w2_row_block = (w2.shape[0] - D_SB) // D_SB
    ospec = pl.BlockSpec((tm, tn), lambda i, j: (i, j))
    oshape = jax.ShapeDtypeStruct((m, n), F32)
    wbspecs = [pl.BlockSpec((D_LRU, tn), lambda i, j: (0, j)), pl.BlockSpec((D_SB, tn), lambda i, j: (0, j))]
    wbshapes = [jax.ShapeDtypeStruct((D_LRU, n), BF16), jax.ShapeDtypeStruct((D_SB, n), BF16)]
    return pl.pallas_call(
        functools.partial(_outproj_body, emit=emit),
        grid=(m // tm, n // tn),
        in_specs=[
            pl.BlockSpec((tm, D_LRU), lambda i, j: (i, 0)),
            pl.BlockSpec((tm, D_SB), lambda i, j: (i, 0)),
            pl.BlockSpec((1, D_SB), lambda i, j: (0, 0)),
            pl.BlockSpec((D_LRU, tn), lambda i, j: (0, j)),
            pl.BlockSpec((D_SB, tn), lambda i, j: (w2_row_block, j)),
            ospec,
        ],
        out_specs=[ospec] + wbspecs if emit else ospec,
        out_shape=[oshape] + wbshapes if emit else oshape,
        scratch_shapes=[pltpu.VMEM((tm, D_SB), BF16)],
        compiler_params=_params("parallel", "arbitrary"),
        name="out_proj",
    )(a, y_sb, g_sb, w1, w2, res)


def _down_body(g_ref, w_ref, res_ref, o_ref, *rest, emit):
    wb_ref, acc_ref = rest if emit else (None,) + rest
    k = pl.program_id(2)

    @pl.when(k == 0)
    def _():
        acc_ref[...] = jnp.zeros_like(acc_ref)

    acc_ref[...] += jnp.dot(g_ref[...], _weight_block(w_ref, wb_ref), preferred_element_type=F32)

    @pl.when(k == pl.num_programs(2) - 1)
    def _():
        o_ref[...] = res_ref[...] + acc_ref[...]


def _down(g, w, res, tm, tn, tk, emit=False):
    m, kdim = g.shape
    n = w.shape[1]
    assert not emit or m == tm
    wspec = pl.BlockSpec((tk, tn), lambda i, j, k: (k, j))
    ospec = pl.BlockSpec((tm, tn), lambda i, j, k: (i, j))
    oshape = jax.ShapeDtypeStruct((m, n), F32)
    return pl.pallas_call(
        functools.partial(_down_body, emit=emit),
        grid=(m // tm, n // tn, kdim // tk),
        in_specs=[pl.BlockSpec((tm, tk), lambda i, j, k: (i, k)), wspec, ospec],
        out_specs=[ospec, wspec] if emit else ospec,
        out_shape=[oshape, jax.ShapeDtypeStruct((kdim, n), BF16)] if emit else oshape,
        scratch_shapes=[pltpu.VMEM((tm, tn), F32)],
        compiler_params=_params("parallel", "parallel", "arbitrary"),
        name="ffn_down",
    )(g, w, res)


def _lru_body(u_ref, gate_ref, h0_ref, cprev_ref, cw_ref, cb_ref, wa_ref, ba_ref, wx_ref, bx_ref,
              lam_ref, og_ref, y_ref, hlast_ref, cstate_ref,
              ext_ref, uc_ref, a_ref, b_ref, hs_ref, hc_ref, *, tt):
    @pl.when(pl.program_id(1) == 0)
    def _():
        hc_ref[...] = h0_ref[0]
        ext_ref[0:HALO, :] = cprev_ref[0]

    u = u_ref[...]
    ext_ref[HALO:HALO + tt, :] = u
    cw = cw_ref[...]
    uc = cb_ref[...]
    for k in range(LRU_CONV - 1):
        lo = HALO - (LRU_CONV - 1) + k
        uc = uc + ext_ref[lo:lo + tt, :] * cw[k:k + 1, :]
    uc_ref[...] = uc + u * cw[LRU_CONV - 1:LRU_CONV, :]

    last = ext_ref[tt:tt + HALO, :]
    cstate_ref[0] = last
    ext_ref[0:HALO, :] = last

    for p in range(N_PAIRS):
        sl = slice(p * LRU_PAIR, (p + 1) * LRU_PAIR)
        ucp = uc_ref[:, sl]
        xb = ucp.astype(BF16)
        r = _sigmoid(jnp.dot(xb, wa_ref[p], preferred_element_type=F32) + ba_ref[:, sl])
        i = _sigmoid(jnp.dot(xb, wx_ref[p], preferred_element_type=F32) + bx_ref[:, sl])
        log_a = (LRU_C * r) * (-_softplus(-lam_ref[:, sl]))
        a = jnp.exp(log_a)
        a_ref[:, sl] = a
        b_ref[:, sl] = jnp.sqrt(-jnp.tanh(log_a) * (a * a + 1.0)) * (i * ucp)

    def step(t, h):
        h = a_ref[pl.ds(t, 1), :] * h + b_ref[pl.ds(t, 1), :]
        hs_ref[pl.ds(t, 1), :] = h
        return h

    h = lax.fori_loop(0, tt, step, hc_ref[...], unroll=8)
    hc_ref[...] = h
    hlast_ref[0] = h

    y = hs_ref[...] * _gelu(gate_ref[...])
    y_ref[...] = _rms(y, og_ref[...]).astype(BF16)


def _lru(proj, h0, cprev, wts, nseq, seqlen, tt):
    m = nseq * seqlen
    nt = seqlen // tt
    row = lambda b, t: (b * nt + t, 0)
    vec = lambda b, t: (0, 0)
    return pl.pallas_call(
        functools.partial(_lru_body, tt=tt),
        grid=(nseq, nt),
        in_specs=[
            pl.BlockSpec((tt, D_LRU), row),
            pl.BlockSpec((tt, D_LRU), lambda b, t: (b * nt + t, 1)),
            pl.BlockSpec((1, 1, D_LRU), lambda b, t: (b, 0, 0)),
            pl.BlockSpec((1, HALO, D_LRU), lambda b, t: (b, 0, 0)),
            pl.BlockSpec((HALO, D_LRU), vec),
            pl.BlockSpec((1, D_LRU), vec),
            pl.BlockSpec((N_PAIRS, LRU_PAIR, LRU_PAIR), lambda b, t: (0, 0, 0)),
            pl.BlockSpec((1, D_LRU), vec),
            pl.BlockSpec((N_PAIRS, LRU_PAIR, LRU_PAIR), lambda b, t: (0, 0, 0)),
            pl.BlockSpec((1, D_LRU), vec),
            pl.BlockSpec((1, D_LRU), vec),
            pl.BlockSpec((1, D_LRU), vec),
        ],
        out_specs=[
            pl.BlockSpec((tt, D_LRU), row),
            pl.BlockSpec((1, 1, D_LRU), lambda b, t: (b, 0, 0)),
            pl.BlockSpec((1, HALO, D_LRU), lambda b, t: (b, 0, 0)),
        ],
        out_shape=[
            jax.ShapeDtypeStruct((m, D_LRU), BF16),
            jax.ShapeDtypeStruct((nseq, 1, D_LRU), F32),
            jax.ShapeDtypeStruct((nseq, HALO, D_LRU), F32),
        ],
        scratch_shapes=[
            pltpu.VMEM((tt + HALO, D_LRU), F32),
            pltpu.VMEM((tt, D_LRU), F32),
            pltpu.VMEM((tt, D_LRU), F32),
            pltpu.VMEM((tt, D_LRU), F32),
            pltpu.VMEM((tt, D_LRU), F32),
            pltpu.VMEM((1, D_LRU), F32),
        ],
        compiler_params=_params("arbitrary", "arbitrary"),
        name="rg_lru",
    )(proj, proj, h0, cprev, wts["lru_conv_w"], wts["lru_conv_b"], wts["lru_w_a"], wts["lru_b_a"],
      wts["lru_w_x"], wts["lru_b_x"], wts["lru_lambda"], wts["lru_out_g"])


def _qkv_body(q_ref, k_ref, v_ref, gq_ref, gk_ref, qb_ref, kf_ref, kb_ref, vf_ref, vb_ref):
    for h in range(N_HEADS):
        sl = slice(h * HEAD_DIM, (h + 1) * HEAD_DIM)
        qb_ref[:, sl] = (_rms(q_ref[:, sl], gq_ref[...]) * Q_SCALE).astype(BF16)
        kn = _rms(k_ref[:, sl], gk_ref[...])
        kf_ref[:, sl] = kn
        kb_ref[:, sl] = kn.astype(BF16)
    v = v_ref[...]
    vf_ref[...] = v
    vb_ref[...] = v.astype(BF16)


def _qkv(proj, gq, gk, tm):
    m = proj.shape[0]
    col0 = 2 * D_LRU // D_SB
    blk = lambda c: pl.BlockSpec((tm, D_SB), lambda i: (i, c))
    vec = pl.BlockSpec((1, HEAD_DIM), lambda i: (0, 0))
    out = pl.BlockSpec((tm, D_SB), lambda i: (i, 0))
    return pl.pallas_call(
        _qkv_body,
        grid=(m // tm,),
        in_specs=[blk(col0), blk(col0 + 1), blk(col0 + 2), vec, vec],
        out_specs=[out] * 5,
        out_shape=[jax.ShapeDtypeStruct((m, D_SB), d) for d in (BF16, F32, BF16, F32, BF16)],
        compiler_params=_params("parallel"),
        name="qkv_norm",
    )(proj, proj, proj, gq, gk)


def _neg_abs(x):
    return lax.bitcast_convert_type(lax.bitcast_convert_type(x, jnp.int32) | jnp.int32(-2 ** 31), F32)


def _sb_step(q, k, v, tri, c_ref, crows, acc_ref, arows, acols, masked):
    z = lax.dot_general(q, k, (((1,), (1,)), ((), ())), preferred_element_type=F32)
    sp = jnp.maximum(z, 0.0) + jnp.log(1.0 + jnp.exp2(_neg_abs(z))) * LOG2E
    if masked:
        mask = (lax.broadcasted_iota(jnp.int32, z.shape, 1) < lax.broadcasted_iota(jnp.int32, z.shape, 0))
        sp = jnp.where(mask, sp, 0.0)
    inner = jnp.dot(sp.astype(BF16), tri, preferred_element_type=F32)
    c = c_ref[crows, :]
    w = jnp.exp2((z - sp) - (inner + c))
    if masked:
        w = jnp.where(mask, w, 0.0)
    acc_ref[arows, acols] += jnp.dot(w.astype(BF16), v, preferred_element_type=F32)
    c_ref[crows, :] = c + jnp.sum(sp, axis=-1, keepdims=True)


NEG_BIG = -1e30
SKIP_LOG2 = 160.0


def _sb_logits(q, k, z_ref, slot):
    z_ref[slot] = lax.dot_general(q, k, (((1,), (1,)), ((), ())), preferred_element_type=F32)


def _sb_scores(z_ref, c_ref, crows, t_ref, spb_ref, slot, mode):
    if mode == "none":
        t_ref[slot] = jnp.full(t_ref.shape[1:], NEG_BIG, F32)
        spb_ref[slot] = jnp.zeros(spb_ref.shape[1:], BF16)
        return
    z = z_ref[slot]
    sp = jnp.maximum(z, 0.0) + jnp.log(1.0 + jnp.exp2(_neg_abs(z))) * LOG2E
    c = c_ref[crows, :]
    t = (z - sp) - c
    if mode == "diag":
        mask = (lax.broadcasted_iota(jnp.int32, z.shape, 1) < lax.broadcasted_iota(jnp.int32, z.shape, 0))
        sp = jnp.where(mask, sp, 0.0)
        t = jnp.where(mask, t, NEG_BIG)
    t_ref[slot] = t
    spb_ref[slot] = sp.astype(BF16)
    c_ref[crows, :] = c + jnp.sum(sp, axis=-1, keepdims=True)


def _sb_weights(tri, t_ref, spb_ref, wb_ref, slot):
    inner = jnp.dot(spb_ref[slot], tri, preferred_element_type=F32)
    wb_ref[slot] = jnp.exp2(t_ref[slot] - inner).astype(BF16)


def _sb_values(v, wb_ref, slot, acc_ref, arows, acols):
    acc_ref[arows, acols] += jnp.dot(wb_ref[slot], v, preferred_element_type=F32)


def _attn_prompt_body(q_ref, k_ref, v_ref, tri_ref, o_ref, c_ref, z_ref, t_ref, spb_ref, wb_ref, *, nsub, nhead):
    ts = SB_TILE
    base = pl.program_id(1) * nsub
    tri = tri_ref[...]
    c_ref[...] = jnp.zeros_like(c_ref)
    o_ref[...] = jnp.zeros_like(o_ref)
    rows = [pl.ds(s * ts, ts) for s in range(nsub)]
    hcols = [slice(h * HEAD_DIM, (h + 1) * HEAD_DIM) for h in range(nhead)]
    chains = [(h, s) for h in range(nhead) for s in range(nsub)]
    crows = [pl.ds(u * ts, ts) for u in range(len(chains))]
    qs = [q_ref[rows[s], hcols[h]] for h, s in chains]

    @pl.when(base == 0)
    def _():
        for t in reversed(range(nsub)):
            for u, (h, s) in enumerate(chains):
                if s >= t:
                    k = k_ref[t * ts:(t + 1) * ts, hcols[h]]
                    v = v_ref[t * ts:(t + 1) * ts, hcols[h]]
                    _sb_step(qs[u], k, v, tri, c_ref, crows[u], o_ref, rows[s], hcols[h], masked=(s == t))

    @pl.when(base > 0)
    def _():
        n = base + nsub

        def tile(i):
            return pl.ds(pl.multiple_of((n - 1 - i) * ts, ts), ts)

        def modes(i):
            kt = nsub - 1 - i
            return tuple("full" if (kt < 0 or s > kt) else ("diag" if s == kt else "none")
                         for s in range(nsub))

        def iteration(i, p, first=1, last=4, static_i=None):
            if last >= 4 and first <= 4:
                vs = [v_ref[tile(i - 3), hc] for hc in hcols]
                for u, (h, s) in enumerate(chains):
                    _sb_values(vs[h], wb_ref.at[1 - p], u, o_ref, rows[s], hcols[h])
            if last >= 3 and first <= 3:
                for u in range(len(chains)):
                    _sb_weights(tri, t_ref.at[1 - p], spb_ref.at[1 - p], wb_ref.at[p], u)
            if last >= 2 and first <= 2:
                md = modes(static_i - 1) if static_i is not None else ("full",) * nsub
                for u, (h, s) in enumerate(chains):
                    _sb_scores(z_ref.at[1 - p], c_ref, crows[u], t_ref.at[p], spb_ref.at[p], u, md[s])
            if last >= 1 and first <= 1:
                ks = [k_ref[tile(i), hc] for hc in hcols]
                for u, (h, s) in enumerate(chains):
                    _sb_logits(qs[u], ks[h], z_ref.at[p], u)

        nfill = 4
        assert nsub == 2
        for i in range(nfill):
            iteration(i, i & 1, last=min(i + 1, 4), static_i=i)

        def cond(carry):
            j, cmin = carry
            return jnp.logical_and(j < (n - nfill) // 2, cmin < SKIP_LOG2)

        def body(carry):
            j, _ = carry
            i = nfill + 2 * j
            iteration(i, 0)
            iteration(i + 1, 1)
            return j + 1, jnp.min(c_ref[...])

        trips, _ = lax.while_loop(cond, body, (jnp.int32(0), jnp.min(c_ref[...])))
        issued = nfill + 2 * trips
        for d in range(3):
            iteration(issued + d, d & 1, first=d + 2)


def _attn_prompt(qb, kb, vb, tri, nsub):
    t = qb.shape[0]
    tq = nsub * SB_TILE
    nhead = SB_HEADS_PER_STEP
    hw = nhead * HEAD_DIM
    nchain = nhead * nsub
    return pl.pallas_call(
        functools.partial(_attn_prompt_body, nsub=nsub, nhead=nhead),
        grid=(N_HEADS // nhead, t // tq),
        in_specs=[
            pl.BlockSpec((tq, hw), lambda h, i: (i, h)),
            pl.BlockSpec((t, hw), lambda h, i: (0, h)),
            pl.BlockSpec((t, hw), lambda h, i: (0, h)),
            pl.BlockSpec((SB_TILE, SB_TILE), lambda h, i: (0, 0)),
        ],
        out_specs=pl.BlockSpec((tq, hw), lambda h, i: (i, h)),
        out_shape=jax.ShapeDtypeStruct((t, D_SB), F32),
        scratch_shapes=[pltpu.VMEM((nchain * SB_TILE, 1), F32),
                        pltpu.VMEM((2, nchain, SB_TILE, SB_TILE), F32),
                        pltpu.VMEM((2, nchain, SB_TILE, SB_TILE), F32),
                        pltpu.VMEM((2, nchain, SB_TILE, SB_TILE), BF16),
                        pltpu.VMEM((2, nchain, SB_TILE, SB_TILE), BF16)],
        compiler_params=_params("parallel", "arbitrary"),
        name="sb_attn_prompt",
    )(qb, kb, vb, tri)


def _attn_sample_body(q_ref, kn_ref, vn_ref, kc_ref, vc_ref, trin_ref, tri_ref, o_ref, c_ref, *, tq, chunk):
    ts = SB_TILE
    tri = tri_ref[...]
    rows = [pl.ds(h * tq, tq) for h in range(N_HEADS)]
    cols = [slice(h * HEAD_DIM, (h + 1) * HEAD_DIM) for h in range(N_HEADS)]
    qs = [q_ref[:, cl] for cl in cols]

    @pl.when(pl.program_id(1) == 0)
    def _():
        c_ref[...] = jnp.zeros_like(c_ref)
        o_ref[...] = jnp.zeros_like(o_ref)
        trin = trin_ref[...]
        for h in range(N_HEADS):
            _sb_step(qs[h], kn_ref[:, cols[h]], vn_ref[:, cols[h]], trin, c_ref, rows[h], o_ref, slice(None), cols[h], True)

    for t in reversed(range(chunk // ts)):
        @pl.when(jnp.min(c_ref[...]) < SKIP_LOG2)
        def _():
            for h in range(N_HEADS):
                sel = pl.ds(t * ts * N_HEADS + h, ts, stride=N_HEADS)
                k = kc_ref[sel, :].astype(BF16)
                v = vc_ref[sel, :].astype(BF16)
                _sb_step(qs[h], k, v, tri, c_ref, rows[h], o_ref, slice(None), cols[h], False)


def _attn_sample(qb, kb, vb, kc, vc, tri_new, tri, nseq, tq, chunk):
    nchunk = kc.shape[1] // (chunk * N_HEADS)
    new = pl.BlockSpec((tq, D_SB), lambda b, c: (b, 0))
    cache = pl.BlockSpec((None, chunk * N_HEADS, HEAD_DIM), lambda b, c: (b, nchunk - 1 - c, 0))
    return pl.pallas_call(
        functools.partial(_attn_sample_body, tq=tq, chunk=chunk),
        grid=(nseq, nchunk),
        in_specs=[new, new, new, cache, cache,
                  pl.BlockSpec((tq, tq), lambda b, c: (0, 0)),
                  pl.BlockSpec((SB_TILE, SB_TILE), lambda b, c: (0, 0))],
        out_specs=new,
        out_shape=jax.ShapeDtypeStruct((nseq * tq, D_SB), F32),
        scratch_shapes=[pltpu.VMEM((N_HEADS * tq, 1), F32)],
        compiler_params=_params("parallel", "arbitrary"),
        name="sb_attn_sample",
    )(qb, kb, vb, kc, vc, tri_new, tri)


def _ffn_conv_act(pre, val, ext_ref, cw_ref, cb_ref, tt, cols=slice(None)):
    ext_ref[HALO:HALO + tt, cols] = pre
    cw = cw_ref[:, cols]
    pc = cb_ref[:, cols]
    for k in range(FFN_CONV - 1):
        lo = HALO - (FFN_CONV - 1) + k
        pc = pc + ext_ref[lo:lo + tt, cols] * cw[k:k + 1, :]
    pc = pc + pre * cw[FFN_CONV - 1:FFN_CONV, :]
    return (_gelu(pc) * val).astype(BF16), ext_ref[tt:tt + HALO, cols]


def _ffn_act_body(val_ref, pre_ref, sprev_ref, cw_ref, cb_ref, g_ref, sout_ref, ext_ref, *, tt):
    @pl.when(pl.program_id(2) == 0)
    def _():
        ext_ref[0:HALO, :] = sprev_ref[0]

    g, last = _ffn_conv_act(pre_ref[...], val_ref[...], ext_ref, cw_ref, cb_ref, tt)
    g_ref[...] = g
    sout_ref[0] = last
    ext_ref[0:HALO, :] = last


def _ffn_act(up, sprev, cw, cb, nseq, seqlen, tt, tc):
    m = nseq * seqlen
    nt = seqlen // tt
    nc = D_FF // tc
    return pl.pallas_call(
        functools.partial(_ffn_act_body, tt=tt),
        grid=(nseq, nc, nt),
        in_specs=[
            pl.BlockSpec((tt, tc), lambda b, c, t: (b * nt + t, c)),
            pl.BlockSpec((tt, tc), lambda b, c, t: (b * nt + t, nc + c)),
            pl.BlockSpec((1, HALO, tc), lambda b, c, t: (b, 0, c)),
            pl.BlockSpec((HALO, tc), lambda b, c, t: (0, c)),
            pl.BlockSpec((1, tc), lambda b, c, t: (0, c)),
        ],
        out_specs=[
            pl.BlockSpec((tt, tc), lambda b, c, t: (b * nt + t, c)),
            pl.BlockSpec((1, HALO, tc), lambda b, c, t: (b, 0, c)),
        ],
        out_shape=[
            jax.ShapeDtypeStruct((m, D_FF), BF16),
            jax.ShapeDtypeStruct((nseq, HALO, D_FF), F32),
        ],
        scratch_shapes=[pltpu.VMEM((tt + HALO, tc), F32)],
        compiler_params=_params("parallel", "parallel", "arbitrary"),
        name="ffn_act",
    )(up, up, sprev, cw, cb)


def _ffn_up_act_body(x_ref, g_ref, wv_ref, wp_ref, sprev_ref, cw_ref, cb_ref, o_ref, last_ref,
                     hn_ref, halo_ref, ext_ref, *pre, tm):
    i = pl.program_id(0)
    j = pl.program_id(1)
    _norm_row_tile(x_ref, g_ref, hn_ref, tuple(pre) if pre else None)

    @pl.when(i == 0)
    def _():
        ext_ref[0:HALO, :] = sprev_ref[...]

    @pl.when(i > 0)
    def _():
        ext_ref[0:HALO, :] = halo_ref[j]

    hn = hn_ref[...]
    tn = o_ref.shape[1]
    halves = [slice(h * MXU_COLS, (h + 1) * MXU_COLS) for h in range(tn // MXU_COLS)]
    pres = [jnp.dot(hn, wp_ref[:, cs], preferred_element_type=F32) for cs in halves]
    vals = [jnp.dot(hn, wv_ref[:, cs], preferred_element_type=F32) for cs in halves]
    for cs, pre, val in zip(halves, pres, vals):
        g, last = _ffn_conv_act(pre, val, ext_ref, cw_ref, cb_ref, tm, cs)
        o_ref[:, cs] = g
        halo_ref[j, :, cs] = last
        last_ref[0, :, cs] = last


def _ffn_up_act(x, g, w_up, sprev, cw, cb, tm, tn):
    m, k = x.shape
    nj = D_FF // tn
    xspec, xscratch = _rows_operand(tm, k, nj)
    return pl.pallas_call(
        functools.partial(_ffn_up_act_body, tm=tm),
        grid=(m // tm, nj),
        in_specs=[
            xspec,
            pl.BlockSpec((1, k), lambda i, j: (0, 0)),
            pl.BlockSpec((k, tn), lambda i, j: (0, j)),
            pl.BlockSpec((k, tn), lambda i, j: (0, nj + j)),
            pl.BlockSpec((HALO, tn), lambda i, j: (0, j)),
            pl.BlockSpec((HALO, tn), lambda i, j: (0, j)),
            pl.BlockSpec((1, tn), lambda i, j: (0, j)),
        ],
        out_specs=[
            pl.BlockSpec((tm, tn), lambda i, j: (i, j)),
            pl.BlockSpec((1, HALO, tn), lambda i, j: (i, 0, j)),
        ],
        out_shape=[
            jax.ShapeDtypeStruct((m, D_FF), BF16),
            jax.ShapeDtypeStruct((m // tm, HALO, D_FF), F32),
        ],
        scratch_shapes=[
            pltpu.VMEM((tm, k), BF16),
            pltpu.VMEM((nj, HALO, tn), F32),
            pltpu.VMEM((tm + HALO, tn), F32),
        ] + xscratch,
        compiler_params=_params("arbitrary", "arbitrary"),
        name="ffn_up_act",
    )(x, g, w_up, w_up, sprev, cw, cb)


def _front_pad(x, rows):
    return jnp.pad(x, ((0, 0), (rows - x.shape[1], 0), (0, 0)))


def _layer(x3, kpast, vpast, h0, cprev, fprev, wts, tri, tm, tt_lru, tt_ffn, attn_nsub=2, fuse_ffn=None,
           emit=False):
    nseq, seqlen, _ = x3.shape
    m = nseq * seqlen
    x = x3.reshape(m, D_MODEL)
    if fuse_ffn is None:
        fuse_ffn = nseq == 1
    assert not (emit and fuse_ffn)
    tn = 512 if emit else 1024
    tm_wide = ROW_TILE_PREFETCH if m % ROW_TILE_PREFETCH == 0 else tm
    wb = {}

    proj = _norm_matmul(x, wts["norm_mix_g"], wts["w_in"], tm_wide, 512, "in_proj", emit)
    if emit:
        proj, wb["w_in"] = proj
    y_lru, h_last, cstate = _lru(proj, h0[:, None, :], _front_pad(cprev, HALO), wts, nseq, seqlen, tt_lru)
    qb, kf, kb, vf, vb = _qkv(proj, wts["q_norm_g"], wts["k_norm_g"], tm)
    if kpast is None:
        y_sb = _attn_prompt(qb, kb, vb, tri, attn_nsub)
    else:
        past = kpast.shape[1]
        y_sb = _attn_sample(qb, kb, vb, kpast.reshape(nseq, past * N_HEADS, HEAD_DIM),
                            vpast.reshape(nseq, past * N_HEADS, HEAD_DIM),
                            tri[:seqlen, :seqlen], tri, nseq, seqlen, min(past, 1024))
    if emit:
        x1, wb["w_out_lru"], wb["w_out_sb"] = _outproj(y_lru, y_sb, wts["sb_out_g"], wts["w_out"], wts["w_out"],
                                                        x, tm, tn, emit)
    else:
        x1 = _outproj(y_lru, y_sb, wts["sb_out_g"], wts["w_out_lru"], wts["w_out_sb"], x, tm_wide, 512)

    fprev8 = _front_pad(fprev, HALO)
    if fuse_ffn:
        g, lasts = _ffn_up_act(x1, wts["norm_ffn_g"], wts["w_ffn_up"], fprev8[0], wts["ffn_conv_w"],
                               wts["ffn_conv_b"], tm_wide, 512)
        fstate = lasts[-1:]
    else:
        up = _norm_matmul(x1, wts["norm_ffn_g"], wts["w_ffn_up"], tm, tn, "ffn_up", emit)
        if emit:
            up, wb["w_ffn_up"] = up
        g, fstate = _ffn_act(up, fprev8, wts["ffn_conv_w"], wts["ffn_conv_b"], nseq, seqlen, tt_ffn,
                             D_FF if tt_ffn <= 64 else 2048)
    out = _down(g, wts["w_ffn_down"], x1, min(m, 1024), 1024, 2048 if emit else 3072, emit)
    if emit:
        out, wb["w_ffn_down"] = out

    return (out.reshape(nseq, seqlen, D_MODEL),
            kf.reshape(nseq, seqlen, N_HEADS, HEAD_DIM),
            vf.reshape(nseq, seqlen, N_HEADS, HEAD_DIM),
            h_last[:, 0, :],
            cstate[:, HALO - (LRU_CONV - 1):, :],
            fstate[:, HALO - (FFN_CONV - 1):, :],
            wb)


def _pair_blocks(w):
    z = jnp.zeros((N_PAIRS, LRU_BLOCK_DIM, LRU_BLOCK_DIM), w.dtype)
    top = jnp.concatenate([w[0::2], z], axis=2)
    bot = jnp.concatenate([z, w[1::2]], axis=2)
    return jnp.concatenate([top, bot], axis=1).astype(BF16)


def _prep_weights(norm_mix_g, w_in, lru_conv_w, lru_conv_b, lru_w_a, lru_b_a, lru_w_x, lru_b_x,
                  lru_lambda, q_norm_g, k_norm_g, lru_out_g, sb_out_g, w_out, norm_ffn_g,
                  w_ffn_up, ffn_conv_w, ffn_conv_b, w_ffn_down):
    row = lambda v: v.reshape(1, -1)
    pad_rows = lambda w: jnp.pad(w, ((0, HALO - w.shape[0]), (0, 0)))
    return {
        "norm_mix_g": row(norm_mix_g), "w_in": w_in,
        "lru_conv_w": pad_rows(lru_conv_w), "lru_conv_b": row(lru_conv_b),
        "lru_w_a": _pair_blocks(lru_w_a), "lru_b_a": row(lru_b_a),
        "lru_w_x": _pair_blocks(lru_w_x), "lru_b_x": row(lru_b_x),
        "lru_lambda": row(lru_lambda),
        "q_norm_g": row(q_norm_g), "k_norm_g": row(k_norm_g),
        "lru_out_g": row(lru_out_g), "sb_out_g": row(sb_out_g), "w_out": w_out,
        "norm_ffn_g": row(norm_ffn_g), "w_ffn_up": w_ffn_up,
        "ffn_conv_w": pad_rows(ffn_conv_w), "ffn_conv_b": row(ffn_conv_b),
        "w_ffn_down": w_ffn_down,
    }


def kernel(x_prompt, x_sample, cache_sb_k, cache_sb_v, state_lru_h, state_lru_conv, state_ffn_conv, norm_mix_g, w_in, lru_conv_w, lru_conv_b, lru_w_a, lru_b_a, lru_w_x, lru_b_x, lru_lambda, q_norm_g, k_norm_g, lru_out_g, sb_out_g, w_out, norm_ffn_g, w_ffn_up, ffn_conv_w, ffn_conv_b, w_ffn_down):
    depth = w_in.shape[0]
    bp = x_prompt.shape[0]
    tri = jnp.tri(SB_TILE, k=-1, dtype=BF16)
    xp, xs = x_prompt, x_sample
    st_p, st_s = [], []
    for l in range(depth):
        wts = _prep_weights(norm_mix_g[l], w_in[l], lru_conv_w[l], lru_conv_b[l], lru_w_a[l], lru_b_a[l],
                            lru_w_x[l], lru_b_x[l], lru_lambda[l], q_norm_g[l], k_norm_g[l], lru_out_g[l],
                            sb_out_g[l], w_out[l], norm_ffn_g[l], w_ffn_up[l], ffn_conv_w[l],
                            ffn_conv_b[l], w_ffn_down[l])
        xs, *ss, wb = _layer(xs, cache_sb_k[l], cache_sb_v[l], state_lru_h[l], state_lru_conv[l],
                             state_ffn_conv[l], wts, tri, tm=xs.shape[0] * xs.shape[1],
                             tt_lru=xs.shape[1], tt_ffn=xs.shape[1], emit=True)
        st_s.append(ss)
        xp, *sp, _ = _layer(xp, None, None,
                            jnp.zeros((bp, D_LRU), F32),
                            jnp.zeros((bp, LRU_CONV - 1, D_LRU), F32),
                            jnp.zeros((bp, FFN_CONV - 1, D_FF), F32),
                            dict(wts, **wb), tri, tm=512, tt_lru=128, tt_ffn=256)
        st_p.append(sp)
    stack = lambda sts, i: jnp.stack([s[i] for s in sts])
    return (xp, xs) + tuple(stack(st_p, i) for i in range(5)) + tuple(stack(st_s, i) for i in range(5))
```

```python
import functools
import math

import jax
import jax.numpy as jnp
from jax import lax
from jax.experimental import pallas as pl
from jax.experimental.pallas import tpu as pltpu

F32 = jnp.float32
BF16 = jnp.bfloat16

D_MODEL = 4096
HEAD_DIM = 128
N_HEADS = 8
D_SB = N_HEADS * HEAD_DIM
D_LRU = D_MODEL - D_SB
LRU_BLOCKS = 16
LRU_BLOCK_DIM = D_LRU // LRU_BLOCKS
LRU_PAIR = 2 * LRU_BLOCK_DIM
N_PAIRS = LRU_BLOCKS // 2
LRU_CONV = 4
LRU_C = 8.0
D_FF = 3 * D_MODEL
FFN_CONV = 3
EPS = 1e-6
LOG2E = math.log2(math.e)
Q_SCALE = HEAD_DIM ** -0.5 * LOG2E
HALO = 8
SB_TILE = 256
SB_HEADS_PER_STEP = 2
MXU_COLS = 256
ROW_TILE_PREFETCH = 1024

VMEM_LIMIT = 60 * 1024 * 1024


def _params(*sem):
    return pltpu.CompilerParams(dimension_semantics=sem, vmem_limit_bytes=VMEM_LIMIT)


def _rms(x, g):
    ms = jnp.mean(x * x, axis=-1, keepdims=True)
    return (x * lax.rsqrt(ms + EPS)) * g


def _rms_rows(x_ref, g_ref, o_ref, chunk=256):
    g = g_ref[...]
    for r in range(0, x_ref.shape[0], chunk):
        rows = slice(r, min(r + chunk, x_ref.shape[0]))
        o_ref[rows, :] = _rms(x_ref[rows, :], g).astype(BF16)


def _gelu(x):
    return x * (0.5 * (1.0 + jnp.tanh(0.7978845608028654 * (x + 0.044715 * (x * x * x)))))


def _sigmoid(x):
    return 1.0 / (1.0 + jnp.exp(-x))


def _softplus(x):
    return jnp.maximum(x, 0.0) + jnp.log1p(jnp.exp(-jnp.abs(x)))


def _weight_block(w_ref, wb_ref):
    w = w_ref[...]
    if wb_ref is not None:
        w = w.astype(BF16)
        wb_ref[...] = w
    return w


def _rows_copy(x_hbm, xbuf, sem, i):
    tm = xbuf.shape[0]
    return pltpu.make_async_copy(x_hbm.at[pl.ds(pl.multiple_of(i * tm, tm), tm), :], xbuf, sem)


def _norm_row_tile(x_ref, g_ref, hn_ref, prefetch):
    i, j = pl.program_id(0), pl.program_id(1)
    if prefetch is None:
        @pl.when(j == 0)
        def _():
            _rms_rows(x_ref, g_ref, hn_ref)
        return
    xbuf, sem = prefetch

    @pl.when(j == 0)
    def _():
        @pl.when(i == 0)
        def _():
            _rows_copy(x_ref, xbuf, sem, 0).start()

        _rows_copy(x_ref, xbuf, sem, i).wait()
        _rms_rows(xbuf, g_ref, hn_ref)

    @pl.when(jnp.logical_and(j == 1, i + 1 < pl.num_programs(0)))
    def _():
        _rows_copy(x_ref, xbuf, sem, i + 1).start()


def _rows_operand(tm, k, ncols):
    if tm >= ROW_TILE_PREFETCH and ncols >= 2:
        return pl.BlockSpec(memory_space=pl.ANY), [pltpu.VMEM((tm, k), F32), pltpu.SemaphoreType.DMA(())]
    return pl.BlockSpec((tm, k), lambda i, j: (i, 0)), []


def _norm_matmul_body(x_ref, g_ref, w_ref, o_ref, *rest, emit, prefetch):
    wb_ref, hn_ref, *pre = rest if emit else (None,) + rest
    _norm_row_tile(x_ref, g_ref, hn_ref, tuple(pre) if prefetch else None)
    o_ref[...] = jnp.dot(hn_ref[...], _weight_block(w_ref, wb_ref), preferred_element_type=F32)


def _norm_matmul(x, g, w, tm, tn, name, emit=False):
    m, k = x.shape
    n = w.shape[1]
    assert not emit or m == tm
    xspec, xscratch = _rows_operand(tm, k, n // tn)
    wspec = pl.BlockSpec((k, tn), lambda i, j: (0, j))
    ospec = pl.BlockSpec((tm, tn), lambda i, j: (i, j))
    oshape = jax.ShapeDtypeStruct((m, n), F32)
    return pl.pallas_call(
        functools.partial(_norm_matmul_body, emit=emit, prefetch=bool(xscratch)),
        grid=(m // tm, n // tn),
        in_specs=[
            xspec,
            pl.BlockSpec((1, k), lambda i, j: (0, 0)),
            wspec,
        ],
        out_specs=[ospec, wspec] if emit else ospec,
        out_shape=[oshape, jax.ShapeDtypeStruct((k, n), BF16)] if emit else oshape,
        scratch_shapes=[pltpu.VMEM((tm, k), BF16)] + xscratch,
        compiler_params=_params("arbitrary", "arbitrary"),
        name=name,
    )(x, g, w)


def _outproj_body(a_ref, y_ref, gsb_ref, w1_ref, w2_ref, res_ref, o_ref, *rest, emit):
    w1b_ref, w2b_ref, yn_ref = rest if emit else (None, None) + rest

    @pl.when(pl.program_id(1) == 0)
    def _():
        _rms_rows(y_ref, gsb_ref, yn_ref)

    acc = jnp.dot(a_ref[...], _weight_block(w1_ref, w1b_ref), preferred_element_type=F32)
    acc = acc + jnp.dot(yn_ref[...], _weight_block(w2_ref, w2b_ref), preferred_element_type=F32)
    o_ref[...] = res_ref[...] + acc


def _outproj(a, y_sb, g_sb, w1, w2, res, tm, tn, emit=False):
    m = a.shape[0]
    n = w1.shape[1]
    assert not emit or m == tm
    w2_row_block = (w2.shape[0] - D_SB) // D_SB
    ospec = pl.BlockSpec((tm, tn), lambda i, j: (i, j))
    oshape = jax.ShapeDtypeStruct((m, n), F32)
    wbspecs = [pl.BlockSpec((D_LRU, tn), lambda i, j: (0, j)), pl.BlockSpec((D_SB, tn), lambda i, j: (0, j))]
    wbshapes = [jax.ShapeDtypeStruct((D_LRU, n), BF16), jax.ShapeDtypeStruct((D_SB, n), BF16)]
    return pl.pallas_call(
        functools.partial(_outproj_body, emit=emit),
        grid=(m // tm, n // tn),
        in_specs=[
            pl.BlockSpec((tm, D_LRU), lambda i, j: (i, 0)),
            pl.BlockSpec((tm, D_SB), lambda i, j: (i, 0)),
            pl.BlockSpec((1, D_SB), lambda i, j: (0, 0)),
            pl.BlockSpec((D_LRU, tn), lambda i, j: (0, j)),
            pl.BlockSpec((D_SB, tn), lambda i, j: (w2_row_block, j)),
            ospec,
        ],
        out_specs=[ospec] + wbspecs if emit else ospec,
        out_shape=[oshape] + wbshapes if emit else oshape,
        scratch_shapes=[pltpu.VMEM((tm, D_SB), BF16)],
        compiler_params=_params("parallel", "arbitrary"),
        name="out_proj",
    )(a, y_sb, g_sb, w1, w2, res)


def _down_body(g_ref, w_ref, res_ref, o_ref, *rest, emit):
    wb_ref = rest[0] if emit else None
    k = pl.program_id(2)

    @pl.when(k == 0)
    def _():
        o_ref[...] = res_ref[...]

    o_ref[...] += jnp.dot(g_ref[...], _weight_block(w_ref, wb_ref), preferred_element_type=F32)


def _down(g, w, res, tm, tn, tk, emit=False):
    m, kdim = g.shape
    n = w.shape[1]
    assert not emit or m == tm
    wspec = pl.BlockSpec((tk, tn), lambda i, j, k: (k, j))
    ospec = pl.BlockSpec((tm, tn), lambda i, j, k: (i, j))
    oshape = jax.ShapeDtypeStruct((m, n), F32)
    return pl.pallas_call(
        functools.partial(_down_body, emit=emit),
        grid=(m // tm, n // tn, kdim // tk),
        in_specs=[pl.BlockSpec((tm, tk), lambda i, j, k: (i, k)), wspec, ospec],
        out_specs=[ospec, wspec] if emit else ospec,
        out_shape=[oshape, jax.ShapeDtypeStruct((kdim, n), BF16)] if emit else oshape,
        compiler_params=_params("parallel", "parallel", "arbitrary"),
        name="ffn_down",
    )(g, w, res)


def _lru_body(u_ref, gate_ref, h0_ref, cprev_ref, cw_ref, cb_ref, wa_ref, ba_ref, wx_ref, bx_ref,
              lam_ref, og_ref, y_ref, hlast_ref, cstate_ref,
              ext_ref, uc_ref, a_ref, b_ref, hs_ref, hc_ref, *, tt):
    @pl.when(pl.program_id(1) == 0)
    def _():
        hc_ref[...] = h0_ref[0]
        ext_ref[0:HALO, :] = cprev_ref[0]

    u = u_ref[...]
    ext_ref[HALO:HALO + tt, :] = u
    cw = cw_ref[...]
    uc = cb_ref[...]
    for k in range(LRU_CONV - 1):
        lo = HALO - (LRU_CONV - 1) + k
        uc = uc + ext_ref[lo:lo + tt, :] * cw[k:k + 1, :]
    uc_ref[...] = uc + u * cw[LRU_CONV - 1:LRU_CONV, :]

    last = ext_ref[tt:tt + HALO, :]
    cstate_ref[0] = last
    ext_ref[0:HALO, :] = last

    for p in range(N_PAIRS):
        sl = slice(p * LRU_PAIR, (p + 1) * LRU_PAIR)
        ucp = uc_ref[:, sl]
        xb = ucp.astype(BF16)
        r = _sigmoid(jnp.dot(xb, wa_ref[p], preferred_element_type=F32) + ba_ref[:, sl])
        i = _sigmoid(jnp.dot(xb, wx_ref[p], preferred_element_type=F32) + bx_ref[:, sl])
        log_a = (LRU_C * r) * (-_softplus(-lam_ref[:, sl]))
        a = jnp.exp(log_a)
        a_ref[:, sl] = a
        b_ref[:, sl] = jnp.sqrt(-jnp.tanh(log_a) * (a * a + 1.0)) * (i * ucp)

    def step(t, h):
        h = a_ref[pl.ds(t, 1), :] * h + b_ref[pl.ds(t, 1), :]
        hs_ref[pl.ds(t, 1), :] = h
        return h

    h = lax.fori_loop(0, tt, step, hc_ref[...], unroll=8)
    hc_ref[...] = h
    hlast_ref[0] = h

    y = hs_ref[...] * _gelu(gate_ref[...])
    y_ref[...] = _rms(y, og_ref[...]).astype(BF16)


def _lru(proj, h0, cprev, wts, nseq, seqlen, tt):
    m = nseq * seqlen
    nt = seqlen // tt
    row = lambda b, t: (b * nt + t, 0)
    vec = lambda b, t: (0, 0)
    return pl.pallas_call(
        functools.partial(_lru_body, tt=tt),
        grid=(nseq, nt),
        in_specs=[
            pl.BlockSpec((tt, D_LRU), row),
            pl.BlockSpec((tt, D_LRU), lambda b, t: (b * nt + t, 1)),
            pl.BlockSpec((1, 1, D_LRU), lambda b, t: (b, 0, 0)),
            pl.BlockSpec((1, HALO, D_LRU), lambda b, t: (b, 0, 0)),
            pl.BlockSpec((HALO, D_LRU), vec),
            pl.BlockSpec((1, D_LRU), vec),
            pl.BlockSpec((N_PAIRS, LRU_PAIR, LRU_PAIR), lambda b, t: (0, 0, 0)),
            pl.BlockSpec((1, D_LRU), vec),
            pl.BlockSpec((N_PAIRS, LRU_PAIR, LRU_PAIR), lambda b, t: (0, 0, 0)),
            pl.BlockSpec((1, D_LRU), vec),
            pl.BlockSpec((1, D_LRU), vec),
            pl.BlockSpec((1, D_LRU), vec),
        ],
        out_specs=[
            pl.BlockSpec((tt, D_LRU), row),
            pl.BlockSpec((1, 1, D_LRU), lambda b, t: (b, 0, 0)),
            pl.BlockSpec((1, HALO, D_LRU), lambda b, t: (b, 0, 0)),
        ],
        out_shape=[
            jax.ShapeDtypeStruct((m, D_LRU), BF16),
            jax.ShapeDtypeStruct((nseq, 1, D_LRU), F32),
            jax.ShapeDtypeStruct((nseq, HALO, D_LRU), F32),
        ],
        scratch_shapes=[
            pltpu.VMEM((tt + HALO, D_LRU), F32),
            pltpu.VMEM((tt, D_LRU), F32),
            pltpu.VMEM((tt, D_LRU), F32),
            pltpu.VMEM((tt, D_LRU), F32),
            pltpu.VMEM((tt, D_LRU), F32),
            pltpu.VMEM((1, D_LRU), F32),
        ],
        compiler_params=_params("arbitrary", "arbitrary"),
        name="rg_lru",
    )(proj, proj, h0, cprev, wts["lru_conv_w"], wts["lru_conv_b"], wts["lru_w_a"], wts["lru_b_a"],
      wts["lru_w_x"], wts["lru_b_x"], wts["lru_lambda"], wts["lru_out_g"])


def _qkv_body(q_ref, k_ref, v_ref, gq_ref, gk_ref, qb_ref, kf_ref, kb_ref, vf_ref, vb_ref):
    for h in range(N_HEADS):
        sl = slice(h * HEAD_DIM, (h + 1) * HEAD_DIM)
        qb_ref[:, sl] = (_rms(q_ref[:, sl], gq_ref[...]) * Q_SCALE).astype(BF16)
        kn = _rms(k_ref[:, sl], gk_ref[...])
        kf_ref[:, sl] = kn
        kb_ref[:, sl] = kn.astype(BF16)
    v = v_ref[...]
    vf_ref[...] = v
    vb_ref[...] = v.astype(BF16)


def _qkv(proj, gq, gk, tm):
    m = proj.shape[0]
    col0 = 2 * D_LRU // D_SB
    blk = lambda c: pl.BlockSpec((tm, D_SB), lambda i: (i, c))
    vec = pl.BlockSpec((1, HEAD_DIM), lambda i: (0, 0))
    out = pl.BlockSpec((tm, D_SB), lambda i: (i, 0))
    return pl.pallas_call(
        _qkv_body,
        grid=(m // tm,),
        in_specs=[blk(col0), blk(col0 + 1), blk(col0 + 2), vec, vec],
        out_specs=[out] * 5,
        out_shape=[jax.ShapeDtypeStruct((m, D_SB), d) for d in (BF16, F32, BF16, F32, BF16)],
        compiler_params=_params("parallel"),
        name="qkv_norm",
    )(proj, proj, proj, gq, gk)


def _neg_abs(x):
    return lax.bitcast_convert_type(lax.bitcast_convert_type(x, jnp.int32) | jnp.int32(-2 ** 31), F32)


def _sb_step(q, k, v, tri, c_ref, crows, acc_ref, arows, acols, masked):
    z = lax.dot_general(q, k, (((1,), (1,)), ((), ())), preferred_element_type=F32)
    sp = jnp.maximum(z, 0.0) + jnp.log(1.0 + jnp.exp2(_neg_abs(z))) * LOG2E
    if masked:
        mask = (lax.broadcasted_iota(jnp.int32, z.shape, 1) < lax.broadcasted_iota(jnp.int32, z.shape, 0))
        sp = jnp.where(mask, sp, 0.0)
    inner = jnp.dot(sp.astype(BF16), tri, preferred_element_type=F32)
    c = c_ref[crows, :]
    w = jnp.exp2((z - sp) - (inner + c))
    if masked:
        w = jnp.where(mask, w, 0.0)
    acc_ref[arows, acols] += jnp.dot(w.astype(BF16), v, preferred_element_type=F32)
    c_ref[crows, :] = c + jnp.sum(sp, axis=-1, keepdims=True)


NEG_BIG = -1e30
SKIP_LOG2 = 160.0


def _sb_logits(q, k, z_ref, slot):
    z_ref[slot] = lax.dot_general(q, k, (((1,), (1,)), ((), ())), preferred_element_type=F32)


def _sb_scores(z_ref, c_ref, crows, t_ref, spb_ref, slot, mode):
    if mode == "none":
        t_ref[slot] = jnp.full(t_ref.shape[1:], NEG_BIG, F32)
        spb_ref[slot] = jnp.zeros(spb_ref.shape[1:], BF16)
        return
    z = z_ref[slot]
    sp = jnp.maximum(z, 0.0) + jnp.log(1.0 + jnp.exp2(_neg_abs(z))) * LOG2E
    c = c_ref[crows, :]
    t = (z - sp) - c
    if mode == "diag":
        mask = (lax.broadcasted_iota(jnp.int32, z.shape, 1) < lax.broadcasted_iota(jnp.int32, z.shape, 0))
        sp = jnp.where(mask, sp, 0.0)
        t = jnp.where(mask, t, NEG_BIG)
    t_ref[slot] = t
    spb_ref[slot] = sp.astype(BF16)
    c_ref[crows, :] = c + jnp.sum(sp, axis=-1, keepdims=True)


def _sb_weights(tri, t_ref, spb_ref, wb_ref, slot):
    inner = jnp.dot(spb_ref[slot], tri, preferred_element_type=F32)
    wb_ref[slot] = jnp.exp2(t_ref[slot] - inner).astype(BF16)


def _sb_values(v, wb_ref, slot, acc_ref, arows, acols):
    acc_ref[arows, acols] += jnp.dot(wb_ref[slot], v, preferred_element_type=F32)


def _attn_prompt_body(q_ref, k_ref, v_ref, tri_ref, o_ref, c_ref, z_ref, t_ref, spb_ref, wb_ref, *, nsub, nhead):
    ts = SB_TILE
    base = pl.program_id(1) * nsub
    tri = tri_ref[...]
    c_ref[...] = jnp.zeros_like(c_ref)
    o_ref[...] = jnp.zeros_like(o_ref)
    rows = [pl.ds(s * ts, ts) for s in range(nsub)]
    hcols = [slice(h * HEAD_DIM, (h + 1) * HEAD_DIM) for h in range(nhead)]
    chains = [(h, s) for h in range(nhead) for s in range(nsub)]
    crows = [pl.ds(u * ts, ts) for u in range(len(chains))]
    qs = [q_ref[rows[s], hcols[h]] for h, s in chains]

    @pl.when(base == 0)
    def _():
        for t in reversed(range(nsub)):
            for u, (h, s) in enumerate(chains):
                if s >= t:
                    k = k_ref[t * ts:(t + 1) * ts, hcols[h]]
                    v = v_ref[t * ts:(t + 1) * ts, hcols[h]]
                    _sb_step(qs[u], k, v, tri, c_ref, crows[u], o_ref, rows[s], hcols[h], masked=(s == t))

    @pl.when(base > 0)
    def _():
        n = base + nsub

        def tile(i):
            return pl.ds(pl.multiple_of((n - 1 - i) * ts, ts), ts)

        def modes(i):
            kt = nsub - 1 - i
            return tuple("full" if (kt < 0 or s > kt) else ("diag" if s == kt else "none")
                         for s in range(nsub))

        def iteration(i, p, first=1, last=4, static_i=None):
            if last >= 4 and first <= 4:
                vs = [v_ref[tile(i - 3), hc] for hc in hcols]
                for u, (h, s) in enumerate(chains):
                    _sb_values(vs[h], wb_ref.at[1 - p], u, o_ref, rows[s], hcols[h])
            if last >= 3 and first <= 3:
                for u in range(len(chains)):
                    _sb_weights(tri, t_ref.at[1 - p], spb_ref.at[1 - p], wb_ref.at[p], u)
            if last >= 2 and first <= 2:
                md = modes(static_i - 1) if static_i is not None else ("full",) * nsub
                for u, (h, s) in enumerate(chains):
                    _sb_scores(z_ref.at[1 - p], c_ref, crows[u], t_ref.at[p], spb_ref.at[p], u, md[s])
            if last >= 1 and first <= 1:
                ks = [k_ref[tile(i), hc] for hc in hcols]
                for u, (h, s) in enumerate(chains):
                    _sb_logits(qs[u], ks[h], z_ref.at[p], u)

        nfill = 4
        assert nsub == 2
        for i in range(nfill):
            iteration(i, i & 1, last=min(i + 1, 4), static_i=i)

        def cond(carry):
            j, cmin = carry
            return jnp.logical_and(j < (n - nfill) // 2, cmin < SKIP_LOG2)

        def body(carry):
            j, _ = carry
            i = nfill + 2 * j
            iteration(i, 0)
            iteration(i + 1, 1)
            return j + 1, jnp.min(c_ref[...])

        trips, _ = lax.while_loop(cond, body, (jnp.int32(0), jnp.min(c_ref[...])))
        issued = nfill + 2 * trips
        for d in range(3):
            iteration(issued + d, d & 1, first=d + 2)


def _attn_prompt(qb, kb, vb, tri, nsub):
    t = qb.shape[0]
    tq = nsub * SB_TILE
    nhead = SB_HEADS_PER_STEP
    hw = nhead * HEAD_DIM
    nchain = nhead * nsub
    return pl.pallas_call(
        functools.partial(_attn_prompt_body, nsub=nsub, nhead=nhead),
        grid=(N_HEADS // nhead, t // tq),
        in_specs=[
            pl.BlockSpec((tq, hw), lambda h, i: (i, h)),
            pl.BlockSpec((t, hw), lambda h, i: (0, h)),
            pl.BlockSpec((t, hw), lambda h, i: (0, h)),
            pl.BlockSpec((SB_TILE, SB_TILE), lambda h, i: (0, 0)),
        ],
        out_specs=pl.BlockSpec((tq, hw), lambda h, i: (i, h)),
        out_shape=jax.ShapeDtypeStruct((t, D_SB), F32),
        scratch_shapes=[pltpu.VMEM((nchain * SB_TILE, 1), F32),
                        pltpu.VMEM((2, nchain, SB_TILE, SB_TILE), F32),
                        pltpu.VMEM((2, nchain, SB_TILE, SB_TILE), F32),
                        pltpu.VMEM((2, nchain, SB_TILE, SB_TILE), BF16),
                        pltpu.VMEM((2, nchain, SB_TILE, SB_TILE), BF16)],
        compiler_params=_params("parallel", "arbitrary"),
        name="sb_attn_prompt",
    )(qb, kb, vb, tri)


def _attn_sample_body(q_ref, kn_ref, vn_ref, kc_ref, vc_ref, trin_ref, tri_ref, o_ref, c_ref, *, tq, chunk):
    ts = SB_TILE
    tri = tri_ref[...]
    rows = [pl.ds(h * tq, tq) for h in range(N_HEADS)]
    cols = [slice(h * HEAD_DIM, (h + 1) * HEAD_DIM) for h in range(N_HEADS)]
    qs = [q_ref[:, cl] for cl in cols]

    @pl.when(pl.program_id(1) == 0)
    def _():
        c_ref[...] = jnp.zeros_like(c_ref)
        o_ref[...] = jnp.zeros_like(o_ref)
        trin = trin_ref[...]
        for h in range(N_HEADS):
            _sb_step(qs[h], kn_ref[:, cols[h]], vn_ref[:, cols[h]], trin, c_ref, rows[h], o_ref, slice(None), cols[h], True)

    for t in reversed(range(chunk // ts)):
        @pl.when(jnp.min(c_ref[...]) < SKIP_LOG2)
        def _():
            for h in range(N_HEADS):
                sel = pl.ds(t * ts * N_HEADS + h, ts, stride=N_HEADS)
                k = kc_ref[sel, :].astype(BF16)
                v = vc_ref[sel, :].astype(BF16)
                _sb_step(qs[h], k, v, tri, c_ref, rows[h], o_ref, slice(None), cols[h], False)


def _attn_sample(qb, kb, vb, kc, vc, tri_new, tri, nseq, tq, chunk):
    nchunk = kc.shape[1] // (chunk * N_HEADS)
    new = pl.BlockSpec((tq, D_SB), lambda b, c: (b, 0))
    cache = pl.BlockSpec((None, chunk * N_HEADS, HEAD_DIM), lambda b, c: (b, nchunk - 1 - c, 0))
    return pl.pallas_call(
        functools.partial(_attn_sample_body, tq=tq, chunk=chunk),
        grid=(nseq, nchunk),
        in_specs=[new, new, new, cache, cache,
                  pl.BlockSpec((tq, tq), lambda b, c: (0, 0)),
                  pl.BlockSpec((SB_TILE, SB_TILE), lambda b, c: (0, 0))],
        out_specs=new,
        out_shape=jax.ShapeDtypeStruct((nseq * tq, D_SB), F32),
        scratch_shapes=[pltpu.VMEM((N_HEADS * tq, 1), F32)],
        compiler_params=_params("parallel", "arbitrary"),
        name="sb_attn_sample",
    )(qb, kb, vb, kc, vc, tri_new, tri)


def _ffn_conv_act(pre, val, ext_ref, cw_ref, cb_ref, tt, cols=slice(None)):
    ext_ref[HALO:HALO + tt, cols] = pre
    cw = cw_ref[:, cols]
    pc = cb_ref[:, cols]
    for k in range(FFN_CONV - 1):
        lo = HALO - (FFN_CONV - 1) + k
        pc = pc + ext_ref[lo:lo + tt, cols] * cw[k:k + 1, :]
    pc = pc + pre * cw[FFN_CONV - 1:FFN_CONV, :]
    return (_gelu(pc) * val).astype(BF16), ext_ref[tt:tt + HALO, cols]


def _ffn_act_body(val_ref, pre_ref, sprev_ref, cw_ref, cb_ref, g_ref, sout_ref, ext_ref, *, tt):
    @pl.when(pl.program_id(2) == 0)
    def _():
        ext_ref[0:HALO, :] = sprev_ref[0]

    g, last = _ffn_conv_act(pre_ref[...], val_ref[...], ext_ref, cw_ref, cb_ref, tt)
    g_ref[...] = g
    sout_ref[0] = last
    ext_ref[0:HALO, :] = last


def _ffn_act(up, sprev, cw, cb, nseq, seqlen, tt, tc):
    m = nseq * seqlen
    nt = seqlen // tt
    nc = D_FF // tc
    return pl.pallas_call(
        functools.partial(_ffn_act_body, tt=tt),
        grid=(nseq, nc, nt),
        in_specs=[
            pl.BlockSpec((tt, tc), lambda b, c, t: (b * nt + t, c)),
            pl.BlockSpec((tt, tc), lambda b, c, t: (b * nt + t, nc + c)),
            pl.BlockSpec((1, HALO, tc), lambda b, c, t: (b, 0, c)),
            pl.BlockSpec((HALO, tc), lambda b, c, t: (0, c)),
            pl.BlockSpec((1, tc), lambda b, c, t: (0, c)),
        ],
        out_specs=[
            pl.BlockSpec((tt, tc), lambda b, c, t: (b * nt + t, c)),
            pl.BlockSpec((1, HALO, tc), lambda b, c, t: (b, 0, c)),
        ],
        out_shape=[
            jax.ShapeDtypeStruct((m, D_FF), BF16),
            jax.ShapeDtypeStruct((nseq, HALO, D_FF), F32),
        ],
        scratch_shapes=[pltpu.VMEM((tt + HALO, tc), F32)],
        compiler_params=_params("parallel", "parallel", "arbitrary"),
        name="ffn_act",
    )(up, up, sprev, cw, cb)


def _ffn_up_act_body(x_ref, g_ref, wv_ref, wp_ref, sprev_ref, cw_ref, cb_ref, o_ref, last_ref,
                     hn_ref, halo_ref, ext_ref, *pre, tm):
    i = pl.program_id(0)
    j = pl.program_id(1)
    _norm_row_tile(x_ref, g_ref, hn_ref, tuple(pre) if pre else None)

    @pl.when(i == 0)
    def _():
        ext_ref[0:HALO, :] = sprev_ref[...]

    @pl.when(i > 0)
    def _():
        ext_ref[0:HALO, :] = halo_ref[j]

    hn = hn_ref[...]
    tn = o_ref.shape[1]
    halves = [slice(h * MXU_COLS, (h + 1) * MXU_COLS) for h in range(tn // MXU_COLS)]
    pres = [jnp.dot(hn, wp_ref[:, cs], preferred_element_type=F32) for cs in halves]
    vals = [jnp.dot(hn, wv_ref[:, cs], preferred_element_type=F32) for cs in halves]
    for cs, pre, val in zip(halves, pres, vals):
        g, last = _ffn_conv_act(pre, val, ext_ref, cw_ref, cb_ref, tm, cs)
        o_ref[:, cs] = g
        halo_ref[j, :, cs] = last
        last_ref[0, :, cs] = last


def _ffn_up_act(x, g, w_up, sprev, cw, cb, tm, tn):
    m, k = x.shape
    nj = D_FF // tn
    xspec, xscratch = _rows_operand(tm, k, nj)
    return pl.pallas_call(
        functools.partial(_ffn_up_act_body, tm=tm),
        grid=(m // tm, nj),
        in_specs=[
            xspec,
            pl.BlockSpec((1, k), lambda i, j: (0, 0)),
            pl.BlockSpec((k, tn), lambda i, j: (0, j)),
            pl.BlockSpec((k, tn), lambda i, j: (0, nj + j)),
            pl.BlockSpec((HALO, tn), lambda i, j: (0, j)),
            pl.BlockSpec((HALO, tn), lambda i, j: (0, j)),
            pl.BlockSpec((1, tn), lambda i, j: (0, j)),
        ],
        out_specs=[
            pl.BlockSpec((tm, tn), lambda i, j: (i, j)),
            pl.BlockSpec((1, HALO, tn), lambda i, j: (i, 0, j)),
        ],
        out_shape=[
            jax.ShapeDtypeStruct((m, D_FF), BF16),
            jax.ShapeDtypeStruct((m // tm, HALO, D_FF), F32),
        ],
        scratch_shapes=[
            pltpu.VMEM((tm, k), BF16),
            pltpu.VMEM((nj, HALO, tn), F32),
            pltpu.VMEM((tm + HALO, tn), F32),
        ] + xscratch,
        compiler_params=_params("arbitrary", "arbitrary"),
        name="ffn_up_act",
    )(x, g, w_up, w_up, sprev, cw, cb)


def _front_pad(x, rows):
    return jnp.pad(x, ((0, 0), (rows - x.shape[1], 0), (0, 0)))


def _layer(x3, kpast, vpast, h0, cprev, fprev, wts, tri, tm, tt_lru, tt_ffn, attn_nsub=2, fuse_ffn=None,
           emit=False):
    nseq, seqlen, _ = x3.shape
    m = nseq * seqlen
    x = x3.reshape(m, D_MODEL)
    if fuse_ffn is None:
        fuse_ffn = nseq == 1
    assert not (emit and fuse_ffn)
    tn = 512 if emit else 1024
    tm_wide = ROW_TILE_PREFETCH if m % ROW_TILE_PREFETCH == 0 else tm
    wb = {}

    proj = _norm_matmul(x, wts["norm_mix_g"], wts["w_in"], tm_wide, 512, "in_proj", emit)
    if emit:
        proj, wb["w_in"] = proj
    y_lru, h_last, cstate = _lru(proj, h0[:, None, :], _front_pad(cprev, HALO), wts, nseq, seqlen, tt_lru)
    qb, kf, kb, vf, vb = _qkv(proj, wts["q_norm_g"], wts["k_norm_g"], tm)
    if kpast is None:
        y_sb = _attn_prompt(qb, kb, vb, tri, attn_nsub)
    else:
        past = kpast.shape[1]
        y_sb = _attn_sample(qb, kb, vb, kpast.reshape(nseq, past * N_HEADS, HEAD_DIM),
                            vpast.reshape(nseq, past * N_HEADS, HEAD_DIM),
                            tri[:seqlen, :seqlen], tri, nseq, seqlen, min(past, 1024))
    if emit:
        x1, wb["w_out_lru"], wb["w_out_sb"] = _outproj(y_lru, y_sb, wts["sb_out_g"], wts["w_out"], wts["w_out"],
                                                        x, tm, tn, emit)
    else:
        x1 = _outproj(y_lru, y_sb, wts["sb_out_g"], wts["w_out_lru"], wts["w_out_sb"], x, tm_wide, 512)

    fprev8 = _front_pad(fprev, HALO)
    if fuse_ffn:
        g, lasts = _ffn_up_act(x1, wts["norm_ffn_g"], wts["w_ffn_up"], fprev8[0], wts["ffn_conv_w"],
                               wts["ffn_conv_b"], tm_wide, 512)
        fstate = lasts[-1:]
    else:
        up = _norm_matmul(x1, wts["norm_ffn_g"], wts["w_ffn_up"], tm, tn, "ffn_up", emit)
        if emit:
            up, wb["w_ffn_up"] = up
        g, fstate = _ffn_act(up, fprev8, wts["ffn_conv_w"], wts["ffn_conv_b"], nseq, seqlen, tt_ffn,
                             D_FF if tt_ffn <= 64 else 2048)
    out = _down(g, wts["w_ffn_down"], x1, min(m, 1024), 1024, 2048 if emit else 3072, emit)
    if emit:
        out, wb["w_ffn_down"] = out

    return (out.reshape(nseq, seqlen, D_MODEL),
            kf.reshape(nseq, seqlen, N_HEADS, HEAD_DIM),
            vf.reshape(nseq, seqlen, N_HEADS, HEAD_DIM),
            h_last[:, 0, :],
            cstate[:, HALO - (LRU_CONV - 1):, :],
            fstate[:, HALO - (FFN_CONV - 1):, :],
            wb)


def _pair_blocks(w):
    z = jnp.zeros((N_PAIRS, LRU_BLOCK_DIM, LRU_BLOCK_DIM), w.dtype)
    top = jnp.concatenate([w[0::2], z], axis=2)
    bot = jnp.concatenate([z, w[1::2]], axis=2)
    return jnp.concatenate([top, bot], axis=1).astype(BF16)


def _prep_weights(norm_mix_g, w_in, lru_conv_w, lru_conv_b, lru_w_a, lru_b_a, lru_w_x, lru_b_x,
                  lru_lambda, q_norm_g, k_norm_g, lru_out_g, sb_out_g, w_out, norm_ffn_g,
                  w_ffn_up, ffn_conv_w, ffn_conv_b, w_ffn_down):
    row = lambda v: v.reshape(1, -1)
    pad_rows = lambda w: jnp.pad(w, ((0, HALO - w.shape[0]), (0, 0)))
    return {
        "norm_mix_g": row(norm_mix_g), "w_in": w_in,
        "lru_conv_w": pad_rows(lru_conv_w), "lru_conv_b": row(lru_conv_b),
        "lru_w_a": _pair_blocks(lru_w_a), "lru_b_a": row(lru_b_a),
        "lru_w_x": _pair_blocks(lru_w_x), "lru_b_x": row(lru_b_x),
        "lru_lambda": row(lru_lambda),
        "q_norm_g": row(q_norm_g), "k_norm_g": row(k_norm_g),
        "lru_out_g": row(lru_out_g), "sb_out_g": row(sb_out_g), "w_out": w_out,
        "norm_ffn_g": row(norm_ffn_g), "w_ffn_up": w_ffn_up,
        "ffn_conv_w": pad_rows(ffn_conv_w), "ffn_conv_b": row(ffn_conv_b),
        "w_ffn_down": w_ffn_down,
    }


def kernel(x_prompt, x_sample, cache_sb_k, cache_sb_v, state_lru_h, state_lru_conv, state_ffn_conv, norm_mix_g, w_in, lru_conv_w, lru_conv_b, lru_w_a, lru_b_a, lru_w_x, lru_b_x, lru_lambda, q_norm_g, k_norm_g, lru_out_g, sb_out_g, w_out, norm_ffn_g, w_ffn_up, ffn_conv_w, ffn_conv_b, w_ffn_down):
    depth = w_in.shape[0]
    bp = x_prompt.shape[0]
    tri = jnp.tri(SB_TILE, k=-1, dtype=BF16)
    xp, xs = x_prompt, x_sample
    st_p, st_s = [], []
    for l in range(depth):
        wts = _prep_weights(norm_mix_g[l], w_in[l], lru_conv_w[l], lru_conv_b[l], lru_w_a[l], lru_b_a[l],
                            lru_w_x[l], lru_b_x[l], lru_lambda[l], q_norm_g[l], k_norm_g[l], lru_out_g[l],
                            sb_out_g[l], w_out[l], norm_ffn_g[l], w_ffn_up[l], ffn_conv_w[l],
                            ffn_conv_b[l], w_ffn_down[l])
        xs, *ss, wb = _layer(xs, cache_sb_k[l], cache_sb_v[l], state_lru_h[l], state_lru_conv[l],
                             state_ffn_conv[l], wts, tri, tm=xs.shape[0] * xs.shape[1],
                             tt_lru=xs.shape[1], tt_ffn=xs.shape[1], emit=True)
        st_s.append(ss)
        xp, *sp, _ = _layer(xp, None, None,
                            jnp.zeros((bp, D_LRU), F32),
                            jnp.zeros((bp, LRU_CONV - 1, D_LRU), F32),
                            jnp.zeros((bp, FFN_CONV - 1, D_FF), F32),
                            dict(wts, **wb), tri, tm=512, tt_lru=128, tt_ffn=256)
        st_p.append(sp)
    stack = lambda sts, i: jnp.stack([s[i] for s in sts])
    return (xp, xs) + tuple(stack(st_p, i) for i in range(5)) + tuple(stack(st_s, i) for i in range(5))
```

```python
import functools
import math

import jax
import jax.numpy as jnp
from jax import lax
from jax.experimental import pallas as pl
from jax.experimental.pallas import tpu as pltpu

F32 = jnp.float32
BF16 = jnp.bfloat16

D_MODEL = 4096
HEAD_DIM = 128
N_HEADS = 8
D_SB = N_HEADS * HEAD_DIM
D_LRU = D_MODEL - D_SB
LRU_BLOCKS = 16
LRU_BLOCK_DIM = D_LRU // LRU_BLOCKS
LRU_PAIR = 2 * LRU_BLOCK_DIM
N_PAIRS = LRU_BLOCKS // 2
LRU_CONV = 4
LRU_C = 8.0
D_FF = 3 * D_MODEL
FFN_CONV = 3
EPS = 1e-6
LOG2E = math.log2(math.e)
Q_SCALE = HEAD_DIM ** -0.5 * LOG2E
HALO = 8
SB_TILE = 256
SB_HEADS_PER_STEP = 2
MXU_COLS = 256
ROW_TILE_PREFETCH = 1024

VMEM_LIMIT = 60 * 1024 * 1024


def _params(*sem):
    return pltpu.CompilerParams(dimension_semantics=sem, vmem_limit_bytes=VMEM_LIMIT)


def _rms(x, g):
    ms = jnp.mean(x * x, axis=-1, keepdims=True)
    return (x * lax.rsqrt(ms + EPS)) * g


def _rms_rows(x_ref, g_ref, o_ref, chunk=256):
    g = g_ref[...]
    for r in range(0, x_ref.shape[0], chunk):
        rows = slice(r, min(r + chunk, x_ref.shape[0]))
        o_ref[rows, :] = _rms(x_ref[rows, :], g).astype(BF16)


def _gelu(x):
    return x * (0.5 * (1.0 + jnp.tanh(0.7978845608028654 * (x + 0.044715 * (x * x * x)))))


def _sigmoid(x):
    return 1.0 / (1.0 + jnp.exp(-x))


def _softplus(x):
    return jnp.maximum(x, 0.0) + jnp.log1p(jnp.exp(-jnp.abs(x)))


def _weight_block(w_ref, wb_ref):
    w = w_ref[...]
    if wb_ref is not None:
        w = w.astype(BF16)
        wb_ref[...] = w
    return w


def _rows_copy(x_hbm, xbuf, sem, i):
    tm = xbuf.shape[0]
    return pltpu.make_async_copy(x_hbm.at[pl.ds(pl.multiple_of(i * tm, tm), tm), :], xbuf, sem)


def _norm_row_tile(x_ref, g_ref, hn_ref, prefetch):
    i, j = pl.program_id(0), pl.program_id(1)
    if prefetch is None:
        @pl.when(j == 0)
        def _():
            _rms_rows(x_ref, g_ref, hn_ref)
        return
    xbuf, sem = prefetch

    @pl.when(j == 0)
    def _():
        @pl.when(i == 0)
        def _():
            _rows_copy(x_ref, xbuf, sem, 0).start()

        _rows_copy(x_ref, xbuf, sem, i).wait()
        _rms_rows(xbuf, g_ref, hn_ref)

    @pl.when(jnp.logical_and(j == 1, i + 1 < pl.num_programs(0)))
    def _():
        _rows_copy(x_ref, xbuf, sem, i + 1).start()


def _rows_operand(tm, k, ncols):
    if tm >= ROW_TILE_PREFETCH and ncols >= 2:
        return pl.BlockSpec(memory_space=pl.ANY), [pltpu.VMEM((tm, k), F32), pltpu.SemaphoreType.DMA(())]
    return pl.BlockSpec((tm, k), lambda i, j: (i, 0)), []


def _norm_matmul_body(x_ref, g_ref, w_ref, o_ref, *rest, emit, prefetch):
    wb_ref, hn_ref, *pre = rest if emit else (None,) + rest
    _norm_row_tile(x_ref, g_ref, hn_ref, tuple(pre) if prefetch else None)
    o_ref[...] = jnp.dot(hn_ref[...], _weight_block(w_ref, wb_ref), preferred_element_type=F32)


def _norm_matmul(x, g, w, tm, tn, name, emit=False):
    m, k = x.shape
    n = w.shape[1]
    assert not emit or m == tm
    xspec, xscratch = _rows_operand(tm, k, n // tn)
    wspec = pl.BlockSpec((k, tn), lambda i, j: (0, j))
    ospec = pl.BlockSpec((tm, tn), lambda i, j: (i, j))
    oshape = jax.ShapeDtypeStruct((m, n), F32)
    return pl.pallas_call(
        functools.partial(_norm_matmul_body, emit=emit, prefetch=bool(xscratch)),
        grid=(m // tm, n // tn),
        in_specs=[
            xspec,
            pl.BlockSpec((1, k), lambda i, j: (0, 0)),
            wspec,
        ],
        out_specs=[ospec, wspec] if emit else ospec,
        out_shape=[oshape, jax.ShapeDtypeStruct((k, n), BF16)] if emit else oshape,
        scratch_shapes=[pltpu.VMEM((tm, k), BF16)] + xscratch,
        compiler_params=_params("arbitrary", "arbitrary"),
        name=name,
    )(x, g, w)


def _outproj_body(a_ref, y_ref, gsb_ref, w1_ref, w2_ref, res_ref, o_ref, *rest, emit):
    w1b_ref, w2b_ref, yn_ref = rest if emit else (None, None) + rest

    @pl.when(pl.program_id(1) == 0)
    def _():
        _rms_rows(y_ref, gsb_ref, yn_ref)

    acc = jnp.dot(a_ref[...], _weight_block(w1_ref, w1b_ref), preferred_element_type=F32)
    acc = acc + jnp.dot(yn_ref[...], _weight_block(w2_ref, w2b_ref), preferred_element_type=F32)
    o_ref[...] = res_ref[...] + acc


def _outproj(a, y_sb, g_sb, w1, w2, res, tm, tn, emit=False):
    m = a.shape[0]
    n = w1.shape[1]
    assert not emit or m == tm
    w2_row_block = (w2.shape[0] - D_SB) // D_SB
    ospec = pl.BlockSpec((tm, tn), lambda i, j: (i, j))
    oshape = jax.ShapeDtypeStruct((m, n), F32)
    wbspecs = [pl.BlockSpec((D_LRU, tn), lambda i, j: (0, j)), pl.BlockSpec((D_SB, tn), lambda i, j: (0, j))]
    wbshapes = [jax.ShapeDtypeStruct((D_LRU, n), BF16), jax.ShapeDtypeStruct((D_SB, n), BF16)]
    return pl.pallas_call(
        functools.partial(_outproj_body, emit=emit),
        grid=(m // tm, n // tn),
        in_specs=[
            pl.BlockSpec((tm, D_LRU), lambda i, j: (i, 0)),
            pl.BlockSpec((tm, D_SB), lambda i, j: (i, 0)),
            pl.BlockSpec((1, D_SB), lambda i, j: (0, 0)),
            pl.BlockSpec((D_LRU, tn), lambda i, j: (0, j)),
            pl.BlockSpec((D_SB, tn), lambda i, j: (w2_row_block, j)),
            ospec,
        ],
        out_specs=[ospec] + wbspecs if emit else ospec,
        out_shape=[oshape] + wbshapes if emit else oshape,
        scratch_shapes=[pltpu.VMEM((tm, D_SB), BF16)],
        compiler_params=_params("parallel", "arbitrary"),
        name="out_proj",
    )(a, y_sb, g_sb, w1, w2, res)


def _down_body(g_ref, w_ref, res_ref, o_ref, *rest, emit):
    wb_ref = rest[0] if emit else None
    k = pl.program_id(2)

    @pl.when(k == 0)
    def _():
        o_ref[...] = res_ref[...]

    o_ref[...] += jnp.dot(g_ref[...], _weight_block(w_ref, wb_ref), preferred_element_type=F32)


def _down(g, w, res, tm, tn, tk, emit=False):
    m, kdim = g.shape
    n = w.shape[1]
    assert not emit or m == tm
    wspec = pl.BlockSpec((tk, tn), lambda i, j, k: (k, j))
    ospec = pl.BlockSpec((tm, tn), lambda i, j, k: (i, j))
    oshape = jax.ShapeDtypeStruct((m, n), F32)
    return pl.pallas_call(
        functools.partial(_down_body, emit=emit),
        grid=(m // tm, n // tn, kdim // tk),
        in_specs=[pl.BlockSpec((tm, tk), lambda i, j, k: (i, k)), wspec, ospec],
        out_specs=[ospec, wspec] if emit else ospec,
        out_shape=[oshape, jax.ShapeDtypeStruct((kdim, n), BF16)] if emit else oshape,
        compiler_params=_params("parallel", "parallel", "arbitrary"),
        name="ffn_down",
    )(g, w, res)


def _lru_body(u_ref, gate_ref, h0_ref, cprev_ref, cw_ref, cb_ref, wa_ref, ba_ref, wx_ref, bx_ref,
              lam_ref, og_ref, q_ref, k_ref, v_ref, gq_ref, gk_ref,
              y_ref, hlast_ref, cstate_ref, qb_ref, kf_ref, kb_ref, vf_ref, vb_ref,
              ext_ref, uc_ref, a_ref, b_ref, hs_ref, hc_ref, *, tt):
    _qkv_body(q_ref, k_ref, v_ref, gq_ref, gk_ref, qb_ref, kf_ref, kb_ref, vf_ref, vb_ref)

    @pl.when(pl.program_id(1) == 0)
    def _():
        hc_ref[...] = h0_ref[0]
        ext_ref[0:HALO, :] = cprev_ref[0]

    u = u_ref[...]
    ext_ref[HALO:HALO + tt, :] = u
    cw = cw_ref[...]
    uc = cb_ref[...]
    for k in range(LRU_CONV - 1):
        lo = HALO - (LRU_CONV - 1) + k
        uc = uc + ext_ref[lo:lo + tt, :] * cw[k:k + 1, :]
    uc_ref[...] = uc + u * cw[LRU_CONV - 1:LRU_CONV, :]

    last = ext_ref[tt:tt + HALO, :]
    cstate_ref[0] = last
    ext_ref[0:HALO, :] = last

    for p in range(N_PAIRS):
        sl = slice(p * LRU_PAIR, (p + 1) * LRU_PAIR)
        ucp = uc_ref[:, sl]
        xb = ucp.astype(BF16)
        r = _sigmoid(jnp.dot(xb, wa_ref[p], preferred_element_type=F32) + ba_ref[:, sl])
        i = _sigmoid(jnp.dot(xb, wx_ref[p], preferred_element_type=F32) + bx_ref[:, sl])
        log_a = (LRU_C * r) * (-_softplus(-lam_ref[:, sl]))
        a = jnp.exp(log_a)
        a_ref[:, sl] = a
        b_ref[:, sl] = jnp.sqrt(-jnp.tanh(log_a) * (a * a + 1.0)) * (i * ucp)

    def step(t, h):
        h = a_ref[pl.ds(t, 1), :] * h + b_ref[pl.ds(t, 1), :]
        hs_ref[pl.ds(t, 1), :] = h
        return h

    h = lax.fori_loop(0, tt, step, hc_ref[...], unroll=8)
    hc_ref[...] = h
    hlast_ref[0] = h

    y = hs_ref[...] * _gelu(gate_ref[...])
    y_ref[...] = _rms(y, og_ref[...]).astype(BF16)


def _lru(proj, h0, cprev, wts, nseq, seqlen, tt):
    m = nseq * seqlen
    nt = seqlen // tt
    row = lambda b, t: (b * nt + t, 0)
    vec = lambda b, t: (0, 0)
    return pl.pallas_call(
        functools.partial(_lru_body, tt=tt),
        grid=(nseq, nt),
        in_specs=[
            pl.BlockSpec((tt, D_LRU), row),
            pl.BlockSpec((tt, D_LRU), lambda b, t: (b * nt + t, 1)),
            pl.BlockSpec((1, 1, D_LRU), lambda b, t: (b, 0, 0)),
            pl.BlockSpec((1, HALO, D_LRU), lambda b, t: (b, 0, 0)),
            pl.BlockSpec((HALO, D_LRU), vec),
            pl.BlockSpec((1, D_LRU), vec),
            pl.BlockSpec((N_PAIRS, LRU_PAIR, LRU_PAIR), lambda b, t: (0, 0, 0)),
            pl.BlockSpec((1, D_LRU), vec),
            pl.BlockSpec((N_PAIRS, LRU_PAIR, LRU_PAIR), lambda b, t: (0, 0, 0)),
            pl.BlockSpec((1, D_LRU), vec),
            pl.BlockSpec((1, D_LRU), vec),
            pl.BlockSpec((1, D_LRU), vec),
        ] + [pl.BlockSpec((tt, D_SB), lambda b, t, c=c: (b * nt + t, 2 * D_LRU // D_SB + c)) for c in range(3)] + [
            pl.BlockSpec((1, HEAD_DIM), vec),
            pl.BlockSpec((1, HEAD_DIM), vec),
        ],
        out_specs=[
            pl.BlockSpec((tt, D_LRU), row),
            pl.BlockSpec((1, 1, D_LRU), lambda b, t: (b, 0, 0)),
            pl.BlockSpec((1, HALO, D_LRU), lambda b, t: (b, 0, 0)),
        ] + [pl.BlockSpec((tt, D_SB), row)] * 5,
        out_shape=[
            jax.ShapeDtypeStruct((m, D_LRU), BF16),
            jax.ShapeDtypeStruct((nseq, 1, D_LRU), F32),
            jax.ShapeDtypeStruct((nseq, HALO, D_LRU), F32),
        ] + [jax.ShapeDtypeStruct((m, D_SB), d) for d in (BF16, F32, BF16, F32, BF16)],
        scratch_shapes=[
            pltpu.VMEM((tt + HALO, D_LRU), F32),
            pltpu.VMEM((tt, D_LRU), F32),
            pltpu.VMEM((tt, D_LRU), F32),
            pltpu.VMEM((tt, D_LRU), F32),
            pltpu.VMEM((tt, D_LRU), F32),
            pltpu.VMEM((1, D_LRU), F32),
        ],
        compiler_params=_params("arbitrary", "arbitrary"),
        name="rg_lru",
    )(proj, proj, h0, cprev, wts["lru_conv_w"], wts["lru_conv_b"], wts["lru_w_a"], wts["lru_b_a"],
      wts["lru_w_x"], wts["lru_b_x"], wts["lru_lambda"], wts["lru_out_g"],
      proj, proj, proj, wts["q_norm_g"], wts["k_norm_g"])


def _qkv_body(q_ref, k_ref, v_ref, gq_ref, gk_ref, qb_ref, kf_ref, kb_ref, vf_ref, vb_ref):
    for h in range(N_HEADS):
        sl = slice(h * HEAD_DIM, (h + 1) * HEAD_DIM)
        qb_ref[:, sl] = (_rms(q_ref[:, sl], gq_ref[...]) * Q_SCALE).astype(BF16)
        kn = _rms(k_ref[:, sl], gk_ref[...])
        kf_ref[:, sl] = kn
        kb_ref[:, sl] = kn.astype(BF16)
    v = v_ref[...]
    vf_ref[...] = v
    vb_ref[...] = v.astype(BF16)


def _qkv(proj, gq, gk, tm):
    m = proj.shape[0]
    col0 = 2 * D_LRU // D_SB
    blk = lambda c: pl.BlockSpec((tm, D_SB), lambda i: (i, c))
    vec = pl.BlockSpec((1, HEAD_DIM), lambda i: (0, 0))
    out = pl.BlockSpec((tm, D_SB), lambda i: (i, 0))
    return pl.pallas_call(
        _qkv_body,
        grid=(m // tm,),
        in_specs=[blk(col0), blk(col0 + 1), blk(col0 + 2), vec, vec],
        out_specs=[out] * 5,
        out_shape=[jax.ShapeDtypeStruct((m, D_SB), d) for d in (BF16, F32, BF16, F32, BF16)],
        compiler_params=_params("parallel"),
        name="qkv_norm",
    )(proj, proj, proj, gq, gk)


def _neg_abs(x):
    return lax.bitcast_convert_type(lax.bitcast_convert_type(x, jnp.int32) | jnp.int32(-2 ** 31), F32)


def _sb_step(q, k, v, tri, c_ref, crows, acc_ref, arows, acols, masked):
    z = lax.dot_general(q, k, (((1,), (1,)), ((), ())), preferred_element_type=F32)
    sp = jnp.maximum(z, 0.0) + jnp.log(1.0 + jnp.exp2(_neg_abs(z))) * LOG2E
    if masked:
        mask = (lax.broadcasted_iota(jnp.int32, z.shape, 1) < lax.broadcasted_iota(jnp.int32, z.shape, 0))
        sp = jnp.where(mask, sp, 0.0)
    inner = jnp.dot(sp.astype(BF16), tri, preferred_element_type=F32)
    c = c_ref[crows, :]
    w = jnp.exp2((z - sp) - (inner + c))
    if masked:
        w = jnp.where(mask, w, 0.0)
    acc_ref[arows, acols] += jnp.dot(w.astype(BF16), v, preferred_element_type=F32)
    c_ref[crows, :] = c + jnp.sum(sp, axis=-1, keepdims=True)


NEG_BIG = -1e30
SKIP_LOG2 = 160.0


def _sb_logits(q, k, z_ref, slot):
    z_ref[slot] = lax.dot_general(q, k, (((1,), (1,)), ((), ())), preferred_element_type=F32)


def _sb_scores(z_ref, c_ref, crows, t_ref, spb_ref, slot, mode):
    if mode == "none":
        t_ref[slot] = jnp.full(t_ref.shape[1:], NEG_BIG, F32)
        spb_ref[slot] = jnp.zeros(spb_ref.shape[1:], BF16)
        return
    z = z_ref[slot]
    sp = jnp.maximum(z, 0.0) + jnp.log(1.0 + jnp.exp2(_neg_abs(z))) * LOG2E
    c = c_ref[crows, :]
    t = (z - sp) - c
    if mode == "diag":
        mask = (lax.broadcasted_iota(jnp.int32, z.shape, 1) < lax.broadcasted_iota(jnp.int32, z.shape, 0))
        sp = jnp.where(mask, sp, 0.0)
        t = jnp.where(mask, t, NEG_BIG)
    t_ref[slot] = t
    spb_ref[slot] = sp.astype(BF16)
    c_ref[crows, :] = c + jnp.sum(sp, axis=-1, keepdims=True)


def _sb_weights(tri, t_ref, spb_ref, wb_ref, slot):
    inner = jnp.dot(spb_ref[slot], tri, preferred_element_type=F32)
    wb_ref[slot] = jnp.exp2(t_ref[slot] - inner).astype(BF16)


def _sb_values(v, wb_ref, slot, acc_ref, arows, acols):
    acc_ref[arows, acols] += jnp.dot(wb_ref[slot], v, preferred_element_type=F32)


def _attn_prompt_body(q_ref, k_ref, v_ref, tri_ref, o_ref, c_ref, z_ref, t_ref, spb_ref, wb_ref, *, nsub, nhead):
    ts = SB_TILE
    base = pl.program_id(1) * nsub
    tri = tri_ref[...]
    c_ref[...] = jnp.zeros_like(c_ref)
    o_ref[...] = jnp.zeros_like(o_ref)
    rows = [pl.ds(s * ts, ts) for s in range(nsub)]
    hcols = [slice(h * HEAD_DIM, (h + 1) * HEAD_DIM) for h in range(nhead)]
    chains = [(h, s) for h in range(nhead) for s in range(nsub)]
    crows = [pl.ds(u * ts, ts) for u in range(len(chains))]
    qs = [q_ref[rows[s], hcols[h]] for h, s in chains]

    @pl.when(base == 0)
    def _():
        for t in reversed(range(nsub)):
            for u, (h, s) in enumerate(chains):
                if s >= t:
                    k = k_ref[t * ts:(t + 1) * ts, hcols[h]]
                    v = v_ref[t * ts:(t + 1) * ts, hcols[h]]
                    _sb_step(qs[u], k, v, tri, c_ref, crows[u], o_ref, rows[s], hcols[h], masked=(s == t))

    @pl.when(base > 0)
    def _():
        n = base + nsub

        def tile(i):
            return pl.ds(pl.multiple_of((n - 1 - i) * ts, ts), ts)

        def modes(i):
            kt = nsub - 1 - i
            return tuple("full" if (kt < 0 or s > kt) else ("diag" if s == kt else "none")
                         for s in range(nsub))

        def iteration(i, p, first=1, last=4, static_i=None):
            if last >= 4 and first <= 4:
                vs = [v_ref[tile(i - 3), hc] for hc in hcols]
                for u, (h, s) in enumerate(chains):
                    _sb_values(vs[h], wb_ref.at[1 - p], u, o_ref, rows[s], hcols[h])
            if last >= 3 and first <= 3:
                for u in range(len(chains)):
                    _sb_weights(tri, t_ref.at[1 - p], spb_ref.at[1 - p], wb_ref.at[p], u)
            if last >= 2 and first <= 2:
                md = modes(static_i - 1) if static_i is not None else ("full",) * nsub
                for u, (h, s) in enumerate(chains):
                    _sb_scores(z_ref.at[1 - p], c_ref, crows[u], t_ref.at[p], spb_ref.at[p], u, md[s])
            if last >= 1 and first <= 1:
                ks = [k_ref[tile(i), hc] for hc in hcols]
                for u, (h, s) in enumerate(chains):
                    _sb_logits(qs[u], ks[h], z_ref.at[p], u)

        nfill = 4
        assert nsub == 2
        for i in range(nfill):
            iteration(i, i & 1, last=min(i + 1, 4), static_i=i)

        def cond(carry):
            j, cmin = carry
            return jnp.logical_and(j < (n - nfill) // 2, cmin < SKIP_LOG2)

        def body(carry):
            j, _ = carry
            i = nfill + 2 * j
            iteration(i, 0)
            iteration(i + 1, 1)
            return j + 1, jnp.min(c_ref[...])

        trips, _ = lax.while_loop(cond, body, (jnp.int32(0), jnp.min(c_ref[...])))
        issued = nfill + 2 * trips
        for d in range(3):
            iteration(issued + d, d & 1, first=d + 2)


def _attn_prompt(qb, kb, vb, tri, nsub):
    t = qb.shape[0]
    tq = nsub * SB_TILE
    nhead = SB_HEADS_PER_STEP
    hw = nhead * HEAD_DIM
    nchain = nhead * nsub
    return pl.pallas_call(
        functools.partial(_attn_prompt_body, nsub=nsub, nhead=nhead),
        grid=(N_HEADS // nhead, t // tq),
        in_specs=[
            pl.BlockSpec((tq, hw), lambda h, i: (i, h)),
            pl.BlockSpec((t, hw), lambda h, i: (0, h)),
            pl.BlockSpec((t, hw), lambda h, i: (0, h)),
            pl.BlockSpec((SB_TILE, SB_TILE), lambda h, i: (0, 0)),
        ],
        out_specs=pl.BlockSpec((tq, hw), lambda h, i: (i, h)),
        out_shape=jax.ShapeDtypeStruct((t, D_SB), F32),
        scratch_shapes=[pltpu.VMEM((nchain * SB_TILE, 1), F32),
                        pltpu.VMEM((2, nchain, SB_TILE, SB_TILE), F32),
                        pltpu.VMEM((2, nchain, SB_TILE, SB_TILE), F32),
                        pltpu.VMEM((2, nchain, SB_TILE, SB_TILE), BF16),
                        pltpu.VMEM((2, nchain, SB_TILE, SB_TILE), BF16)],
        compiler_params=_params("parallel", "arbitrary"),
        name="sb_attn_prompt",
    )(qb, kb, vb, tri)


def _attn_sample_body(q_ref, kn_ref, vn_ref, kc_ref, vc_ref, trin_ref, tri_ref, o_ref, c_ref, *, tq, chunk):
    ts = SB_TILE
    tri = tri_ref[...]
    rows = [pl.ds(h * tq, tq) for h in range(N_HEADS)]
    cols = [slice(h * HEAD_DIM, (h + 1) * HEAD_DIM) for h in range(N_HEADS)]
    qs = [q_ref[:, cl] for cl in cols]

    @pl.when(pl.program_id(1) == 0)
    def _():
        c_ref[...] = jnp.zeros_like(c_ref)
        o_ref[...] = jnp.zeros_like(o_ref)
        trin = trin_ref[...]
        for h in range(N_HEADS):
            _sb_step(qs[h], kn_ref[:, cols[h]], vn_ref[:, cols[h]], trin, c_ref, rows[h], o_ref, slice(None), cols[h], True)

    for t in reversed(range(chunk // ts)):
        @pl.when(jnp.min(c_ref[...]) < SKIP_LOG2)
        def _():
            for h in range(N_HEADS):
                sel = pl.ds(t * ts * N_HEADS + h, ts, stride=N_HEADS)
                k = kc_ref[sel, :].astype(BF16)
                v = vc_ref[sel, :].astype(BF16)
                _sb_step(qs[h], k, v, tri, c_ref, rows[h], o_ref, slice(None), cols[h], False)


def _attn_sample(qb, kb, vb, kc, vc, tri_new, tri, nseq, tq, chunk):
    nchunk = kc.shape[1] // (chunk * N_HEADS)
    new = pl.BlockSpec((tq, D_SB), lambda b, c: (b, 0))
    cache = pl.BlockSpec((None, chunk * N_HEADS, HEAD_DIM), lambda b, c: (b, nchunk - 1 - c, 0))
    return pl.pallas_call(
        functools.partial(_attn_sample_body, tq=tq, chunk=chunk),
        grid=(nseq, nchunk),
        in_specs=[new, new, new, cache, cache,
                  pl.BlockSpec((tq, tq), lambda b, c: (0, 0)),
                  pl.BlockSpec((SB_TILE, SB_TILE), lambda b, c: (0, 0))],
        out_specs=new,
        out_shape=jax.ShapeDtypeStruct((nseq * tq, D_SB), F32),
        scratch_shapes=[pltpu.VMEM((N_HEADS * tq, 1), F32)],
        compiler_params=_params("parallel", "arbitrary"),
        name="sb_attn_sample",
    )(qb, kb, vb, kc, vc, tri_new, tri)


def _ffn_conv_act(pre, val, ext_ref, cw_ref, cb_ref, tt, cols=slice(None)):
    ext_ref[HALO:HALO + tt, cols] = pre
    cw = cw_ref[:, cols]
    pc = cb_ref[:, cols]
    for k in range(FFN_CONV - 1):
        lo = HALO - (FFN_CONV - 1) + k
        pc = pc + ext_ref[lo:lo + tt, cols] * cw[k:k + 1, :]
    pc = pc + pre * cw[FFN_CONV - 1:FFN_CONV, :]
    return (_gelu(pc) * val).astype(BF16), ext_ref[tt:tt + HALO, cols]


def _ffn_act_body(val_ref, pre_ref, sprev_ref, cw_ref, cb_ref, g_ref, sout_ref, ext_ref, *, tt):
    @pl.when(pl.program_id(2) == 0)
    def _():
        ext_ref[0:HALO, :] = sprev_ref[0]

    g, last = _ffn_conv_act(pre_ref[...], val_ref[...], ext_ref, cw_ref, cb_ref, tt)
    g_ref[...] = g
    sout_ref[0] = last
    ext_ref[0:HALO, :] = last


def _ffn_act(up, sprev, cw, cb, nseq, seqlen, tt, tc):
    m = nseq * seqlen
    nt = seqlen // tt
    nc = D_FF // tc
    return pl.pallas_call(
        functools.partial(_ffn_act_body, tt=tt),
        grid=(nseq, nc, nt),
        in_specs=[
            pl.BlockSpec((tt, tc), lambda b, c, t: (b * nt + t, c)),
            pl.BlockSpec((tt, tc), lambda b, c, t: (b * nt + t, nc + c)),
            pl.BlockSpec((1, HALO, tc), lambda b, c, t: (b, 0, c)),
            pl.BlockSpec((HALO, tc), lambda b, c, t: (0, c)),
            pl.BlockSpec((1, tc), lambda b, c, t: (0, c)),
        ],
        out_specs=[
            pl.BlockSpec((tt, tc), lambda b, c, t: (b * nt + t, c)),
            pl.BlockSpec((1, HALO, tc), lambda b, c, t: (b, 0, c)),
        ],
        out_shape=[
            jax.ShapeDtypeStruct((m, D_FF), BF16),
            jax.ShapeDtypeStruct((nseq, HALO, D_FF), F32),
        ],
        scratch_shapes=[pltpu.VMEM((tt + HALO, tc), F32)],
        compiler_params=_params("parallel", "parallel", "arbitrary"),
        name="ffn_act",
    )(up, up, sprev, cw, cb)


def _ffn_up_act_body(x_ref, g_ref, wv_ref, wp_ref, sprev_ref, cw_ref, cb_ref, o_ref, last_ref,
                     hn_ref, halo_ref, ext_ref, *pre, tm):
    i = pl.program_id(0)
    j = pl.program_id(1)
    _norm_row_tile(x_ref, g_ref, hn_ref, tuple(pre) if pre else None)

    @pl.when(i == 0)
    def _():
        ext_ref[0:HALO, :] = sprev_ref[...]

    @pl.when(i > 0)
    def _():
        ext_ref[0:HALO, :] = halo_ref[j]

    hn = hn_ref[...]
    tn = o_ref.shape[1]
    halves = [slice(h * MXU_COLS, (h + 1) * MXU_COLS) for h in range(tn // MXU_COLS)]
    pres = [jnp.dot(hn, wp_ref[:, cs], preferred_element_type=F32) for cs in halves]
    vals = [jnp.dot(hn, wv_ref[:, cs], preferred_element_type=F32) for cs in halves]
    for cs, pre, val in zip(halves, pres, vals):
        g, last = _ffn_conv_act(pre, val, ext_ref, cw_ref, cb_ref, tm, cs)
        o_ref[:, cs] = g
        halo_ref[j, :, cs] = last
        last_ref[0, :, cs] = last


def _ffn_up_act(x, g, w_up, sprev, cw, cb, tm, tn):
    m, k = x.shape
    nj = D_FF // tn
    xspec, xscratch = _rows_operand(tm, k, nj)
    return pl.pallas_call(
        functools.partial(_ffn_up_act_body, tm=tm),
        grid=(m // tm, nj),
        in_specs=[
            xspec,
            pl.BlockSpec((1, k), lambda i, j: (0, 0)),
            pl.BlockSpec((k, tn), lambda i, j: (0, j)),
            pl.BlockSpec((k, tn), lambda i, j: (0, nj + j)),
            pl.BlockSpec((HALO, tn), lambda i, j: (0, j)),
            pl.BlockSpec((HALO, tn), lambda i, j: (0, j)),
            pl.BlockSpec((1, tn), lambda i, j: (0, j)),
        ],
        out_specs=[
            pl.BlockSpec((tm, tn), lambda i, j: (i, j)),
            pl.BlockSpec((1, HALO, tn), lambda i, j: (i, 0, j)),
        ],
        out_shape=[
            jax.ShapeDtypeStruct((m, D_FF), BF16),
            jax.ShapeDtypeStruct((m // tm, HALO, D_FF), F32),
        ],
        scratch_shapes=[
            pltpu.VMEM((tm, k), BF16),
            pltpu.VMEM((nj, HALO, tn), F32),
            pltpu.VMEM((tm + HALO, tn), F32),
        ] + xscratch,
        compiler_params=_params("arbitrary", "arbitrary"),
        name="ffn_up_act",
    )(x, g, w_up, w_up, sprev, cw, cb)


def _front_pad(x, rows):
    return jnp.pad(x, ((0, 0), (rows - x.shape[1], 0), (0, 0)))


def _layer(x3, kpast, vpast, h0, cprev, fprev, wts, tri, tm, tt_lru, tt_ffn, attn_nsub=2, fuse_ffn=None,
           emit=False):
    nseq, seqlen, _ = x3.shape
    m = nseq * seqlen
    x = x3.reshape(m, D_MODEL)
    if fuse_ffn is None:
        fuse_ffn = nseq == 1
    assert not (emit and fuse_ffn)
    tn = 512 if emit else 1024
    tm_wide = ROW_TILE_PREFETCH if m % ROW_TILE_PREFETCH == 0 else tm
    wb = {}

    proj = _norm_matmul(x, wts["norm_mix_g"], wts["w_in"], tm_wide, 512, "in_proj", emit)
    if emit:
        proj, wb["w_in"] = proj
    y_lru, h_last, cstate, qb, kf, kb, vf, vb = _lru(proj, h0[:, None, :], _front_pad(cprev, HALO), wts,
                                                      nseq, seqlen, tt_lru)
    if kpast is None:
        y_sb = _attn_prompt(qb, kb, vb, tri, attn_nsub)
    else:
        past = kpast.shape[1]
        y_sb = _attn_sample(qb, kb, vb, kpast.reshape(nseq, past * N_HEADS, HEAD_DIM),
                            vpast.reshape(nseq, past * N_HEADS, HEAD_DIM),
                            tri[:seqlen, :seqlen], tri, nseq, seqlen, min(past, 1024))
    if emit:
        x1, wb["w_out_lru"], wb["w_out_sb"] = _outproj(y_lru, y_sb, wts["sb_out_g"], wts["w_out"], wts["w_out"],
                                                        x, tm, tn, emit)
    else:
        x1 = _outproj(y_lru, y_sb, wts["sb_out_g"], wts["w_out_lru"], wts["w_out_sb"], x, tm_wide, 512)

    fprev8 = _front_pad(fprev, HALO)
    if fuse_ffn:
        g, lasts = _ffn_up_act(x1, wts["norm_ffn_g"], wts["w_ffn_up"], fprev8[0], wts["ffn_conv_w"],
                               wts["ffn_conv_b"], tm_wide, 512)
        fstate = lasts[-1:]
    else:
        up = _norm_matmul(x1, wts["norm_ffn_g"], wts["w_ffn_up"], tm, tn, "ffn_up", emit)
        if emit:
            up, wb["w_ffn_up"] = up
        g, fstate = _ffn_act(up, fprev8, wts["ffn_conv_w"], wts["ffn_conv_b"], nseq, seqlen, tt_ffn,
                             D_FF if tt_ffn <= 64 else 2048)
    out = _down(g, wts["w_ffn_down"], x1, min(m, 1024), 1024, 2048 if emit else 3072, emit)
    if emit:
        out, wb["w_ffn_down"] = out

    return (out.reshape(nseq, seqlen, D_MODEL),
            kf.reshape(nseq, seqlen, N_HEADS, HEAD_DIM),
            vf.reshape(nseq, seqlen, N_HEADS, HEAD_DIM),
            h_last[:, 0, :],
            cstate[:, HALO - (LRU_CONV - 1):, :],
            fstate[:, HALO - (FFN_CONV - 1):, :],
            wb)


def _pair_blocks(w):
    z = jnp.zeros((N_PAIRS, LRU_BLOCK_DIM, LRU_BLOCK_DIM), w.dtype)
    top = jnp.concatenate([w[0::2], z], axis=2)
    bot = jnp.concatenate([z, w[1::2]], axis=2)
    return jnp.concatenate([top, bot], axis=1).astype(BF16)


def _prep_weights(norm_mix_g, w_in, lru_conv_w, lru_conv_b, lru_w_a, lru_b_a, lru_w_x, lru_b_x,
                  lru_lambda, q_norm_g, k_norm_g, lru_out_g, sb_out_g, w_out, norm_ffn_g,
                  w_ffn_up, ffn_conv_w, ffn_conv_b, w_ffn_down):
    row = lambda v: v.reshape(1, -1)
    pad_rows = lambda w: jnp.pad(w, ((0, HALO - w.shape[0]), (0, 0)))
    return {
        "norm_mix_g": row(norm_mix_g), "w_in": w_in,
        "lru_conv_w": pad_rows(lru_conv_w), "lru_conv_b": row(lru_conv_b),
        "lru_w_a": _pair_blocks(lru_w_a), "lru_b_a": row(lru_b_a),
        "lru_w_x": _pair_blocks(lru_w_x), "lru_b_x": row(lru_b_x),
        "lru_lambda": row(lru_lambda),
        "q_norm_g": row(q_norm_g), "k_norm_g": row(k_norm_g),
        "lru_out_g": row(lru_out_g), "sb_out_g": row(sb_out_g), "w_out": w_out,
        "norm_ffn_g": row(norm_ffn_g), "w_ffn_up": w_ffn_up,
        "ffn_conv_w": pad_rows(ffn_conv_w), "ffn_conv_b": row(ffn_conv_b),
        "w_ffn_down": w_ffn_down,
    }


def kernel(x_prompt, x_sample, cache_sb_k, cache_sb_v, state_lru_h, state_lru_conv, state_ffn_conv, norm_mix_g, w_in, lru_conv_w, lru_conv_b, lru_w_a, lru_b_a, lru_w_x, lru_b_x, lru_lambda, q_norm_g, k_norm_g, lru_out_g, sb_out_g, w_out, norm_ffn_g, w_ffn_up, ffn_conv_w, ffn_conv_b, w_ffn_down):
    depth = w_in.shape[0]
    bp = x_prompt.shape[0]
    tri = jnp.tri(SB_TILE, k=-1, dtype=BF16)
    xp, xs = x_prompt, x_sample
    st_p, st_s = [], []
    for l in range(depth):
        wts = _prep_weights(norm_mix_g[l], w_in[l], lru_conv_w[l], lru_conv_b[l], lru_w_a[l], lru_b_a[l],
                            lru_w_x[l], lru_b_x[l], lru_lambda[l], q_norm_g[l], k_norm_g[l], lru_out_g[l],
                            sb_out_g[l], w_out[l], norm_ffn_g[l], w_ffn_up[l], ffn_conv_w[l],
                            ffn_conv_b[l], w_ffn_down[l])
        xs, *ss, wb = _layer(xs, cache_sb_k[l], cache_sb_v[l], state_lru_h[l], state_lru_conv[l],
                             state_ffn_conv[l], wts, tri, tm=xs.shape[0] * xs.shape[1],
                             tt_lru=xs.shape[1], tt_ffn=xs.shape[1], emit=True)
        st_s.append(ss)
        xp, *sp, _ = _layer(xp, None, None,
                            jnp.zeros((bp, D_LRU), F32),
                            jnp.zeros((bp, LRU_CONV - 1, D_LRU), F32),
                            jnp.zeros((bp, FFN_CONV - 1, D_FF), F32),
                            dict(wts, **wb), tri, tm=512, tt_lru=128, tt_ffn=256)
        st_p.append(sp)
    stack = lambda sts, i: jnp.stack([s[i] for s in sts])
    return (xp, xs) + tuple(stack(st_p, i) for i in range(5)) + tuple(stack(st_s, i) for i in range(5))
```
